```python
import math
import jax, jax.numpy as jnp
from jax import lax
import numpy as np

D_MODEL = 2048
BATCH = 4
SEQ = 2048
DEPTH = 1

MEM_LEN = 256
DA_HEADS = 4
DA_HEAD_DIM = 128
DA_V_DIM = 2 * DA_HEAD_DIM
RET_HEADS = 4
RET_QK_DIM = 128
RET_V_DIM = 256
MIX_WIDTH = DA_HEADS * DA_V_DIM + RET_HEADS * RET_V_DIM
COL_SIZES = (
    DA_HEADS * 2 * DA_HEAD_DIM,
    DA_HEADS * 2 * DA_HEAD_DIM,
    DA_HEADS * DA_V_DIM,
    RET_HEADS * RET_QK_DIM,
    RET_HEADS * RET_QK_DIM,
    RET_HEADS * RET_V_DIM,
    RET_HEADS * RET_V_DIM,
)
IN_COLS = sum(COL_SIZES)
XATTN_HEADS = 4
XATTN_HEAD_DIM = D_MODEL // XATTN_HEADS
D_FF = ((8 * D_MODEL + 3 * 256 - 1) // (3 * 256)) * 256
BLOCK = 128
CHUNK = 128
NORM_EPS = 1e-6
NEG_INF = -1e30

kernel_name = "hybrid_diffattn_retention_block"


def rmsnorm(x, g=None, eps=NORM_EPS):
    x32 = x.astype(jnp.float32)
    y = x32 * lax.rsqrt(jnp.mean(x32 * x32, axis=-1, keepdims=True) + eps)
    if g is not None:
        y = y * g.astype(jnp.float32)
    return y.astype(x.dtype)


def alibi_slopes(n_heads):
    return jnp.asarray(2.0 ** (-8.0 * np.arange(1, n_heads + 1) / n_heads), dtype=jnp.float32)


def retention_log_gammas(n_heads):
    return jnp.asarray(np.log(1.0 - 2.0 ** (-5.0 - np.arange(n_heads))), dtype=jnp.float32)


def split_cols(p):
    out, start = [], 0
    for size in COL_SIZES:
        out.append(p[..., start:start + size])
        start += size
    return out


def diff_attention(q, k, v, lam, slopes):
    b, s, h, _, d = q.shape
    nb = s // BLOCK
    scale = d ** -0.5
    qb = q.reshape(b, nb, BLOCK, h, 2, d).transpose(1, 0, 2, 3, 4, 5)
    kpos = jnp.arange(s)
    lam32 = lam.astype(jnp.float32)

    def one_block(args):
        qi, i = args
        qpos = i * BLOCK + jnp.arange(BLOCK)
        sc = jnp.einsum('bqhcd,bkhcd->bhcqk', qi, k).astype(jnp.float32) * scale
        dist = (qpos[:, None] - kpos[None, :]).astype(jnp.float32)
        bias = -slopes[:, None, None, None] * dist
        sc = jnp.where(dist >= 0, sc + bias, NEG_INF)
        p = jax.nn.softmax(sc, axis=-1)
        a = p[:, :, 0] - lam32 * p[:, :, 1]
        return jnp.einsum('bhqk,bkhe->bqhe', a.astype(v.dtype), v)

    out = lax.map(one_block, (qb, jnp.arange(nb)))
    return out.transpose(1, 0, 2, 3, 4).reshape(b, s, h, v.shape[-1])


def retention(q, k, v, log_gamma):
    b, s, h, dk = q.shape
    dv = v.shape[-1]
    n = s // CHUNK
    f32 = jnp.float32
    qc = q.astype(f32).reshape(b, n, CHUNK, h, dk).transpose(1, 0, 2, 3, 4)
    kc = (k.astype(f32) * dk ** -0.5).reshape(b, n, CHUNK, h, dk).transpose(1, 0, 2, 3, 4)
    vc = v.astype(f32).reshape(b, n, CHUNK, h, dv).transpose(1, 0, 2, 3, 4)
    idx = jnp.arange(CHUNK, dtype=f32)
    diff = idx[:, None] - idx[None, :]
    intra = jnp.where(diff >= 0, jnp.exp(log_gamma[:, None, None] * jnp.maximum(diff, 0.0)), 0.0)
    q_decay = jnp.exp(log_gamma[None, :] * (idx[:, None] + 1.0))[None, :, :, None]
    k_decay = jnp.exp(log_gamma[None, :] * (CHUNK - 1.0 - idx[:, None]))[None, :, :, None]
    chunk_decay = jnp.exp(log_gamma * CHUNK)[None, :, None, None]

    def step(state, inp):
        qi, ki, vi = inp
        scores = jnp.einsum('bihd,bjhd->bhij', qi, ki) * intra
        inner = jnp.einsum('bhij,bjhe->bihe', scores, vi)
        cross = jnp.einsum('bihd,bhde->bihe', qi, state) * q_decay
        new_state = state * chunk_decay + jnp.einsum('bjhd,bjhe->bhde', ki * k_decay, vi)
        return new_state, inner + cross

    state0 = jnp.zeros((b, h, dk, dv), f32)
    _, out = lax.scan(step, state0, (qc, kc, vc))
    return out.transpose(1, 0, 2, 3, 4).reshape(b, s, h, dv).astype(q.dtype)


def setup_inputs(seed: int = 0) -> dict:
    key = jax.random.key(seed)
    ks = jax.random.split(key, 24)
    f32 = jnp.float32

    def w(k, shape, fan_in):
        return jax.random.normal(k, shape, f32) * fan_in ** -0.5

    def gain(k, shape):
        return 1.0 + 0.02 * jax.random.normal(k, shape, f32)

    return {
        "x": jax.random.normal(ks[0], (BATCH, SEQ, D_MODEL), f32),
        "mem": jax.random.normal(ks[1], (BATCH, MEM_LEN, D_MODEL), f32),
        "norm_mix_g": gain(ks[2], (DEPTH, D_MODEL)),
        "w_in": w(ks[3], (DEPTH, D_MODEL, IN_COLS), D_MODEL),
        "lambda_q1": 0.1 * jax.random.normal(ks[4], (DEPTH, DA_HEAD_DIM), f32),
        "lambda_k1": 0.1 * jax.random.normal(ks[5], (DEPTH, DA_HEAD_DIM), f32),
        "lambda_q2": 0.1 * jax.random.normal(ks[6], (DEPTH, DA_HEAD_DIM), f32),
        "lambda_k2": 0.1 * jax.random.normal(ks[7], (DEPTH, DA_HEAD_DIM), f32),
        "da_subln_g": gain(ks[8], (DEPTH, DA_V_DIM)),
        "w_o": w(ks[9], (DEPTH, MIX_WIDTH, D_MODEL), MIX_WIDTH),
        "norm_x_g": gain(ks[10], (DEPTH, D_MODEL)),
        "norm_mem_g": gain(ks[11], (DEPTH, D_MODEL)),
        "w_xq": w(ks[12], (DEPTH, D_MODEL, D_MODEL), D_MODEL),
        "w_xk": w(ks[13], (DEPTH, D_MODEL, D_MODEL), D_MODEL),
        "w_xv": w(ks[14], (DEPTH, D_MODEL, D_MODEL), D_MODEL),
        "w_xo": w(ks[15], (DEPTH, D_MODEL, D_MODEL), D_MODEL),
        "norm_ffn_g": gain(ks[16], (DEPTH, D_MODEL)),
        "w_gate": w(ks[17], (DEPTH, D_MODEL, D_FF), D_MODEL),
        "w_up": w(ks[18], (DEPTH, D_MODEL, D_FF), D_MODEL),
        "w_down": w(ks[19], (DEPTH, D_FF, D_MODEL), D_FF),
        "norm_f_g": gain(ks[20], (D_MODEL,)),
    }


def reference(x, mem, norm_mix_g, w_in, lambda_q1, lambda_k1, lambda_q2, lambda_k2,
              da_subln_g, w_o, norm_x_g, norm_mem_g, w_xq, w_xk, w_xv, w_xo,
              norm_ffn_g, w_gate, w_up, w_down, norm_f_g):
    b, s, _ = x.shape
    slopes = alibi_slopes(DA_HEADS)
    log_gamma = retention_log_gammas(RET_HEADS)
    for l in range(DEPTH):
        lam_init = 0.8 - 0.6 * math.exp(-0.3 * l)
        h = rmsnorm(x, norm_mix_g[l])
        proj = h @ w_in[l]
        dq, dk, dv, rq, rk, rv, rg = split_cols(proj)
        lam = (jnp.exp(jnp.sum(lambda_q1[l] * lambda_k1[l]))
               - jnp.exp(jnp.sum(lambda_q2[l] * lambda_k2[l])) + lam_init)
        da = diff_attention(dq.reshape(b, s, DA_HEADS, 2, DA_HEAD_DIM),
                            dk.reshape(b, s, DA_HEADS, 2, DA_HEAD_DIM),
                            dv.reshape(b, s, DA_HEADS, DA_V_DIM), lam, slopes)
        da = rmsnorm(da, da_subln_g[l]) * (1.0 - lam_init)
        ret = retention(rq.reshape(b, s, RET_HEADS, RET_QK_DIM),
                        rk.reshape(b, s, RET_HEADS, RET_QK_DIM),
                        rv.reshape(b, s, RET_HEADS, RET_V_DIM), log_gamma)
        ret = rmsnorm(ret).reshape(b, s, RET_HEADS * RET_V_DIM) * jax.nn.silu(rg)
        mixed = jnp.concatenate([da.reshape(b, s, DA_HEADS * DA_V_DIM), ret], axis=-1)
        x = x + mixed @ w_o[l]
        hx = rmsnorm(x, norm_x_g[l])
        hm = rmsnorm(mem, norm_mem_g[l])
        xq = (hx @ w_xq[l]).reshape(b, s, XATTN_HEADS, XATTN_HEAD_DIM)
        xk = (hm @ w_xk[l]).reshape(b, MEM_LEN, XATTN_HEADS, XATTN_HEAD_DIM)
        xv = (hm @ w_xv[l]).reshape(b, MEM_LEN, XATTN_HEADS, XATTN_HEAD_DIM)
        sc = jnp.einsum('bqhd,bkhd->bhqk', xq, xk).astype(jnp.float32) * XATTN_HEAD_DIM ** -0.5
        p = jax.nn.softmax(sc, axis=-1).astype(x.dtype)
        xo = jnp.einsum('bhqk,bkhd->bqhd', p, xv).reshape(b, s, D_MODEL)
        x = x + xo @ w_xo[l]
        hf = rmsnorm(x, norm_ffn_g[l])
        x = x + (jax.nn.silu(hf @ w_gate[l]) * (hf @ w_up[l])) @ w_down[l]
    return rmsnorm(x, norm_f_g)
```

```python
import functools
import math

import jax
import jax.numpy as jnp
import numpy as np
from jax import lax
from jax.experimental import pallas as pl
from jax.experimental.pallas import tpu as pltpu

F32 = jnp.float32
BF16 = jnp.bfloat16

DA_HEADS = 4
DA_HEAD_DIM = 128
DA_V_DIM = 2 * DA_HEAD_DIM
RET_HEADS = 4
RET_QK_DIM = 128
RET_V_DIM = 256
XATTN_HEADS = 4
RET_CHUNK = 128
NORM_EPS = 1e-6
NEG_INF = -1e30

V7X_VMEM_BYTES = 64 * 1024 * 1024
V7X_VMEM_USABLE_BYTES = V7X_VMEM_BYTES - 8 * 1024 * 1024
COMPILER_SCRATCH_BYTES = 4 * 1024 * 1024


def _nbytes(shape, dtype):
    return int(np.prod(shape)) * jnp.dtype(dtype).itemsize


def _vmem_limit(pipelined, resident):
    need = 2 * sum(pipelined) + sum(resident) + COMPILER_SCRATCH_BYTES
    return int(min(V7X_VMEM_USABLE_BYTES, need))


def _rms(x):
    return x * lax.rsqrt(jnp.mean(x * x, axis=-1, keepdims=True) + NORM_EPS)


def _dot(a, b):
    return jnp.dot(a, b, preferred_element_type=F32)


def _dot_nt(a, b):
    return lax.dot_general(a, b, (((1,), (1,)), ((), ())), preferred_element_type=F32)


def _dot_tn(a, b):
    return lax.dot_general(a, b, (((0,), (0,)), ((), ())), preferred_element_type=F32)


def _norm_matmul_kernel(x_ref, g_ref, w_ref, o_ref, h_ref):
    @pl.when(pl.program_id(1) == 0)
    def _():
        h_ref[...] = (_rms(x_ref[...]) * g_ref[...]).astype(h_ref.dtype)

    o_ref[...] = _dot(h_ref[...], w_ref[...]).astype(o_ref.dtype)


def _norm_matmul(x, g, w, out_dtype, *, tm, tn, name):
    m, d = x.shape
    n = w.shape[1]
    tm, tn = min(tm, m), min(tn, n)
    limit = _vmem_limit(
        [_nbytes((tm, d), x.dtype), _nbytes((d, tn), w.dtype), _nbytes((tm, tn), out_dtype)],
        [_nbytes((tm, d), BF16), _nbytes((tm, d), F32), _nbytes((tm, tn), F32)],
    )
    return pl.pallas_call(
        _norm_matmul_kernel,
        grid=(m // tm, n // tn),
        in_specs=[
            pl.BlockSpec((tm, d), lambda i, j: (i, 0)),
            pl.BlockSpec((1, d), lambda i, j: (0, 0)),
            pl.BlockSpec((d, tn), lambda i, j: (0, j)),
        ],
        out_specs=pl.BlockSpec((tm, tn), lambda i, j: (i, j)),
        out_shape=jax.ShapeDtypeStruct((m, n), out_dtype),
        scratch_shapes=[pltpu.VMEM((tm, d), BF16)],
        compiler_params=pltpu.CompilerParams(
            dimension_semantics=("parallel", "arbitrary"), vmem_limit_bytes=limit
        ),
        name=name,
    )(x, g.reshape(1, d), w)


def _matmul2_res_kernel(a1_ref, a2_ref, w1_ref, w2_ref, res_ref, o_ref):
    acc = _dot(a1_ref[...], w1_ref[...]) + _dot(a2_ref[...], w2_ref[...])
    o_ref[...] = res_ref[...] + acc


def _matmul2_res(a1, a2, w, res, *, tm, tn, name):
    m, k1 = a1.shape
    k2 = a2.shape[1]
    assert k1 == k2 and w.shape[0] == k1 + k2
    n = w.shape[1]
    limit = _vmem_limit(
        [
            _nbytes((tm, k1), a1.dtype),
            _nbytes((tm, k2), a2.dtype),
            _nbytes((k1, tn), w.dtype),
            _nbytes((k2, tn), w.dtype),
            _nbytes((tm, tn), F32),
            _nbytes((tm, tn), F32),
        ],
        [_nbytes((tm, tn), F32)],
    )
    return pl.pallas_call(
        _matmul2_res_kernel,
        grid=(m // tm, n // tn),
        in_specs=[
            pl.BlockSpec((tm, k1), lambda i, j: (i, 0)),
            pl.BlockSpec((tm, k2), lambda i, j: (i, 0)),
            pl.BlockSpec((k1, tn), lambda i, j: (0, j)),
            pl.BlockSpec((k2, tn), lambda i, j: (1, j)),
            pl.BlockSpec((tm, tn), lambda i, j: (i, j)),
        ],
        out_specs=pl.BlockSpec((tm, tn), lambda i, j: (i, j)),
        out_shape=jax.ShapeDtypeStruct((m, n), F32),
        compiler_params=pltpu.CompilerParams(
            dimension_semantics=("parallel", "arbitrary"), vmem_limit_bytes=limit
        ),
        name=name,
    )(a1, a2, w, w, res)


def _diff_attn_kernel(slope_ref, lq1_ref, lk1_ref, lq2_ref, lk2_ref, g_ref,
                      q_ref, k_ref, v_ref, o_ref, *, tq, lam_init):
    s_len = q_ref.shape[0]
    d = DA_HEAD_DIM
    scale = d ** -0.5
    slope = slope_ref[pl.program_id(1)]
    lam = (jnp.exp(jnp.sum(lq1_ref[...] * lk1_ref[...], axis=-1, keepdims=True))
           - jnp.exp(jnp.sum(lq2_ref[...] * lk2_ref[...], axis=-1, keepdims=True))
           + lam_init)

    row = lax.broadcasted_iota(jnp.int32, (tq, tq), 0)
    col = lax.broadcasted_iota(jnp.int32, (tq, tq), 1)
    causal = col <= row
    diag_bias = slope * (col - (tq - 1)).astype(F32)
    kpos = lax.broadcasted_iota(jnp.int32, (1, s_len), 1).astype(F32)

    for qi in range(s_len // tq):
        lo, hi = qi * tq, (qi + 1) * tq
        probs = []
        for c in range(2):
            qc = q_ref[lo:hi, c * d:(c + 1) * d]
            s_d = _dot_nt(qc, k_ref[lo:hi, c * d:(c + 1) * d])
            s_d = jnp.where(causal, s_d * scale + diag_bias, NEG_INF)
            m = jnp.max(s_d, axis=-1, keepdims=True)
            if qi > 0:
                s_p = _dot_nt(qc, k_ref[0:lo, c * d:(c + 1) * d])
                s_p = s_p * scale + slope * (kpos[:, 0:lo] - float(hi - 1))
                m = jnp.maximum(m, jnp.max(s_p, axis=-1, keepdims=True))
                p_p = jnp.exp(s_p - m)
                p_d = jnp.exp(s_d - m)
                l = jnp.sum(p_p, axis=-1, keepdims=True) + jnp.sum(p_d, axis=-1, keepdims=True)
            else:
                p_p = None
                p_d = jnp.exp(s_d - m)
                l = jnp.sum(p_d, axis=-1, keepdims=True)
            probs.append((p_p, p_d, l))
        (p1_p, p1_d, l1), (p2_p, p2_d, l2) = probs
        w1 = 1.0 / l1
        w2 = lam / l2
        out = _dot((p1_d * w1 - p2_d * w2).astype(BF16), v_ref[lo:hi, :])
        if qi > 0:
            out = out + _dot((p1_p * w1 - p2_p * w2).astype(BF16), v_ref[0:lo, :])
        y = _rms(out) * g_ref[...] * (1.0 - lam_init)
        o_ref[lo:hi, :] = y.astype(o_ref.dtype)


def _diff_attention(proj, slopes, lq1, lk1, lq2, lk2, g, *, batch, seq, lam_init, tq):
    hb = DA_V_DIM
    q_blk0, k_blk0, v_blk0 = 0, DA_HEADS, 2 * DA_HEADS
    seq_bytes = _nbytes((seq, hb), proj.dtype)
    limit = _vmem_limit([seq_bytes] * 4, [8 * _nbytes((tq, seq), F32)])
    vec = pl.BlockSpec((1, DA_HEAD_DIM), lambda b, h: (0, 0))
    kernel = functools.partial(_diff_attn_kernel, tq=tq, lam_init=lam_init)
    return pl.pallas_call(
        kernel,
        grid=(batch, DA_HEADS),
        in_specs=[
            pl.BlockSpec(memory_space=pltpu.SMEM),
            vec, vec, vec, vec,
            pl.BlockSpec((1, hb), lambda b, h: (0, 0)),
            pl.BlockSpec((seq, hb), lambda b, h: (b, q_blk0 + h)),
            pl.BlockSpec((seq, hb), lambda b, h: (b, k_blk0 + h)),
            pl.BlockSpec((seq, hb), lambda b, h: (b, v_blk0 + h)),
        ],
        out_specs=pl.BlockSpec((seq, hb), lambda b, h: (b, h)),
        out_shape=jax.ShapeDtypeStruct((batch * seq, DA_HEADS * hb), BF16),
        compiler_params=pltpu.CompilerParams(
            dimension_semantics=("parallel", "parallel"), vmem_limit_bytes=limit
        ),
        name="diff_attention",
    )(slopes, lq1.reshape(1, -1), lk1.reshape(1, -1), lq2.reshape(1, -1), lk2.reshape(1, -1),
      g.reshape(1, hb), proj, proj, proj)


def _retention_kernel(lg_ref, q_ref, k_ref, v_ref, gate_ref, o_ref):
    s_len = q_ref.shape[0]
    c = RET_CHUNK
    dk, dv = RET_QK_DIM, RET_V_DIM
    scale = dk ** -0.5
    lg = lg_ref[pl.program_id(1)]

    row = lax.broadcasted_iota(jnp.int32, (c, c), 0)
    col = lax.broadcasted_iota(jnp.int32, (c, c), 1)
    diff = (row - col).astype(F32)
    intra = jnp.where(diff >= 0, jnp.exp(lg * jnp.maximum(diff, 0.0)), 0.0) * scale
    row_k = lax.broadcasted_iota(jnp.int32, (c, dk), 0).astype(F32)
    k_decay = jnp.exp(lg * (float(c - 1) - row_k)) * scale
    row_v = lax.broadcasted_iota(jnp.int32, (c, dv), 0).astype(F32)
    q_decay = jnp.exp(lg * (row_v + 1.0))
    chunk_decay = jnp.exp(jnp.full((1, dv), lg * float(c), F32))

    state = jnp.zeros((dk, dv), F32)
    for i in range(s_len // c):
        lo, hi = i * c, (i + 1) * c
        q = q_ref[lo:hi, :]
        k = k_ref[lo:hi, :]
        v = v_ref[lo:hi, :]
        scores = _dot_nt(q, k) * intra
        y = _dot(scores.astype(BF16), v)
        if i > 0:
            y = y + _dot(q, state.astype(BF16)) * q_decay
        if i + 1 < s_len // c:
            kd = (k.astype(F32) * k_decay).astype(BF16)
            state = state * chunk_decay + _dot_tn(kd, v)
        gate = gate_ref[lo:hi, :].astype(F32)
        o_ref[lo:hi, :] = (_rms(y) * (gate * jax.nn.sigmoid(gate))).astype(o_ref.dtype)


def _retention(proj, log_gammas, *, batch, seq):
    da_cols = 3 * DA_HEADS * DA_V_DIM
    q_blk0 = da_cols // RET_QK_DIM
    k_blk0 = q_blk0 + RET_HEADS
    v_blk0 = (da_cols + 2 * RET_HEADS * RET_QK_DIM) // RET_V_DIM
    g_blk0 = v_blk0 + RET_HEADS
    limit = _vmem_limit(
        [_nbytes((seq, RET_QK_DIM), proj.dtype)] * 2 + [_nbytes((seq, RET_V_DIM), proj.dtype)] * 3,
        [16 * _nbytes((RET_CHUNK, RET_V_DIM), F32) * (seq // RET_CHUNK)],
    )
    return pl.pallas_call(
        _retention_kernel,
        grid=(batch, RET_HEADS),
        in_specs=[
            pl.BlockSpec(memory_space=pltpu.SMEM),
            pl.BlockSpec((seq, RET_QK_DIM), lambda b, h: (b, q_blk0 + h)),
            pl.BlockSpec((seq, RET_QK_DIM), lambda b, h: (b, k_blk0 + h)),
            pl.BlockSpec((seq, RET_V_DIM), lambda b, h: (b, v_blk0 + h)),
            pl.BlockSpec((seq, RET_V_DIM), lambda b, h: (b, g_blk0 + h)),
        ],
        out_specs=pl.BlockSpec((seq, RET_V_DIM), lambda b, h: (b, h)),
        out_shape=jax.ShapeDtypeStruct((batch * seq, RET_HEADS * RET_V_DIM), BF16),
        compiler_params=pltpu.CompilerParams(
            dimension_semantics=("parallel", "parallel"), vmem_limit_bytes=limit
        ),
        name="retention",
    )(log_gammas, proj, proj, proj, proj)


def _xattn_kernel(xq_ref, xk_ref, xv_ref, res_ref, wo_ref, o_ref, xo_ref):
    d_model = xq_ref.shape[1]
    hd = d_model // XATTN_HEADS
    scale = hd ** -0.5
    for h in range(XATTN_HEADS):
        cols = slice(h * hd, (h + 1) * hd)
        s = _dot_nt(xq_ref[:, cols], xk_ref[:, cols]) * scale
        p = jnp.exp(s - jnp.max(s, axis=-1, keepdims=True))
        p = p / jnp.sum(p, axis=-1, keepdims=True)
        xo_ref[:, cols] = _dot(p.astype(BF16), xv_ref[:, cols]).astype(xo_ref.dtype)
    o_ref[...] = res_ref[...] + _dot(xo_ref[...], wo_ref[...])


def _cross_attention(xq, xk, xv, res, wo, *, batch, seq, mem_len, tq):
    d = xq.shape[1]
    nq = seq // tq
    limit = _vmem_limit(
        [
            _nbytes((tq, d), xq.dtype),
            _nbytes((mem_len, d), xk.dtype),
            _nbytes((mem_len, d), xv.dtype),
            _nbytes((tq, d), F32),
            _nbytes((d, d), wo.dtype),
            _nbytes((tq, d), F32),
        ],
        [_nbytes((tq, d), BF16), _nbytes((tq, d), F32)],
    )
    return pl.pallas_call(
        _xattn_kernel,
        grid=(batch, nq),
        in_specs=[
            pl.BlockSpec((tq, d), lambda b, i: (b * nq + i, 0)),
            pl.BlockSpec((mem_len, d), lambda b, i: (b, 0)),
            pl.BlockSpec((mem_len, d), lambda b, i: (b, 0)),
            pl.BlockSpec((tq, d), lambda b, i: (b * nq + i, 0)),
            pl.BlockSpec((d, d), lambda b, i: (0, 0)),
        ],
        out_specs=pl.BlockSpec((tq, d), lambda b, i: (b * nq + i, 0)),
        out_shape=jax.ShapeDtypeStruct((batch * seq, d), F32),
        scratch_shapes=[pltpu.VMEM((tq, d), BF16)],
        compiler_params=pltpu.CompilerParams(
            dimension_semantics=("parallel", "arbitrary"), vmem_limit_bytes=limit
        ),
        name="cross_attention",
    )(xq, xk, xv, res, wo)


def _ffn_kernel(x_ref, g_ref, wg_ref, wu_ref, wd_ref, gf_ref, o_ref, h_ref, *, final_norm):
    f = pl.program_id(1)

    @pl.when(f == 0)
    def _():
        x = x_ref[...]
        h_ref[...] = (_rms(x) * g_ref[...]).astype(h_ref.dtype)
        o_ref[...] = x

    h = h_ref[...]
    gate = _dot(h, wg_ref[...])
    up = _dot(h, wu_ref[...])
    act = (gate * jax.nn.sigmoid(gate)) * up
    o_ref[...] += _dot(act.astype(BF16), wd_ref[...])

    if final_norm:
        @pl.when(f == pl.num_programs(1) - 1)
        def _():
            o_ref[...] = _rms(o_ref[...]) * gf_ref[...]


def _ffn(x, g, wg, wu, wd, gf, *, final_norm, tm, tf):
    m, d = x.shape
    d_ff = wg.shape[1]
    limit = _vmem_limit(
        [
            _nbytes((tm, d), F32),
            _nbytes((d, tf), wg.dtype),
            _nbytes((d, tf), wu.dtype),
            _nbytes((tf, d), wd.dtype),
            _nbytes((tm, d), F32),
        ],
        [_nbytes((tm, d), BF16), _nbytes((tm, d), F32), 4 * _nbytes((tm, tf), F32)],
    )
    return pl.pallas_call(
        functools.partial(_ffn_kernel, final_norm=final_norm),
        grid=(m // tm, d_ff // tf),
        in_specs=[
            pl.BlockSpec((tm, d), lambda i, f: (i, 0)),
            pl.BlockSpec((1, d), lambda i, f: (0, 0)),
            pl.BlockSpec((d, tf), lambda i, f: (0, f)),
            pl.BlockSpec((d, tf), lambda i, f: (0, f)),
            pl.BlockSpec((tf, d), lambda i, f: (f, 0)),
            pl.BlockSpec((1, d), lambda i, f: (0, 0)),
        ],
        out_specs=pl.BlockSpec((tm, d), lambda i, f: (i, 0)),
        out_shape=jax.ShapeDtypeStruct((m, d), F32),
        scratch_shapes=[pltpu.VMEM((tm, d), BF16)],
        compiler_params=pltpu.CompilerParams(
            dimension_semantics=("parallel", "arbitrary"), vmem_limit_bytes=limit
        ),
        name="swiglu_ffn",
    )(x, g.reshape(1, d), wg, wu, wd, gf.reshape(1, d))


def kernel(x, mem, norm_mix_g, w_in, lambda_q1, lambda_k1, lambda_q2, lambda_k2, da_subln_g, w_o, norm_x_g, norm_mem_g, w_xq, w_xk, w_xv, w_xo, norm_ffn_g, w_gate, w_up, w_down, norm_f_g):
    batch, seq, d_model = x.shape
    mem_len = mem.shape[1]
    depth = w_in.shape[0]
    slopes = jnp.asarray(2.0 ** (-8.0 * np.arange(1, DA_HEADS + 1) / DA_HEADS), dtype=F32)
    log_gammas = jnp.asarray(np.log(1.0 - 2.0 ** (-5.0 - np.arange(RET_HEADS))), dtype=F32)

    xf = x.reshape(batch * seq, d_model)
    memf = mem.reshape(batch * mem_len, d_model)
    for l in range(depth):
        lam_init = 0.8 - 0.6 * math.exp(-0.3 * l)
        bf = lambda w: w[l].astype(BF16)

        proj = _norm_matmul(xf, norm_mix_g[l], bf(w_in), BF16, tm=1024, tn=1024, name="in_proj")
        da = _diff_attention(proj, slopes, lambda_q1[l], lambda_k1[l], lambda_q2[l], lambda_k2[l],
                             da_subln_g[l], batch=batch, seq=seq, lam_init=lam_init, tq=256)
        ret = _retention(proj, log_gammas, batch=batch, seq=seq)
        xf = _matmul2_res(da, ret, bf(w_o), xf, tm=1024, tn=1024, name="out_proj")

        xq = _norm_matmul(xf, norm_x_g[l], bf(w_xq), BF16, tm=1024, tn=1024, name="xattn_q")
        xk = _norm_matmul(memf, norm_mem_g[l], bf(w_xk), BF16, tm=1024, tn=1024, name="xattn_k")
        xv = _norm_matmul(memf, norm_mem_g[l], bf(w_xv), BF16, tm=1024, tn=1024, name="xattn_v")
        xf = _cross_attention(xq, xk, xv, xf, bf(w_xo), batch=batch, seq=seq, mem_len=mem_len, tq=512)

        xf = _ffn(xf, norm_ffn_g[l], bf(w_gate), bf(w_up), bf(w_down), norm_f_g,
                  final_norm=(l == depth - 1), tm=512, tf=512)
    return xf.reshape(batch, seq, d_model)
```

```python
import functools
import math

import jax
import jax.numpy as jnp
import numpy as np
from jax import lax
from jax.experimental import pallas as pl
from jax.experimental.pallas import tpu as pltpu

F32 = jnp.float32
BF16 = jnp.bfloat16

DA_HEADS = 4
DA_HEAD_DIM = 128
DA_V_DIM = 2 * DA_HEAD_DIM
RET_HEADS = 4
RET_QK_DIM = 128
RET_V_DIM = 256
XATTN_HEADS = 4
RET_CHUNK = 128
NORM_EPS = 1e-6
NEG_INF = -1e30
LOG2_E = math.log2(math.e)
NORM_BLOCK_ROWS = 256

V7X_LANES = 128
V7X_VMEM_BYTES = 64 * 1024 * 1024
V7X_VMEM_USABLE_BYTES = V7X_VMEM_BYTES - 8 * 1024 * 1024
COMPILER_SCRATCH_BYTES = 4 * 1024 * 1024


def _nbytes(shape, dtype):
    return int(np.prod(shape)) * jnp.dtype(dtype).itemsize


def _vmem_limit(pipelined, resident):
    need = 2 * sum(pipelined) + sum(resident) + COMPILER_SCRATCH_BYTES
    return int(min(V7X_VMEM_USABLE_BYTES, need))


def _rms(x):
    return x * lax.rsqrt(jnp.mean(x * x, axis=-1, keepdims=True) + NORM_EPS)


def _fold_lanes(x, op):
    tiles = [x[:, i:i + V7X_LANES] for i in range(0, x.shape[1], V7X_LANES)]
    return functools.reduce(op, tiles)


def _dot(a, b):
    return jnp.dot(a, b, preferred_element_type=F32)


def _dot_nt(a, b):
    return lax.dot_general(a, b, (((1,), (1,)), ((), ())), preferred_element_type=F32)


def _dot_tn(a, b):
    return lax.dot_general(a, b, (((0,), (0,)), ((), ())), preferred_element_type=F32)


def _row_blocks(n_rows):
    step = min(NORM_BLOCK_ROWS, n_rows)
    return [slice(r, r + step) for r in range(0, n_rows, step)]


def _norm_matmul_kernel(x_ref, g_ref, w_ref, o_ref, h_ref):
    j = pl.program_id(1)

    @pl.when(j == 0)
    def _():
        for rows in _row_blocks(x_ref.shape[0]):
            h = (_rms(x_ref[rows, :]) * g_ref[...]).astype(h_ref.dtype)
            h_ref[rows, :] = h
            o_ref[rows, :] = _dot(h, w_ref[...]).astype(o_ref.dtype)

    @pl.when(j > 0)
    def _():
        o_ref[...] = _dot(h_ref[...], w_ref[...]).astype(o_ref.dtype)


def _norm_matmul(x, g, w, out_dtype, *, tm, tn, name):
    m, d = x.shape
    n = w.shape[1]
    tm, tn = min(tm, m), min(tn, n)
    limit = _vmem_limit(
        [_nbytes((tm, d), x.dtype), _nbytes((d, tn), w.dtype), _nbytes((tm, tn), out_dtype)],
        [_nbytes((tm, d), BF16), _nbytes((tm, d), F32), _nbytes((tm, tn), F32)],
    )
    return pl.pallas_call(
        _norm_matmul_kernel,
        grid=(m // tm, n // tn),
        in_specs=[
            pl.BlockSpec((tm, d), lambda i, j: (i, 0)),
            pl.BlockSpec((1, d), lambda i, j: (0, 0)),
            pl.BlockSpec((d, tn), lambda i, j: (0, j)),
        ],
        out_specs=pl.BlockSpec((tm, tn), lambda i, j: (i, j)),
        out_shape=jax.ShapeDtypeStruct((m, n), out_dtype),
        scratch_shapes=[pltpu.VMEM((tm, d), BF16)],
        compiler_params=pltpu.CompilerParams(
            dimension_semantics=("parallel", "arbitrary"), vmem_limit_bytes=limit
        ),
        name=name,
    )(x, g.reshape(1, d), w)


def _matmul2_res_kernel(a1_ref, a2_ref, w1_ref, w2_ref, res_ref, o_ref):
    acc = _dot(a1_ref[...], w1_ref[...]) + _dot(a2_ref[...], w2_ref[...])
    o_ref[...] = res_ref[...] + acc


def _matmul2_res(a1, a2, w, res, *, tm, tn, name):
    m, k1 = a1.shape
    k2 = a2.shape[1]
    assert k1 == k2 and w.shape[0] == k1 + k2
    n = w.shape[1]
    limit = _vmem_limit(
        [
            _nbytes((tm, k1), a1.dtype),
            _nbytes((tm, k2), a2.dtype),
            _nbytes((k1, tn), w.dtype),
            _nbytes((k2, tn), w.dtype),
            _nbytes((tm, tn), F32),
            _nbytes((tm, tn), F32),
        ],
        [_nbytes((tm, tn), F32)],
    )
    return pl.pallas_call(
        _matmul2_res_kernel,
        grid=(m // tm, n // tn),
        in_specs=[
            pl.BlockSpec((tm, k1), lambda i, j: (i, 0)),
            pl.BlockSpec((tm, k2), lambda i, j: (i, 0)),
            pl.BlockSpec((k1, tn), lambda i, j: (0, j)),
            pl.BlockSpec((k2, tn), lambda i, j: (1, j)),
            pl.BlockSpec((tm, tn), lambda i, j: (i, j)),
        ],
        out_specs=pl.BlockSpec((tm, tn), lambda i, j: (i, j)),
        out_shape=jax.ShapeDtypeStruct((m, n), F32),
        compiler_params=pltpu.CompilerParams(
            dimension_semantics=("parallel", "arbitrary"), vmem_limit_bytes=limit
        ),
        name=name,
    )(a1, a2, w, w, res)


def _diff_attn_kernel(slope_ref, lq1_ref, lk1_ref, lq2_ref, lk2_ref, g_ref,
                      q_ref, k_ref, v_ref, o_ref, s_ref, p_ref, *, tq, lam_init):
    s_len = q_ref.shape[0]
    d = DA_HEAD_DIM
    scale2 = d ** -0.5 * LOG2_E
    slope2 = slope_ref[pl.program_id(1)] * LOG2_E
    lam = (jnp.exp(jnp.sum(lq1_ref[...] * lk1_ref[...], axis=-1, keepdims=True))
           - jnp.exp(jnp.sum(lq2_ref[...] * lk2_ref[...], axis=-1, keepdims=True))
           + lam_init)

    row = lax.broadcasted_iota(jnp.int32, (tq, tq), 0)
    col = lax.broadcasted_iota(jnp.int32, (tq, tq), 1)
    causal = col <= row
    diag_bias = slope2 * (col - (tq - 1)).astype(F32)
    kpos = lax.broadcasted_iota(jnp.int32, (1, s_len), 1).astype(F32)

    for qi in range(s_len // tq):
        lo, hi = qi * tq, (qi + 1) * tq
        key_blocks = [slice(j * tq, (j + 1) * tq) for j in range(qi + 1)]
        heads = []
        for c in range(2):
            dcols = slice(c * d, (c + 1) * d)
            qc = q_ref[lo:hi, dcols]
            m = None
            for j, cols in enumerate(key_blocks):
                s = _dot_nt(qc, k_ref[cols, dcols]) * scale2
                if j == qi:
                    s = jnp.where(causal, s + diag_bias, NEG_INF)
                else:
                    s = s + slope2 * (kpos[:, cols] - float(hi - 1))
                s_ref[c, :, cols] = s
                bm = _fold_lanes(s, jnp.maximum)
                m = bm if m is None else jnp.maximum(m, bm)
            m = jnp.max(m, axis=-1, keepdims=True)
            l = None
            for cols in key_blocks:
                p = jnp.exp2(s_ref[c, :, cols] - m)
                bl = _fold_lanes(p, jnp.add)
                l = bl if l is None else l + bl
                p_ref[c, :, cols] = p.astype(p_ref.dtype)
            l = jnp.sum(l, axis=-1, keepdims=True)
            acc = _dot(p_ref[c, :, 0:hi], v_ref[0:hi, :])
            heads.append((acc, l))
        (acc1, l1), (acc2, l2) = heads
        out = acc1 * (1.0 / l1) - acc2 * (lam / l2)
        y = _rms(out) * g_ref[...] * (1.0 - lam_init)
        o_ref[lo:hi, :] = y.astype(o_ref.dtype)


def _diff_attention(proj, slopes, lq1, lk1, lq2, lk2, g, *, batch, seq, lam_init, tq):
    hb = DA_V_DIM
    q_blk0, k_blk0, v_blk0 = 0, DA_HEADS, 2 * DA_HEADS
    seq_bytes = _nbytes((seq, hb), proj.dtype)
    scratch = [pltpu.VMEM((2, tq, seq), F32), pltpu.VMEM((2, tq, seq), BF16)]
    limit = _vmem_limit(
        [seq_bytes] * 4,
        [3 * _nbytes((2, tq, seq), F32), 3 * _nbytes((2, tq, seq), BF16)],
    )
    vec = pl.BlockSpec((1, DA_HEAD_DIM), lambda b, h: (0, 0))
    kernel = functools.partial(_diff_attn_kernel, tq=tq, lam_init=lam_init)
    return pl.pallas_call(
        kernel,
        grid=(batch, DA_HEADS),
        in_specs=[
            pl.BlockSpec(memory_space=pltpu.SMEM),
            vec, vec, vec, vec,
            pl.BlockSpec((1, hb), lambda b, h: (0, 0)),
            pl.BlockSpec((seq, hb), lambda b, h: (b, q_blk0 + h)),
            pl.BlockSpec((seq, hb), lambda b, h: (b, k_blk0 + h)),
            pl.BlockSpec((seq, hb), lambda b, h: (b, v_blk0 + h)),
        ],
        out_specs=pl.BlockSpec((seq, hb), lambda b, h: (b, h)),
        out_shape=jax.ShapeDtypeStruct((batch * seq, DA_HEADS * hb), BF16),
        scratch_shapes=scratch,
        compiler_params=pltpu.CompilerParams(
            dimension_semantics=("parallel", "parallel"), vmem_limit_bytes=limit
        ),
        name="diff_attention",
    )(slopes, lq1.reshape(1, -1), lk1.reshape(1, -1), lq2.reshape(1, -1), lk2.reshape(1, -1),
      g.reshape(1, hb), proj, proj, proj)


def _retention_kernel(lg_ref, q_ref, k_ref, v_ref, gate_ref, o_ref):
    s_len = q_ref.shape[0]
    c = RET_CHUNK
    dk, dv = RET_QK_DIM, RET_V_DIM
    scale = dk ** -0.5
    lg = lg_ref[pl.program_id(1)]

    row = lax.broadcasted_iota(jnp.int32, (c, c), 0)
    col = lax.broadcasted_iota(jnp.int32, (c, c), 1)
    diff = (row - col).astype(F32)
    intra = jnp.where(diff >= 0, jnp.exp(lg * jnp.maximum(diff, 0.0)), 0.0) * scale
    row_k = lax.broadcasted_iota(jnp.int32, (c, dk), 0).astype(F32)
    k_decay = jnp.exp(lg * (float(c - 1) - row_k)) * scale
    row_v = lax.broadcasted_iota(jnp.int32, (c, dv), 0).astype(F32)
    q_decay = jnp.exp(lg * (row_v + 1.0))
    chunk_decay = jnp.exp(jnp.full((1, dv), lg * float(c), F32))

    state = jnp.zeros((dk, dv), F32)
    for i in range(s_len // c):
        lo, hi = i * c, (i + 1) * c
        q = q_ref[lo:hi, :]
        k = k_ref[lo:hi, :]
        v = v_ref[lo:hi, :]
        scores = _dot_nt(q, k) * intra
        y = _dot(scores.astype(BF16), v)
        if i > 0:
            y = y + _dot(q, state.astype(BF16)) * q_decay
        if i + 1 < s_len // c:
            kd = (k.astype(F32) * k_decay).astype(BF16)
            state = state * chunk_decay + _dot_tn(kd, v)
        gate = gate_ref[lo:hi, :].astype(F32)
        o_ref[lo:hi, :] = (_rms(y) * (gate * jax.nn.sigmoid(gate))).astype(o_ref.dtype)


def _retention(proj, log_gammas, *, batch, seq):
    da_cols = 3 * DA_HEADS * DA_V_DIM
    q_blk0 = da_cols // RET_QK_DIM
    k_blk0 = q_blk0 + RET_HEADS
    v_blk0 = (da_cols + 2 * RET_HEADS * RET_QK_DIM) // RET_V_DIM
    g_blk0 = v_blk0 + RET_HEADS
    limit = _vmem_limit(
        [_nbytes((seq, RET_QK_DIM), proj.dtype)] * 2 + [_nbytes((seq, RET_V_DIM), proj.dtype)] * 3,
        [16 * _nbytes((RET_CHUNK, RET_V_DIM), F32) * (seq // RET_CHUNK)],
    )
    return pl.pallas_call(
        _retention_kernel,
        grid=(batch, RET_HEADS),
        in_specs=[
            pl.BlockSpec(memory_space=pltpu.SMEM),
            pl.BlockSpec((seq, RET_QK_DIM), lambda b, h: (b, q_blk0 + h)),
            pl.BlockSpec((seq, RET_QK_DIM), lambda b, h: (b, k_blk0 + h)),
            pl.BlockSpec((seq, RET_V_DIM), lambda b, h: (b, v_blk0 + h)),
            pl.BlockSpec((seq, RET_V_DIM), lambda b, h: (b, g_blk0 + h)),
        ],
        out_specs=pl.BlockSpec((seq, RET_V_DIM), lambda b, h: (b, h)),
        out_shape=jax.ShapeDtypeStruct((batch * seq, RET_HEADS * RET_V_DIM), BF16),
        compiler_params=pltpu.CompilerParams(
            dimension_semantics=("parallel", "parallel"), vmem_limit_bytes=limit
        ),
        name="retention",
    )(log_gammas, proj, proj, proj, proj)


def _xattn_kernel(xq_ref, xk_ref, xv_ref, res_ref, wo_ref, o_ref, xo_ref):
    d_model = xq_ref.shape[1]
    hd = d_model // XATTN_HEADS
    scale2 = hd ** -0.5 * LOG2_E
    for h in range(XATTN_HEADS):
        cols = slice(h * hd, (h + 1) * hd)
        s = _dot_nt(xq_ref[:, cols], xk_ref[:, cols]) * scale2
        p = jnp.exp2(s - jnp.max(s, axis=-1, keepdims=True))
        p = p * (1.0 / jnp.sum(p, axis=-1, keepdims=True))
        xo_ref[:, cols] = _dot(p.astype(BF16), xv_ref[:, cols]).astype(xo_ref.dtype)
    o_ref[...] = res_ref[...] + _dot(xo_ref[...], wo_ref[...])


def _cross_attention(xq, xk, xv, res, wo, *, batch, seq, mem_len, tq):
    d = xq.shape[1]
    nq = seq // tq
    limit = _vmem_limit(
        [
            _nbytes((tq, d), xq.dtype),
            _nbytes((mem_len, d), xk.dtype),
            _nbytes((mem_len, d), xv.dtype),
            _nbytes((tq, d), F32),
            _nbytes((d, d), wo.dtype),
            _nbytes((tq, d), F32),
        ],
        [_nbytes((tq, d), BF16), _nbytes((tq, d), F32)],
    )
    return pl.pallas_call(
        _xattn_kernel,
        grid=(batch, nq),
        in_specs=[
            pl.BlockSpec((tq, d), lambda b, i: (b * nq + i, 0)),
            pl.BlockSpec((mem_len, d), lambda b, i: (b, 0)),
            pl.BlockSpec((mem_len, d), lambda b, i: (b, 0)),
            pl.BlockSpec((tq, d), lambda b, i: (b * nq + i, 0)),
            pl.BlockSpec((d, d), lambda b, i: (0, 0)),
        ],
        out_specs=pl.BlockSpec((tq, d), lambda b, i: (b * nq + i, 0)),
        out_shape=jax.ShapeDtypeStruct((batch * seq, d), F32),
        scratch_shapes=[pltpu.VMEM((tq, d), BF16)],
        compiler_params=pltpu.CompilerParams(
            dimension_semantics=("parallel", "arbitrary"), vmem_limit_bytes=limit
        ),
        name="cross_attention",
    )(xq, xk, xv, res, wo)


def _ffn_kernel(x_ref, g_ref, wg_ref, wu_ref, wd_ref, gf_ref, o_ref, h_ref, *, final_norm):
    f = pl.program_id(1)
    last = pl.num_programs(1) - 1
    blocks = _row_blocks(x_ref.shape[0])

    def partial_ffn(h):
        gate = _dot(h, wg_ref[...])
        up = _dot(h, wu_ref[...])
        act = (gate * jax.nn.sigmoid(gate)) * up
        return _dot(act.astype(BF16), wd_ref[...])

    @pl.when(f == 0)
    def _():
        for rows in blocks:
            x = x_ref[rows, :]
            h = (_rms(x) * g_ref[...]).astype(h_ref.dtype)
            h_ref[rows, :] = h
            o_ref[rows, :] = x + partial_ffn(h)

    if final_norm:
        @pl.when(jnp.logical_and(f > 0, f < last))
        def _():
            o_ref[...] += partial_ffn(h_ref[...])

        @pl.when(f == last)
        def _():
            for rows in blocks:
                y = o_ref[rows, :] + partial_ffn(h_ref[rows, :])
                o_ref[rows, :] = _rms(y) * gf_ref[...]
    else:
        @pl.when(f > 0)
        def _():
            o_ref[...] += partial_ffn(h_ref[...])


def _ffn(x, g, wg, wu, wd, gf, *, final_norm, tm, tf):
    m, d = x.shape
    d_ff = wg.shape[1]
    limit = _vmem_limit(
        [
            _nbytes((tm, d), F32),
            _nbytes((d, tf), wg.dtype),
            _nbytes((d, tf), wu.dtype),
            _nbytes((tf, d), wd.dtype),
            _nbytes((tm, d), F32),
        ],
        [_nbytes((tm, d), BF16), _nbytes((tm, d), F32), 4 * _nbytes((tm, tf), F32)],
    )
    return pl.pallas_call(
        functools.partial(_ffn_kernel, final_norm=final_norm),
        grid=(m // tm, d_ff // tf),
        in_specs=[
            pl.BlockSpec((tm, d), lambda i, f: (i, 0)),
            pl.BlockSpec((1, d), lambda i, f: (0, 0)),
            pl.BlockSpec((d, tf), lambda i, f: (0, f)),
            pl.BlockSpec((d, tf), lambda i, f: (0, f)),
            pl.BlockSpec((tf, d), lambda i, f: (f, 0)),
            pl.BlockSpec((1, d), lambda i, f: (0, 0)),
        ],
        out_specs=pl.BlockSpec((tm, d), lambda i, f: (i, 0)),
        out_shape=jax.ShapeDtypeStruct((m, d), F32),
        scratch_shapes=[pltpu.VMEM((tm, d), BF16)],
        compiler_params=pltpu.CompilerParams(
            dimension_semantics=("parallel", "arbitrary"), vmem_limit_bytes=limit
        ),
        name="swiglu_ffn",
    )(x, g.reshape(1, d), wg, wu, wd, gf.reshape(1, d))


def kernel(x, mem, norm_mix_g, w_in, lambda_q1, lambda_k1, lambda_q2, lambda_k2, da_subln_g, w_o, norm_x_g, norm_mem_g, w_xq, w_xk, w_xv, w_xo, norm_ffn_g, w_gate, w_up, w_down, norm_f_g):
    batch, seq, d_model = x.shape
    mem_len = mem.shape[1]
    depth = w_in.shape[0]
    slopes = jnp.asarray(2.0 ** (-8.0 * np.arange(1, DA_HEADS + 1) / DA_HEADS), dtype=F32)
    log_gammas = jnp.asarray(np.log(1.0 - 2.0 ** (-5.0 - np.arange(RET_HEADS))), dtype=F32)

    xf = x.reshape(batch * seq, d_model)
    memf = mem.reshape(batch * mem_len, d_model)
    for l in range(depth):
        lam_init = 0.8 - 0.6 * math.exp(-0.3 * l)
        bf = lambda w: w[l].astype(BF16)

        proj = _norm_matmul(xf, norm_mix_g[l], bf(w_in), BF16, tm=1024, tn=1024, name="in_proj")
        da = _diff_attention(proj, slopes, lambda_q1[l], lambda_k1[l], lambda_q2[l], lambda_k2[l],
                             da_subln_g[l], batch=batch, seq=seq, lam_init=lam_init, tq=512)
        ret = _retention(proj, log_gammas, batch=batch, seq=seq)
        xf = _matmul2_res(da, ret, bf(w_o), xf, tm=1024, tn=1024, name="out_proj")

        xq = _norm_matmul(xf, norm_x_g[l], bf(w_xq), BF16, tm=1024, tn=1024, name="xattn_q")
        xk = _norm_matmul(memf, norm_mem_g[l], bf(w_xk), BF16, tm=1024, tn=1024, name="xattn_k")
        xv = _norm_matmul(memf, norm_mem_g[l], bf(w_xv), BF16, tm=1024, tn=1024, name="xattn_v")
        xf = _cross_attention(xq, xk, xv, xf, bf(w_xo), batch=batch, seq=seq, mem_len=mem_len, tq=512)

        xf = _ffn(xf, norm_ffn_g[l], bf(w_gate), bf(w_up), bf(w_down), norm_f_g,
                  final_norm=(l == depth - 1), tm=512, tf=512)
    return xf.reshape(batch, seq, d_model)
```

```python
import functools
import math

import jax
import jax.numpy as jnp
import numpy as np
from jax import lax
from jax.experimental import pallas as pl
from jax.experimental.pallas import tpu as pltpu

F32 = jnp.float32
BF16 = jnp.bfloat16

DA_HEADS = 4
DA_HEAD_DIM = 128
DA_V_DIM = 2 * DA_HEAD_DIM
RET_HEADS = 4
RET_QK_DIM = 128
RET_V_DIM = 256
XATTN_HEADS = 4
RET_CHUNK = 128
NORM_EPS = 1e-6
NEG_INF = -1e30
LOG2_E = math.log2(math.e)
NORM_BLOCK_ROWS = 256

V7X_LANES = 128
MXU_COLS = 256
BF16_TILE_ROWS = 16
V7X_VMEM_BYTES = 64 * 1024 * 1024
V7X_VMEM_USABLE_BYTES = V7X_VMEM_BYTES - 8 * 1024 * 1024
COMPILER_SCRATCH_BYTES = 4 * 1024 * 1024


def _nbytes(shape, dtype):
    return int(np.prod(shape)) * jnp.dtype(dtype).itemsize


def _vmem_limit(pipelined, resident):
    need = 2 * sum(pipelined) + sum(resident) + COMPILER_SCRATCH_BYTES
    return int(min(V7X_VMEM_USABLE_BYTES, need))


def _rms(x):
    return x * lax.rsqrt(jnp.mean(x * x, axis=-1, keepdims=True) + NORM_EPS)


def _fold_lanes(x, op):
    tiles = [x[:, i:i + V7X_LANES] for i in range(0, x.shape[1], V7X_LANES)]
    return functools.reduce(op, tiles)


def _dot(a, b):
    return jnp.dot(a, b, preferred_element_type=F32)


def _dot_nt(a, b):
    return lax.dot_general(a, b, (((1,), (1,)), ((), ())), preferred_element_type=F32)


def _dot_tn(a, b):
    return lax.dot_general(a, b, (((0,), (0,)), ((), ())), preferred_element_type=F32)


class _SideCasts:
    def __init__(self, weights, grid):
        self.weights = list(weights)
        self.grid = tuple(grid)
        n_steps = int(np.prod(self.grid))
        self.plans = []
        for w in self.weights:
            rows, n_blocks = w.shape[0], n_steps
            while rows % n_blocks or (rows // n_blocks) % BF16_TILE_ROWS:
                n_blocks -= 1
            self.plans.append((n_blocks, rows // n_blocks))

    def __len__(self):
        return len(self.weights)

    def _specs(self):
        specs = []
        for w, (n_blocks, block_rows) in zip(self.weights, self.plans):
            def index(*ids, n_blocks=n_blocks):
                step = ids[0]
                for extent, idx in zip(self.grid[1:], ids[1:]):
                    step = step * extent + idx
                return (jnp.minimum(step, n_blocks - 1), 0)
            specs.append(pl.BlockSpec((block_rows, w.shape[1]), index))
        return specs

    in_specs = property(_specs)
    out_specs = property(_specs)

    @property
    def out_shapes(self):
        return [jax.ShapeDtypeStruct(w.shape, BF16) for w in self.weights]

    @property
    def window_bytes(self):
        return [_nbytes((rows, w.shape[1]), dt)
                for w, (_, rows) in zip(self.weights, self.plans) for dt in (w.dtype, BF16)]


def _split_refs(refs, n_in, n_side):
    ins, rest = refs[:n_in], refs[n_in:]
    side_in, rest = rest[:n_side], rest[n_side:]
    out, side_out, scratch = rest[0], rest[1:1 + n_side], rest[1 + n_side:]
    return ins, out, scratch, list(zip(side_in, side_out))


def _cast_blocks(pairs):
    for src, dst in pairs:
        dst[...] = src[...].astype(dst.dtype)


def _row_blocks(n_rows):
    step = min(NORM_BLOCK_ROWS, n_rows)
    return [slice(r, r + step) for r in range(0, n_rows, step)]


def _col_chunks(n_cols):
    step = min(MXU_COLS, n_cols)
    return [slice(c, c + step) for c in range(0, n_cols, step)]


def _norm_matmul_kernel(*refs, n_side, cast_w):
    (x_ref, g_ref, w_ref), o_ref, scratch, side = _split_refs(refs, 3, n_side)
    h_ref = scratch[0]
    j = pl.program_id(1)

    @pl.when(j == 0)
    def _():
        _cast_blocks(side)
        if cast_w:
            wb_ref = scratch[1]
            wb_ref[...] = w_ref[...].astype(wb_ref.dtype)
        else:
            wb_ref = w_ref
        for rows in _row_blocks(x_ref.shape[0]):
            h = (_rms(x_ref[rows, :]) * g_ref[...]).astype(h_ref.dtype)
            h_ref[rows, :] = h
            o_ref[rows, :] = _dot(h, wb_ref[...]).astype(o_ref.dtype)

    @pl.when(j > 0)
    def _():
        _cast_blocks(side)
        if cast_w:
            for cols in _col_chunks(w_ref.shape[1]):
                o_ref[:, cols] = _dot(h_ref[...], w_ref[:, cols].astype(BF16)).astype(o_ref.dtype)
        else:
            o_ref[...] = _dot(h_ref[...], w_ref[...]).astype(o_ref.dtype)


def _norm_matmul(x, g, w, out_dtype, *, tm, tn, name, side_weights=()):
    m, d = x.shape
    n = w.shape[1]
    tm, tn = min(tm, m), min(tn, n)
    grid = (m // tm, n // tn)
    side = _SideCasts(side_weights, grid)
    cast_w = w.dtype != BF16
    scratch = [pltpu.VMEM((tm, d), BF16)] + ([pltpu.VMEM((d, tn), BF16)] if cast_w else [])
    limit = _vmem_limit(
        [_nbytes((tm, d), x.dtype), _nbytes((d, tn), w.dtype), _nbytes((tm, tn), out_dtype)]
        + side.window_bytes,
        [_nbytes((tm, d), BF16), _nbytes((d, tn), BF16) * cast_w, _nbytes((tm, tn), F32)],
    )
    outs = pl.pallas_call(
        functools.partial(_norm_matmul_kernel, n_side=len(side), cast_w=cast_w),
        grid=grid,
        in_specs=[
            pl.BlockSpec((tm, d), lambda i, j: (i, 0)),
            pl.BlockSpec((1, d), lambda i, j: (0, 0)),
            pl.BlockSpec((d, tn), lambda i, j: (0, j)),
        ] + side.in_specs,
        out_specs=[pl.BlockSpec((tm, tn), lambda i, j: (i, j))] + side.out_specs,
        out_shape=[jax.ShapeDtypeStruct((m, n), out_dtype)] + side.out_shapes,
        scratch_shapes=scratch,
        compiler_params=pltpu.CompilerParams(
            dimension_semantics=("arbitrary", "arbitrary"), vmem_limit_bytes=limit
        ),
        name=name,
    )(x, g.reshape(1, d), w, *side_weights)
    return tuple(outs)


def _matmul2_res_kernel(a1_ref, a2_ref, w1_ref, w2_ref, res_ref, o_ref):
    acc = _dot(a1_ref[...], w1_ref[...]) + _dot(a2_ref[...], w2_ref[...])
    o_ref[...] = res_ref[...] + acc


def _matmul2_res(a1, a2, w, res, *, tm, tn, name):
    m, k1 = a1.shape
    k2 = a2.shape[1]
    assert k1 == k2 and w.shape[0] == k1 + k2
    n = w.shape[1]
    limit = _vmem_limit(
        [
            _nbytes((tm, k1), a1.dtype),
            _nbytes((tm, k2), a2.dtype),
            _nbytes((k1, tn), w.dtype),
            _nbytes((k2, tn), w.dtype),
            _nbytes((tm, tn), F32),
            _nbytes((tm, tn), F32),
        ],
        [_nbytes((tm, tn), F32)],
    )
    return pl.pallas_call(
        _matmul2_res_kernel,
        grid=(m // tm, n // tn),
        in_specs=[
            pl.BlockSpec((tm, k1), lambda i, j: (i, 0)),
            pl.BlockSpec((tm, k2), lambda i, j: (i, 0)),
            pl.BlockSpec((k1, tn), lambda i, j: (0, j)),
            pl.BlockSpec((k2, tn), lambda i, j: (1, j)),
            pl.BlockSpec((tm, tn), lambda i, j: (i, j)),
        ],
        out_specs=pl.BlockSpec((tm, tn), lambda i, j: (i, j)),
        out_shape=jax.ShapeDtypeStruct((m, n), F32),
        compiler_params=pltpu.CompilerParams(
            dimension_semantics=("parallel", "arbitrary"), vmem_limit_bytes=limit
        ),
        name=name,
    )(a1, a2, w, w, res)


def _diff_attn_kernel(*refs, n_side, tq, lam_init):
    ins, o_ref, (s_ref, p_ref), side = _split_refs(refs, 9, n_side)
    slope_ref, lq1_ref, lk1_ref, lq2_ref, lk2_ref, g_ref, q_ref, k_ref, v_ref = ins
    _cast_blocks(side)
    s_len = q_ref.shape[0]
    d = DA_HEAD_DIM
    scale2 = d ** -0.5 * LOG2_E
    slope2 = slope_ref[pl.program_id(1)] * LOG2_E
    lam = (jnp.exp(jnp.sum(lq1_ref[...] * lk1_ref[...], axis=-1, keepdims=True))
           - jnp.exp(jnp.sum(lq2_ref[...] * lk2_ref[...], axis=-1, keepdims=True))
           + lam_init)

    row = lax.broadcasted_iota(jnp.int32, (tq, tq), 0)
    col = lax.broadcasted_iota(jnp.int32, (tq, tq), 1)
    causal = col <= row
    diag_bias = slope2 * (col - (tq - 1)).astype(F32)
    kpos = lax.broadcasted_iota(jnp.int32, (1, s_len), 1).astype(F32)

    for qi in range(s_len // tq):
        lo, hi = qi * tq, (qi + 1) * tq
        key_blocks = [slice(j * tq, (j + 1) * tq) for j in range(qi + 1)]
        heads = []
        for c in range(2):
            dcols = slice(c * d, (c + 1) * d)
            qc = q_ref[lo:hi, dcols]
            m = None
            for j, cols in enumerate(key_blocks):
                s = _dot_nt(qc, k_ref[cols, dcols]) * scale2
                if j == qi:
                    s = jnp.where(causal, s + diag_bias, NEG_INF)
                else:
                    s = s + slope2 * (kpos[:, cols] - float(hi - 1))
                s_ref[c, :, cols] = s
                bm = _fold_lanes(s, jnp.maximum)
                m = bm if m is None else jnp.maximum(m, bm)
            m = jnp.max(m, axis=-1, keepdims=True)
            l = None
            for cols in key_blocks:
                p = jnp.exp2(s_ref[c, :, cols] - m)
                bl = _fold_lanes(p, jnp.add)
                l = bl if l is None else l + bl
                p_ref[c, :, cols] = p.astype(p_ref.dtype)
            l = jnp.sum(l, axis=-1, keepdims=True)
            acc = _dot(p_ref[c, :, 0:hi], v_ref[0:hi, :])
            heads.append((acc, l))
        (acc1, l1), (acc2, l2) = heads
        out = acc1 * (1.0 / l1) - acc2 * (lam / l2)
        y = _rms(out) * g_ref[...] * (1.0 - lam_init)
        o_ref[lo:hi, :] = y.astype(o_ref.dtype)


def _diff_attention(proj, slopes, lq1, lk1, lq2, lk2, g, *, batch, seq, lam_init, tq,
                    side_weights=()):
    hb = DA_V_DIM
    q_blk0, k_blk0, v_blk0 = 0, DA_HEADS, 2 * DA_HEADS
    grid = (batch, DA_HEADS)
    side = _SideCasts(side_weights, grid)
    seq_bytes = _nbytes((seq, hb), proj.dtype)
    scratch = [pltpu.VMEM((2, tq, seq), F32), pltpu.VMEM((2, tq, seq), BF16)]
    limit = _vmem_limit(
        [seq_bytes] * 4 + side.window_bytes,
        [3 * _nbytes((2, tq, seq), F32), 3 * _nbytes((2, tq, seq), BF16)],
    )
    vec = pl.BlockSpec((1, DA_HEAD_DIM), lambda b, h: (0, 0))
    kernel = functools.partial(_diff_attn_kernel, n_side=len(side), tq=tq, lam_init=lam_init)
    outs = pl.pallas_call(
        kernel,
        grid=grid,
        in_specs=[
            pl.BlockSpec(memory_space=pltpu.SMEM),
            vec, vec, vec, vec,
            pl.BlockSpec((1, hb), lambda b, h: (0, 0)),
            pl.BlockSpec((seq, hb), lambda b, h: (b, q_blk0 + h)),
            pl.BlockSpec((seq, hb), lambda b, h: (b, k_blk0 + h)),
            pl.BlockSpec((seq, hb), lambda b, h: (b, v_blk0 + h)),
        ] + side.in_specs,
        out_specs=[pl.BlockSpec((seq, hb), lambda b, h: (b, h))] + side.out_specs,
        out_shape=[jax.ShapeDtypeStruct((batch * seq, DA_HEADS * hb), BF16)] + side.out_shapes,
        scratch_shapes=scratch,
        compiler_params=pltpu.CompilerParams(
            dimension_semantics=("arbitrary", "arbitrary"), vmem_limit_bytes=limit
        ),
        name="diff_attention",
    )(slopes, lq1.reshape(1, -1), lk1.reshape(1, -1), lq2.reshape(1, -1), lk2.reshape(1, -1),
      g.reshape(1, hb), proj, proj, proj, *side_weights)
    return tuple(outs)


def _retention_kernel(*refs, n_side):
    (lg_ref, q_ref, k_ref, v_ref, gate_ref), o_ref, _, side = _split_refs(refs, 5, n_side)
    _cast_blocks(side)
    s_len = q_ref.shape[0]
    c = RET_CHUNK
    dk, dv = RET_QK_DIM, RET_V_DIM
    scale = dk ** -0.5
    lg = lg_ref[pl.program_id(1)]

    row = lax.broadcasted_iota(jnp.int32, (c, c), 0)
    col = lax.broadcasted_iota(jnp.int32, (c, c), 1)
    diff = (row - col).astype(F32)
    intra = jnp.where(diff >= 0, jnp.exp(lg * jnp.maximum(diff, 0.0)), 0.0) * scale
    row_k = lax.broadcasted_iota(jnp.int32, (c, dk), 0).astype(F32)
    k_decay = jnp.exp(lg * (float(c - 1) - row_k)) * scale
    row_v = lax.broadcasted_iota(jnp.int32, (c, dv), 0).astype(F32)
    q_decay = jnp.exp(lg * (row_v + 1.0))
    chunk_decay = jnp.exp(jnp.full((1, dv), lg * float(c), F32))

    state = jnp.zeros((dk, dv), F32)
    for i in range(s_len // c):
        lo, hi = i * c, (i + 1) * c
        q = q_ref[lo:hi, :]
        k = k_ref[lo:hi, :]
        v = v_ref[lo:hi, :]
        scores = _dot_nt(q, k) * intra
        y = _dot(scores.astype(BF16), v)
        if i > 0:
            y = y + _dot(q, state.astype(BF16)) * q_decay
        if i + 1 < s_len // c:
            kd = (k.astype(F32) * k_decay).astype(BF16)
            state = state * chunk_decay + _dot_tn(kd, v)
        gate = gate_ref[lo:hi, :].astype(F32)
        o_ref[lo:hi, :] = (_rms(y) * (gate * jax.nn.sigmoid(gate))).astype(o_ref.dtype)


def _retention(proj, log_gammas, *, batch, seq, side_weights=()):
    da_cols = 3 * DA_HEADS * DA_V_DIM
    q_blk0 = da_cols // RET_QK_DIM
    k_blk0 = q_blk0 + RET_HEADS
    v_blk0 = (da_cols + 2 * RET_HEADS * RET_QK_DIM) // RET_V_DIM
    g_blk0 = v_blk0 + RET_HEADS
    grid = (batch, RET_HEADS)
    side = _SideCasts(side_weights, grid)
    limit = _vmem_limit(
        [_nbytes((seq, RET_QK_DIM), proj.dtype)] * 2 + [_nbytes((seq, RET_V_DIM), proj.dtype)] * 3
        + side.window_bytes,
        [16 * _nbytes((RET_CHUNK, RET_V_DIM), F32) * (seq // RET_CHUNK)],
    )
    outs = pl.pallas_call(
        functools.partial(_retention_kernel, n_side=len(side)),
        grid=grid,
        in_specs=[
            pl.BlockSpec(memory_space=pltpu.SMEM),
            pl.BlockSpec((seq, RET_QK_DIM), lambda b, h: (b, q_blk0 + h)),
            pl.BlockSpec((seq, RET_QK_DIM), lambda b, h: (b, k_blk0 + h)),
            pl.BlockSpec((seq, RET_V_DIM), lambda b, h: (b, v_blk0 + h)),
            pl.BlockSpec((seq, RET_V_DIM), lambda b, h: (b, g_blk0 + h)),
        ] + side.in_specs,
        out_specs=[pl.BlockSpec((seq, RET_V_DIM), lambda b, h: (b, h))] + side.out_specs,
        out_shape=[jax.ShapeDtypeStruct((batch * seq, RET_HEADS * RET_V_DIM), BF16)] + side.out_shapes,
        compiler_params=pltpu.CompilerParams(
            dimension_semantics=("arbitrary", "arbitrary"), vmem_limit_bytes=limit
        ),
        name="retention",
    )(log_gammas, proj, proj, proj, proj, *side_weights)
    return tuple(outs)


def _xattn_kernel(xq_ref, xk_ref, xv_ref, res_ref, wo_ref, o_ref, xo_ref):
    d_model = xq_ref.shape[1]
    hd = d_model // XATTN_HEADS
    scale2 = hd ** -0.5 * LOG2_E
    for h in range(XATTN_HEADS):
        cols = slice(h * hd, (h + 1) * hd)
        s = _dot_nt(xq_ref[:, cols], xk_ref[:, cols]) * scale2
        p = jnp.exp2(s - jnp.max(s, axis=-1, keepdims=True))
        p = p * (1.0 / jnp.sum(p, axis=-1, keepdims=True))
        xo_ref[:, cols] = _dot(p.astype(BF16), xv_ref[:, cols]).astype(xo_ref.dtype)
    o_ref[...] = res_ref[...] + _dot(xo_ref[...], wo_ref[...])


def _cross_attention(xq, xk, xv, res, wo, *, batch, seq, mem_len, tq):
    d = xq.shape[1]
    nq = seq // tq
    limit = _vmem_limit(
        [
            _nbytes((tq, d), xq.dtype),
            _nbytes((mem_len, d), xk.dtype),
            _nbytes((mem_len, d), xv.dtype),
            _nbytes((tq, d), F32),
            _nbytes((d, d), wo.dtype),
            _nbytes((tq, d), F32),
        ],
        [_nbytes((tq, d), BF16), _nbytes((tq, d), F32)],
    )
    return pl.pallas_call(
        _xattn_kernel,
        grid=(batch, nq),
        in_specs=[
            pl.BlockSpec((tq, d), lambda b, i: (b * nq + i, 0)),
            pl.BlockSpec((mem_len, d), lambda b, i: (b, 0)),
            pl.BlockSpec((mem_len, d), lambda b, i: (b, 0)),
            pl.BlockSpec((tq, d), lambda b, i: (b * nq + i, 0)),
            pl.BlockSpec((d, d), lambda b, i: (0, 0)),
        ],
        out_specs=pl.BlockSpec((tq, d), lambda b, i: (b * nq + i, 0)),
        out_shape=jax.ShapeDtypeStruct((batch * seq, d), F32),
        scratch_shapes=[pltpu.VMEM((tq, d), BF16)],
        compiler_params=pltpu.CompilerParams(
            dimension_semantics=("parallel", "arbitrary"), vmem_limit_bytes=limit
        ),
        name="cross_attention",
    )(xq, xk, xv, res, wo)


def _ffn_kernel(x_ref, g_ref, wg_ref, wu_ref, wd_ref, gf_ref, o_ref, h_ref, *, final_norm):
    f = pl.program_id(1)
    last = pl.num_programs(1) - 1
    blocks = _row_blocks(x_ref.shape[0])

    def partial_ffn(h):
        gate = _dot(h, wg_ref[...])
        up = _dot(h, wu_ref[...])
        act = (gate * jax.nn.sigmoid(gate)) * up
        return _dot(act.astype(BF16), wd_ref[...])

    @pl.when(f == 0)
    def _():
        for rows in blocks:
            x = x_ref[rows, :]
            h = (_rms(x) * g_ref[...]).astype(h_ref.dtype)
            h_ref[rows, :] = h
            o_ref[rows, :] = x + partial_ffn(h)

    if final_norm:
        @pl.when(jnp.logical_and(f > 0, f < last))
        def _():
            o_ref[...] += partial_ffn(h_ref[...])

        @pl.when(f == last)
        def _():
            for rows in blocks:
                y = o_ref[rows, :] + partial_ffn(h_ref[rows, :])
                o_ref[rows, :] = _rms(y) * gf_ref[...]
    else:
        @pl.when(f > 0)
        def _():
            o_ref[...] += partial_ffn(h_ref[...])


def _ffn(x, g, wg, wu, wd, gf, *, final_norm, tm, tf):
    m, d = x.shape
    d_ff = wg.shape[1]
    limit = _vmem_limit(
        [
            _nbytes((tm, d), F32),
            _nbytes((d, tf), wg.dtype),
            _nbytes((d, tf), wu.dtype),
            _nbytes((tf, d), wd.dtype),
            _nbytes((tm, d), F32),
        ],
        [_nbytes((tm, d), BF16), _nbytes((tm, d), F32), 4 * _nbytes((tm, tf), F32)],
    )
    return pl.pallas_call(
        functools.partial(_ffn_kernel, final_norm=final_norm),
        grid=(m // tm, d_ff // tf),
        in_specs=[
            pl.BlockSpec((tm, d), lambda i, f: (i, 0)),
            pl.BlockSpec((1, d), lambda i, f: (0, 0)),
            pl.BlockSpec((d, tf), lambda i, f: (0, f)),
            pl.BlockSpec((d, tf), lambda i, f: (0, f)),
            pl.BlockSpec((tf, d), lambda i, f: (f, 0)),
            pl.BlockSpec((1, d), lambda i, f: (0, 0)),
        ],
        out_specs=pl.BlockSpec((tm, d), lambda i, f: (i, 0)),
        out_shape=jax.ShapeDtypeStruct((m, d), F32),
        scratch_shapes=[pltpu.VMEM((tm, d), BF16)],
        compiler_params=pltpu.CompilerParams(
            dimension_semantics=("parallel", "arbitrary"), vmem_limit_bytes=limit
        ),
        name="swiglu_ffn",
    )(x, g.reshape(1, d), wg, wu, wd, gf.reshape(1, d))


def kernel(x, mem, norm_mix_g, w_in, lambda_q1, lambda_k1, lambda_q2, lambda_k2, da_subln_g, w_o, norm_x_g, norm_mem_g, w_xq, w_xk, w_xv, w_xo, norm_ffn_g, w_gate, w_up, w_down, norm_f_g):
    batch, seq, d_model = x.shape
    mem_len = mem.shape[1]
    depth = w_in.shape[0]
    slopes = jnp.asarray(2.0 ** (-8.0 * np.arange(1, DA_HEADS + 1) / DA_HEADS), dtype=F32)
    log_gammas = jnp.asarray(np.log(1.0 - 2.0 ** (-5.0 - np.arange(RET_HEADS))), dtype=F32)

    xf = x.reshape(batch * seq, d_model)
    memf = mem.reshape(batch * mem_len, d_model)
    for l in range(depth):
        lam_init = 0.8 - 0.6 * math.exp(-0.3 * l)
        proj, wb_o, wb_xq = _norm_matmul(
            xf, norm_mix_g[l], w_in[l], BF16, tm=1024, tn=1024, name="in_proj",
            side_weights=(w_o[l], w_xq[l]))
        da, wb_gate, wb_up = _diff_attention(
            proj, slopes, lambda_q1[l], lambda_k1[l], lambda_q2[l], lambda_k2[l], da_subln_g[l],
            batch=batch, seq=seq, lam_init=lam_init, tq=512, side_weights=(w_gate[l], w_up[l]))
        ret, wb_down = _retention(proj, log_gammas, batch=batch, seq=seq, side_weights=(w_down[l],))
        xf = _matmul2_res(da, ret, wb_o, xf, tm=1024, tn=1024, name="out_proj")

        xq, wb_xk, wb_xv, wb_xo = _norm_matmul(
            xf, norm_x_g[l], wb_xq, BF16, tm=1024, tn=1024, name="xattn_q",
            side_weights=(w_xk[l], w_xv[l], w_xo[l]))
        xk, = _norm_matmul(memf, norm_mem_g[l], wb_xk, BF16, tm=1024, tn=1024, name="xattn_k")
        xv, = _norm_matmul(memf, norm_mem_g[l], wb_xv, BF16, tm=1024, tn=1024, name="xattn_v")
        xf = _cross_attention(xq, xk, xv, xf, wb_xo, batch=batch, seq=seq, mem_len=mem_len, tq=512)

        xf = _ffn(xf, norm_ffn_g[l], wb_gate, wb_up, wb_down, norm_f_g,
                  final_norm=(l == depth - 1), tm=512, tf=512)
    return xf.reshape(batch, seq, d_model)
```

```python
import functools
import math

import jax
import jax.numpy as jnp
import numpy as np
from jax import lax
from jax.experimental import pallas as pl
from jax.experimental.pallas import tpu as pltpu

F32 = jnp.float32
BF16 = jnp.bfloat16

DA_HEADS = 4
DA_HEAD_DIM = 128
DA_V_DIM = 2 * DA_HEAD_DIM
RET_HEADS = 4
RET_QK_DIM = 128
RET_V_DIM = 256
XATTN_HEADS = 4
RET_CHUNK = 128
NORM_EPS = 1e-6
NEG_INF = -1e30
LOG2_E = math.log2(math.e)
NORM_BLOCK_ROWS = 256

V7X_LANES = 128
MXU_COLS = 256
BF16_TILE_ROWS = 16
V7X_VMEM_BYTES = 64 * 1024 * 1024
V7X_VMEM_USABLE_BYTES = V7X_VMEM_BYTES - 8 * 1024 * 1024
COMPILER_SCRATCH_BYTES = 4 * 1024 * 1024


def _nbytes(shape, dtype):
    return int(np.prod(shape)) * jnp.dtype(dtype).itemsize


def _vmem_limit(pipelined, resident):
    need = 2 * sum(pipelined) + sum(resident) + COMPILER_SCRATCH_BYTES
    return int(min(V7X_VMEM_USABLE_BYTES, need))


def _rms(x):
    return x * lax.rsqrt(jnp.mean(x * x, axis=-1, keepdims=True) + NORM_EPS)


def _fold_lanes(x, op):
    tiles = [x[:, i:i + V7X_LANES] for i in range(0, x.shape[1], V7X_LANES)]
    return functools.reduce(op, tiles)


def _dot(a, b):
    return jnp.dot(a, b, preferred_element_type=F32)


def _dot_nt(a, b):
    return lax.dot_general(a, b, (((1,), (1,)), ((), ())), preferred_element_type=F32)


def _dot_tn(a, b):
    return lax.dot_general(a, b, (((0,), (0,)), ((), ())), preferred_element_type=F32)


class _SideCasts:
    def __init__(self, weights, grid):
        self.weights = list(weights)
        self.grid = tuple(grid)
        n_steps = int(np.prod(self.grid))
        self.plans = []
        for w in self.weights:
            rows, n_blocks = w.shape[0], n_steps
            while rows % n_blocks or (rows // n_blocks) % BF16_TILE_ROWS:
                n_blocks -= 1
            self.plans.append((n_blocks, rows // n_blocks))

    def __len__(self):
        return len(self.weights)

    def _specs(self):
        specs = []
        for w, (n_blocks, block_rows) in zip(self.weights, self.plans):
            def index(*ids, n_blocks=n_blocks):
                step = ids[0]
                for extent, idx in zip(self.grid[1:], ids[1:]):
                    step = step * extent + idx
                return (jnp.minimum(step, n_blocks - 1), 0)
            specs.append(pl.BlockSpec((block_rows, w.shape[1]), index))
        return specs

    in_specs = property(_specs)
    out_specs = property(_specs)

    @property
    def out_shapes(self):
        return [jax.ShapeDtypeStruct(w.shape, BF16) for w in self.weights]

    @property
    def window_bytes(self):
        return [_nbytes((rows, w.shape[1]), dt)
                for w, (_, rows) in zip(self.weights, self.plans) for dt in (w.dtype, BF16)]


def _split_refs(refs, n_in, n_side):
    ins, rest = refs[:n_in], refs[n_in:]
    side_in, rest = rest[:n_side], rest[n_side:]
    out, side_out, scratch = rest[0], rest[1:1 + n_side], rest[1 + n_side:]
    return ins, out, scratch, list(zip(side_in, side_out))


def _cast_blocks(pairs):
    for src, dst in pairs:
        dst[...] = src[...].astype(dst.dtype)


def _row_blocks(n_rows):
    step = min(NORM_BLOCK_ROWS, n_rows)
    return [slice(r, r + step) for r in range(0, n_rows, step)]


def _col_chunks(n_cols):
    step = min(MXU_COLS, n_cols)
    return [slice(c, c + step) for c in range(0, n_cols, step)]


def _norm_matmul_kernel(*refs, n_side, cast_w):
    (x_ref, g_ref, w_ref), o_ref, scratch, side = _split_refs(refs, 3, n_side)
    h_ref = scratch[0]
    j = pl.program_id(1)

    @pl.when(j == 0)
    def _():
        _cast_blocks(side)
        if cast_w:
            wb_ref = scratch[1]
            wb_ref[...] = w_ref[...].astype(wb_ref.dtype)
        else:
            wb_ref = w_ref
        for rows in _row_blocks(x_ref.shape[0]):
            h = (_rms(x_ref[rows, :]) * g_ref[...]).astype(h_ref.dtype)
            h_ref[rows, :] = h
            o_ref[rows, :] = _dot(h, wb_ref[...]).astype(o_ref.dtype)

    @pl.when(j > 0)
    def _():
        _cast_blocks(side)
        if cast_w:
            for cols in _col_chunks(w_ref.shape[1]):
                o_ref[:, cols] = _dot(h_ref[...], w_ref[:, cols].astype(BF16)).astype(o_ref.dtype)
        else:
            o_ref[...] = _dot(h_ref[...], w_ref[...]).astype(o_ref.dtype)


def _norm_matmul(x, g, w, out_dtype, *, tm, tn, name, side_weights=()):
    m, d = x.shape
    n = w.shape[1]
    tm, tn = min(tm, m), min(tn, n)
    grid = (m // tm, n // tn)
    side = _SideCasts(side_weights, grid)
    cast_w = w.dtype != BF16
    scratch = [pltpu.VMEM((tm, d), BF16)] + ([pltpu.VMEM((d, tn), BF16)] if cast_w else [])
    limit = _vmem_limit(
        [_nbytes((tm, d), x.dtype), _nbytes((d, tn), w.dtype), _nbytes((tm, tn), out_dtype)]
        + side.window_bytes,
        [_nbytes((tm, d), BF16), _nbytes((d, tn), BF16) * cast_w, _nbytes((tm, tn), F32)],
    )
    outs = pl.pallas_call(
        functools.partial(_norm_matmul_kernel, n_side=len(side), cast_w=cast_w),
        grid=grid,
        in_specs=[
            pl.BlockSpec((tm, d), lambda i, j: (i, 0)),
            pl.BlockSpec((1, d), lambda i, j: (0, 0)),
            pl.BlockSpec((d, tn), lambda i, j: (0, j)),
        ] + side.in_specs,
        out_specs=[pl.BlockSpec((tm, tn), lambda i, j: (i, j))] + side.out_specs,
        out_shape=[jax.ShapeDtypeStruct((m, n), out_dtype)] + side.out_shapes,
        scratch_shapes=scratch,
        compiler_params=pltpu.CompilerParams(
            dimension_semantics=("arbitrary", "arbitrary"), vmem_limit_bytes=limit
        ),
        name=name,
    )(x, g.reshape(1, d), w, *side_weights)
    return tuple(outs)


def _matmul2_res_kernel(a1_ref, a2_ref, w1_ref, w2_ref, res_ref, o_ref):
    acc = _dot(a1_ref[...], w1_ref[...]) + _dot(a2_ref[...], w2_ref[...])
    o_ref[...] = res_ref[...] + acc


def _matmul2_res(a1, a2, w, res, *, tm, tn, name):
    m, k1 = a1.shape
    k2 = a2.shape[1]
    assert k1 == k2 and w.shape[0] == k1 + k2
    n = w.shape[1]
    limit = _vmem_limit(
        [
            _nbytes((tm, k1), a1.dtype),
            _nbytes((tm, k2), a2.dtype),
            _nbytes((k1, tn), w.dtype),
            _nbytes((k2, tn), w.dtype),
            _nbytes((tm, tn), F32),
            _nbytes((tm, tn), F32),
        ],
        [_nbytes((tm, tn), F32)],
    )
    return pl.pallas_call(
        _matmul2_res_kernel,
        grid=(m // tm, n // tn),
        in_specs=[
            pl.BlockSpec((tm, k1), lambda i, j: (i, 0)),
            pl.BlockSpec((tm, k2), lambda i, j: (i, 0)),
            pl.BlockSpec((k1, tn), lambda i, j: (0, j)),
            pl.BlockSpec((k2, tn), lambda i, j: (1, j)),
            pl.BlockSpec((tm, tn), lambda i, j: (i, j)),
        ],
        out_specs=pl.BlockSpec((tm, tn), lambda i, j: (i, j)),
        out_shape=jax.ShapeDtypeStruct((m, n), F32),
        compiler_params=pltpu.CompilerParams(
            dimension_semantics=("parallel", "arbitrary"), vmem_limit_bytes=limit
        ),
        name=name,
    )(a1, a2, w, w, res)


def _diff_attn_kernel(*refs, n_side, tq, lam_init):
    ins, o_ref, (s_ref, p_ref), side = _split_refs(refs, 9, n_side)
    slope_ref, lq1_ref, lk1_ref, lq2_ref, lk2_ref, g_ref, q_ref, k_ref, v_ref = ins
    _cast_blocks(side)
    s_len = q_ref.shape[0]
    d = DA_HEAD_DIM
    scale2 = d ** -0.5 * LOG2_E
    slope2 = slope_ref[pl.program_id(1)] * LOG2_E
    lam = (jnp.exp(jnp.sum(lq1_ref[...] * lk1_ref[...], axis=-1, keepdims=True))
           - jnp.exp(jnp.sum(lq2_ref[...] * lk2_ref[...], axis=-1, keepdims=True))
           + lam_init)

    row = lax.broadcasted_iota(jnp.int32, (tq, tq), 0)
    col = lax.broadcasted_iota(jnp.int32, (tq, tq), 1)
    causal = col <= row
    diag_bias = slope2 * (col - (tq - 1)).astype(F32)
    kpos = lax.broadcasted_iota(jnp.int32, (1, s_len), 1).astype(F32)

    for qi in range(s_len // tq):
        lo, hi = qi * tq, (qi + 1) * tq
        key_blocks = [slice(j * tq, (j + 1) * tq) for j in range(qi + 1)]
        heads = []
        for c in range(2):
            dcols = slice(c * d, (c + 1) * d)
            qc = q_ref[lo:hi, dcols]
            m = None
            for j, cols in enumerate(key_blocks):
                s = _dot_nt(qc, k_ref[cols, dcols]) * scale2
                if j == qi:
                    s = jnp.where(causal, s + diag_bias, NEG_INF)
                else:
                    s = s + slope2 * (kpos[:, cols] - float(hi - 1))
                s_ref[c, :, cols] = s
                bm = _fold_lanes(s, jnp.maximum)
                m = bm if m is None else jnp.maximum(m, bm)
            m = jnp.max(m, axis=-1, keepdims=True)
            l = None
            for cols in key_blocks:
                p = jnp.exp2(s_ref[c, :, cols] - m)
                bl = _fold_lanes(p, jnp.add)
                l = bl if l is None else l + bl
                p_ref[c, :, cols] = p.astype(p_ref.dtype)
            l = jnp.sum(l, axis=-1, keepdims=True)
            acc = _dot(p_ref[c, :, 0:hi], v_ref[0:hi, :])
            heads.append((acc, l))
        (acc1, l1), (acc2, l2) = heads
        out = acc1 * (1.0 / l1) - acc2 * (lam / l2)
        y = _rms(out) * g_ref[...] * (1.0 - lam_init)
        o_ref[lo:hi, :] = y.astype(o_ref.dtype)


def _diff_attention(proj, slopes, lq1, lk1, lq2, lk2, g, *, batch, seq, lam_init, tq,
                    side_weights=()):
    hb = DA_V_DIM
    q_blk0, k_blk0, v_blk0 = 0, DA_HEADS, 2 * DA_HEADS
    grid = (batch, DA_HEADS)
    side = _SideCasts(side_weights, grid)
    seq_bytes = _nbytes((seq, hb), proj.dtype)
    scratch = [pltpu.VMEM((2, tq, seq), F32), pltpu.VMEM((2, tq, seq), BF16)]
    limit = _vmem_limit(
        [seq_bytes] * 4 + side.window_bytes,
        [3 * _nbytes((2, tq, seq), F32), 3 * _nbytes((2, tq, seq), BF16)],
    )
    vec = pl.BlockSpec((1, DA_HEAD_DIM), lambda b, h: (0, 0))
    kernel = functools.partial(_diff_attn_kernel, n_side=len(side), tq=tq, lam_init=lam_init)
    outs = pl.pallas_call(
        kernel,
        grid=grid,
        in_specs=[
            pl.BlockSpec(memory_space=pltpu.SMEM),
            vec, vec, vec, vec,
            pl.BlockSpec((1, hb), lambda b, h: (0, 0)),
            pl.BlockSpec((seq, hb), lambda b, h: (b, q_blk0 + h)),
            pl.BlockSpec((seq, hb), lambda b, h: (b, k_blk0 + h)),
            pl.BlockSpec((seq, hb), lambda b, h: (b, v_blk0 + h)),
        ] + side.in_specs,
        out_specs=[pl.BlockSpec((seq, hb), lambda b, h: (b, h))] + side.out_specs,
        out_shape=[jax.ShapeDtypeStruct((batch * seq, DA_HEADS * hb), BF16)] + side.out_shapes,
        scratch_shapes=scratch,
        compiler_params=pltpu.CompilerParams(
            dimension_semantics=("arbitrary", "arbitrary"), vmem_limit_bytes=limit
        ),
        name="diff_attention",
    )(slopes, lq1.reshape(1, -1), lk1.reshape(1, -1), lq2.reshape(1, -1), lk2.reshape(1, -1),
      g.reshape(1, hb), proj, proj, proj, *side_weights)
    return tuple(outs)


def _retention_kernel(*refs, n_side):
    (lg_ref, q_ref, k_ref, v_ref, gate_ref), o_ref, _, side = _split_refs(refs, 5, n_side)
    _cast_blocks(side)
    s_len = q_ref.shape[0]
    c = RET_CHUNK
    dk, dv = RET_QK_DIM, RET_V_DIM
    scale = dk ** -0.5
    lg = lg_ref[pl.program_id(1)]

    row = lax.broadcasted_iota(jnp.int32, (c, c), 0)
    col = lax.broadcasted_iota(jnp.int32, (c, c), 1)
    diff = (row - col).astype(F32)
    intra = jnp.where(diff >= 0, jnp.exp(lg * jnp.maximum(diff, 0.0)), 0.0) * scale
    row_k = lax.broadcasted_iota(jnp.int32, (c, dk), 0).astype(F32)
    k_decay = jnp.exp(lg * (float(c - 1) - row_k)) * scale
    row_v = lax.broadcasted_iota(jnp.int32, (c, dv), 0).astype(F32)
    q_decay = jnp.exp(lg * (row_v + 1.0))
    chunk_decay = jnp.exp(jnp.full((1, dv), lg * float(c), F32))

    state = jnp.zeros((dk, dv), F32)
    for i in range(s_len // c):
        lo, hi = i * c, (i + 1) * c
        q = q_ref[lo:hi, :]
        k = k_ref[lo:hi, :]
        v = v_ref[lo:hi, :]
        scores = _dot_nt(q, k) * intra
        y = _dot(scores.astype(BF16), v)
        if i > 0:
            y = y + _dot(q, state.astype(BF16)) * q_decay
        if i + 1 < s_len // c:
            kd = (k.astype(F32) * k_decay).astype(BF16)
            state = state * chunk_decay + _dot_tn(kd, v)
        gate = gate_ref[lo:hi, :].astype(F32)
        o_ref[lo:hi, :] = (_rms(y) * (gate * jax.nn.sigmoid(gate))).astype(o_ref.dtype)


def _retention(proj, log_gammas, *, batch, seq, side_weights=()):
    da_cols = 3 * DA_HEADS * DA_V_DIM
    q_blk0 = da_cols // RET_QK_DIM
    k_blk0 = q_blk0 + RET_HEADS
    v_blk0 = (da_cols + 2 * RET_HEADS * RET_QK_DIM) // RET_V_DIM
    g_blk0 = v_blk0 + RET_HEADS
    grid = (batch, RET_HEADS)
    side = _SideCasts(side_weights, grid)
    limit = _vmem_limit(
        [_nbytes((seq, RET_QK_DIM), proj.dtype)] * 2 + [_nbytes((seq, RET_V_DIM), proj.dtype)] * 3
        + side.window_bytes,
        [16 * _nbytes((RET_CHUNK, RET_V_DIM), F32) * (seq // RET_CHUNK)],
    )
    outs = pl.pallas_call(
        functools.partial(_retention_kernel, n_side=len(side)),
        grid=grid,
        in_specs=[
            pl.BlockSpec(memory_space=pltpu.SMEM),
            pl.BlockSpec((seq, RET_QK_DIM), lambda b, h: (b, q_blk0 + h)),
            pl.BlockSpec((seq, RET_QK_DIM), lambda b, h: (b, k_blk0 + h)),
            pl.BlockSpec((seq, RET_V_DIM), lambda b, h: (b, v_blk0 + h)),
            pl.BlockSpec((seq, RET_V_DIM), lambda b, h: (b, g_blk0 + h)),
        ] + side.in_specs,
        out_specs=[pl.BlockSpec((seq, RET_V_DIM), lambda b, h: (b, h))] + side.out_specs,
        out_shape=[jax.ShapeDtypeStruct((batch * seq, RET_HEADS * RET_V_DIM), BF16)] + side.out_shapes,
        compiler_params=pltpu.CompilerParams(
            dimension_semantics=("arbitrary", "arbitrary"), vmem_limit_bytes=limit
        ),
        name="retention",
    )(log_gammas, proj, proj, proj, proj, *side_weights)
    return tuple(outs)


def _xattn_kernel(xq_ref, xk_ref, xv_ref, res_ref, wo_ref, o_ref, xo_ref):
    d_model = xq_ref.shape[1]
    hd = d_model // XATTN_HEADS
    scale2 = hd ** -0.5 * LOG2_E
    for h in range(XATTN_HEADS):
        cols = slice(h * hd, (h + 1) * hd)
        s = _dot_nt(xq_ref[:, cols], xk_ref[:, cols]) * scale2
        p = jnp.exp2(s - jnp.max(s, axis=-1, keepdims=True))
        p = p * (1.0 / jnp.sum(p, axis=-1, keepdims=True))
        xo_ref[:, cols] = _dot(p.astype(BF16), xv_ref[:, cols]).astype(xo_ref.dtype)
    o_ref[...] = res_ref[...] + _dot(xo_ref[...], wo_ref[...])


def _cross_attention(xq, xk, xv, res, wo, *, batch, seq, mem_len, tq):
    d = xq.shape[1]
    nq = seq // tq
    limit = _vmem_limit(
        [
            _nbytes((tq, d), xq.dtype),
            _nbytes((mem_len, d), xk.dtype),
            _nbytes((mem_len, d), xv.dtype),
            _nbytes((tq, d), F32),
            _nbytes((d, d), wo.dtype),
            _nbytes((tq, d), F32),
        ],
        [_nbytes((tq, d), BF16), _nbytes((tq, d), F32)],
    )
    return pl.pallas_call(
        _xattn_kernel,
        grid=(batch, nq),
        in_specs=[
            pl.BlockSpec((tq, d), lambda b, i: (b * nq + i, 0)),
            pl.BlockSpec((mem_len, d), lambda b, i: (b, 0)),
            pl.BlockSpec((mem_len, d), lambda b, i: (b, 0)),
            pl.BlockSpec((tq, d), lambda b, i: (b * nq + i, 0)),
            pl.BlockSpec((d, d), lambda b, i: (0, 0)),
        ],
        out_specs=pl.BlockSpec((tq, d), lambda b, i: (b * nq + i, 0)),
        out_shape=jax.ShapeDtypeStruct((batch * seq, d), F32),
        scratch_shapes=[pltpu.VMEM((tq, d), BF16)],
        compiler_params=pltpu.CompilerParams(
            dimension_semantics=("parallel", "arbitrary"), vmem_limit_bytes=limit
        ),
        name="cross_attention",
    )(xq, xk, xv, res, wo)


def _ffn_kernel(x_ref, g_ref, wg_ref, wu_ref, wd_ref, gf_ref, o_ref, h_ref, *, final_norm):
    f = pl.program_id(1)
    last = pl.num_programs(1) - 1
    blocks = _row_blocks(x_ref.shape[0])

    def partial_ffn(h):
        gate = _dot(h, wg_ref[...])
        up = _dot(h, wu_ref[...])
        act = (gate * jax.nn.sigmoid(gate)) * up
        return _dot(act.astype(BF16), wd_ref[...])

    @pl.when(f == 0)
    def _():
        for rows in blocks:
            x = x_ref[rows, :]
            h = (_rms(x) * g_ref[...]).astype(h_ref.dtype)
            h_ref[rows, :] = h
            o_ref[rows, :] = x + partial_ffn(h)

    if final_norm:
        @pl.when(jnp.logical_and(f > 0, f < last))
        def _():
            o_ref[...] += partial_ffn(h_ref[...])

        @pl.when(f == last)
        def _():
            for rows in blocks:
                y = o_ref[rows, :] + partial_ffn(h_ref[rows, :])
                o_ref[rows, :] = _rms(y) * gf_ref[...]
    else:
        @pl.when(f > 0)
        def _():
            o_ref[...] += partial_ffn(h_ref[...])


def _ffn(x, g, wg, wu, wd, gf, *, final_norm, tm, tf):
    m, d = x.shape
    d_ff = wg.shape[1]
    limit = _vmem_limit(
        [
            _nbytes((tm, d), F32),
            _nbytes((d, tf), wg.dtype),
            _nbytes((d, tf), wu.dtype),
            _nbytes((tf, d), wd.dtype),
            _nbytes((tm, d), F32),
        ],
        [_nbytes((tm, d), BF16), _nbytes((tm, d), F32), 4 * _nbytes((tm, tf), F32)],
    )
    return pl.pallas_call(
        functools.partial(_ffn_kernel, final_norm=final_norm),
        grid=(m // tm, d_ff // tf),
        in_specs=[
            pl.BlockSpec((tm, d), lambda i, f: (i, 0)),
            pl.BlockSpec((1, d), lambda i, f: (0, 0)),
            pl.BlockSpec((d, tf), lambda i, f: (0, f)),
            pl.BlockSpec((d, tf), lambda i, f: (0, f)),
            pl.BlockSpec((tf, d), lambda i, f: (f, 0)),
            pl.BlockSpec((1, d), lambda i, f: (0, 0)),
        ],
        out_specs=pl.BlockSpec((tm, d), lambda i, f: (i, 0)),
        out_shape=jax.ShapeDtypeStruct((m, d), F32),
        scratch_shapes=[pltpu.VMEM((tm, d), BF16)],
        compiler_params=pltpu.CompilerParams(
            dimension_semantics=("parallel", "arbitrary"), vmem_limit_bytes=limit
        ),
        name="swiglu_ffn",
    )(x, g.reshape(1, d), wg, wu, wd, gf.reshape(1, d))


def kernel(x, mem, norm_mix_g, w_in, lambda_q1, lambda_k1, lambda_q2, lambda_k2, da_subln_g, w_o, norm_x_g, norm_mem_g, w_xq, w_xk, w_xv, w_xo, norm_ffn_g, w_gate, w_up, w_down, norm_f_g):
    batch, seq, d_model = x.shape
    mem_len = mem.shape[1]
    depth = w_in.shape[0]
    slopes = jnp.asarray(2.0 ** (-8.0 * np.arange(1, DA_HEADS + 1) / DA_HEADS), dtype=F32)
    log_gammas = jnp.asarray(np.log(1.0 - 2.0 ** (-5.0 - np.arange(RET_HEADS))), dtype=F32)

    xf = x.reshape(batch * seq, d_model)
    memf = mem.reshape(batch * mem_len, d_model)
    for l in range(depth):
        lam_init = 0.8 - 0.6 * math.exp(-0.3 * l)
        proj, wb_o, wb_xq = _norm_matmul(
            xf, norm_mix_g[l], w_in[l], BF16, tm=1024, tn=1024, name="in_proj",
            side_weights=(w_o[l], w_xq[l]))
        da, wb_gate, wb_up = _diff_attention(
            proj, slopes, lambda_q1[l], lambda_k1[l], lambda_q2[l], lambda_k2[l], da_subln_g[l],
            batch=batch, seq=seq, lam_init=lam_init, tq=512, side_weights=(w_gate[l], w_up[l]))
        ret, wb_down = _retention(proj, log_gammas, batch=batch, seq=seq, side_weights=(w_down[l],))
        xf = _matmul2_res(da, ret, wb_o, xf, tm=1024, tn=1024, name="out_proj")

        xq, wb_xk, wb_xv, wb_xo = _norm_matmul(
            xf, norm_x_g[l], wb_xq, BF16, tm=1024, tn=1024, name="xattn_q",
            side_weights=(w_xk[l], w_xv[l], w_xo[l]))
        xk, = _norm_matmul(memf, norm_mem_g[l], wb_xk, BF16, tm=1024, tn=1024, name="xattn_k")
        xv, = _norm_matmul(memf, norm_mem_g[l], wb_xv, BF16, tm=1024, tn=1024, name="xattn_v")
        xf = _cross_attention(xq, xk, xv, xf, wb_xo, batch=batch, seq=seq, mem_len=mem_len, tq=512)

        xf = _ffn(xf, norm_ffn_g[l], wb_gate, wb_up, wb_down, norm_f_g,
                  final_norm=(l == depth - 1), tm=1024, tf=256)
    return xf.reshape(batch, seq, d_model)
```

```python
import functools
import math

import jax
import jax.numpy as jnp
import numpy as np
from jax import lax
from jax.experimental import pallas as pl
from jax.experimental.pallas import tpu as pltpu

F32 = jnp.float32
BF16 = jnp.bfloat16

DA_HEADS = 4
DA_HEAD_DIM = 128
DA_V_DIM = 2 * DA_HEAD_DIM
RET_HEADS = 4
RET_QK_DIM = 128
RET_V_DIM = 256
XATTN_HEADS = 4
RET_CHUNK = 256
NORM_EPS = 1e-6
NEG_INF = -1e30
LOG2_E = math.log2(math.e)
NORM_BLOCK_ROWS = 256

V7X_LANES = 128
MXU_COLS = 256
BF16_TILE_ROWS = 16
V7X_VMEM_BYTES = 64 * 1024 * 1024
V7X_VMEM_USABLE_BYTES = V7X_VMEM_BYTES - 8 * 1024 * 1024
COMPILER_SCRATCH_BYTES = 4 * 1024 * 1024


def _nbytes(shape, dtype):
    return int(np.prod(shape)) * jnp.dtype(dtype).itemsize


def _vmem_limit(pipelined, resident):
    need = 2 * sum(pipelined) + sum(resident) + COMPILER_SCRATCH_BYTES
    return int(min(V7X_VMEM_USABLE_BYTES, need))


def _rms(x):
    return x * lax.rsqrt(jnp.mean(x * x, axis=-1, keepdims=True) + NORM_EPS)


def _fold_lanes(x, op):
    tiles = [x[:, i:i + V7X_LANES] for i in range(0, x.shape[1], V7X_LANES)]
    return functools.reduce(op, tiles)


def _dot(a, b):
    return jnp.dot(a, b, preferred_element_type=F32)


def _dot_nt(a, b):
    return lax.dot_general(a, b, (((1,), (1,)), ((), ())), preferred_element_type=F32)


def _dot_tn(a, b):
    return lax.dot_general(a, b, (((0,), (0,)), ((), ())), preferred_element_type=F32)


class _SideCasts:
    def __init__(self, weights, grid):
        self.weights = list(weights)
        self.grid = tuple(grid)
        n_steps = int(np.prod(self.grid))
        self.plans = []
        for w in self.weights:
            rows, n_blocks = w.shape[0], n_steps
            while rows % n_blocks or (rows // n_blocks) % BF16_TILE_ROWS:
                n_blocks -= 1
            self.plans.append((n_blocks, rows // n_blocks))

    def __len__(self):
        return len(self.weights)

    def _specs(self):
        specs = []
        for w, (n_blocks, block_rows) in zip(self.weights, self.plans):
            def index(*ids, n_blocks=n_blocks):
                step = ids[0]
                for extent, idx in zip(self.grid[1:], ids[1:]):
                    step = step * extent + idx
                return (jnp.minimum(step, n_blocks - 1), 0)
            specs.append(pl.BlockSpec((block_rows, w.shape[1]), index))
        return specs

    in_specs = property(_specs)
    out_specs = property(_specs)

    @property
    def out_shapes(self):
        return [jax.ShapeDtypeStruct(w.shape, BF16) for w in self.weights]

    @property
    def window_bytes(self):
        return [_nbytes((rows, w.shape[1]), dt)
                for w, (_, rows) in zip(self.weights, self.plans) for dt in (w.dtype, BF16)]


def _split_refs(refs, n_in, n_side):
    ins, rest = refs[:n_in], refs[n_in:]
    side_in, rest = rest[:n_side], rest[n_side:]
    out, side_out, scratch = rest[0], rest[1:1 + n_side], rest[1 + n_side:]
    return ins, out, scratch, list(zip(side_in, side_out))


def _cast_blocks(pairs):
    for src, dst in pairs:
        dst[...] = src[...].astype(dst.dtype)


def _row_blocks(n_rows):
    step = min(NORM_BLOCK_ROWS, n_rows)
    return [slice(r, r + step) for r in range(0, n_rows, step)]


def _col_chunks(n_cols):
    step = min(MXU_COLS, n_cols)
    return [slice(c, c + step) for c in range(0, n_cols, step)]


def _norm_matmul_kernel(*refs, n_side, cast_w):
    (x_ref, g_ref, w_ref), o_ref, scratch, side = _split_refs(refs, 3, n_side)
    h_ref = scratch[0]
    wb_ref = scratch[1] if cast_w else w_ref
    n, t = pl.program_id(1), pl.program_id(2)

    @pl.when(n == 0)
    def _():
        _cast_blocks(side)
        if cast_w:
            @pl.when(t == 0)
            def _():
                wb_ref[...] = w_ref[...].astype(wb_ref.dtype)
        for rows in _row_blocks(x_ref.shape[0]):
            h = (_rms(x_ref[rows, :]) * g_ref[...]).astype(h_ref.dtype)
            h_ref[t, rows, :] = h
            o_ref[rows, :] = _dot(h, wb_ref[...]).astype(o_ref.dtype)

    @pl.when(n > 0)
    def _():
        _cast_blocks(side)
        if cast_w:
            @pl.when(t == 0)
            def _():
                for cols in _col_chunks(w_ref.shape[1]):
                    wb_ref[:, cols] = w_ref[:, cols].astype(wb_ref.dtype)
                    o_ref[:, cols] = _dot(h_ref[t], wb_ref[:, cols]).astype(o_ref.dtype)

            @pl.when(t > 0)
            def _():
                o_ref[...] = _dot(h_ref[t], wb_ref[...]).astype(o_ref.dtype)
        else:
            o_ref[...] = _dot(h_ref[t], w_ref[...]).astype(o_ref.dtype)


def _norm_matmul(x, g, w, out_dtype, *, tm, tn, name, group_tiles=1, side_weights=()):
    m, d = x.shape
    n = w.shape[1]
    tm, tn = min(tm, m), min(tn, n)
    gt = group_tiles
    grid = (m // (tm * gt), n // tn, gt)
    side = _SideCasts(side_weights, grid)
    cast_w = w.dtype != BF16
    scratch = [pltpu.VMEM((gt, tm, d), BF16)] + ([pltpu.VMEM((d, tn), BF16)] if cast_w else [])
    limit = _vmem_limit(
        [_nbytes((tm, d), x.dtype), _nbytes((d, tn), w.dtype), _nbytes((tm, tn), out_dtype)]
        + side.window_bytes,
        [_nbytes((gt, tm, d), BF16), _nbytes((d, tn), BF16) * cast_w],
    )
    outs = pl.pallas_call(
        functools.partial(_norm_matmul_kernel, n_side=len(side), cast_w=cast_w),
        grid=grid,
        in_specs=[
            pl.BlockSpec((tm, d), lambda gi, j, t: (gi * gt + jnp.where(j == 0, t, gt - 1), 0)),
            pl.BlockSpec((1, d), lambda gi, j, t: (0, 0)),
            pl.BlockSpec((d, tn), lambda gi, j, t: (0, j)),
        ] + side.in_specs,
        out_specs=[pl.BlockSpec((tm, tn), lambda gi, j, t: (gi * gt + t, j))] + side.out_specs,
        out_shape=[jax.ShapeDtypeStruct((m, n), out_dtype)] + side.out_shapes,
        scratch_shapes=scratch,
        compiler_params=pltpu.CompilerParams(
            dimension_semantics=("arbitrary", "arbitrary", "arbitrary"), vmem_limit_bytes=limit
        ),
        name=name,
    )(x, g.reshape(1, d), w, *side_weights)
    return tuple(outs)


def _matmul2_res_kernel(a1_ref, a2_ref, w1_ref, w2_ref, res_ref, o_ref):
    acc = _dot(a1_ref[...], w1_ref[...]) + _dot(a2_ref[...], w2_ref[...])
    o_ref[...] = res_ref[...] + acc


def _matmul2_res(a1, a2, w, res, *, tm, tn, name):
    m, k1 = a1.shape
    k2 = a2.shape[1]
    assert k1 == k2 and w.shape[0] == k1 + k2
    n = w.shape[1]
    limit = _vmem_limit(
        [
            _nbytes((tm, k1), a1.dtype),
            _nbytes((tm, k2), a2.dtype),
            _nbytes((k1, tn), w.dtype),
            _nbytes((k2, tn), w.dtype),
            _nbytes((tm, tn), F32),
            _nbytes((tm, tn), F32),
        ],
        [_nbytes((tm, tn), F32)],
    )
    return pl.pallas_call(
        _matmul2_res_kernel,
        grid=(m // tm, n // tn),
        in_specs=[
            pl.BlockSpec((tm, k1), lambda i, j: (i, 0)),
            pl.BlockSpec((tm, k2), lambda i, j: (i, 0)),
            pl.BlockSpec((k1, tn), lambda i, j: (0, j)),
            pl.BlockSpec((k2, tn), lambda i, j: (1, j)),
            pl.BlockSpec((tm, tn), lambda i, j: (i, j)),
        ],
        out_specs=pl.BlockSpec((tm, tn), lambda i, j: (i, j)),
        out_shape=jax.ShapeDtypeStruct((m, n), F32),
        compiler_params=pltpu.CompilerParams(
            dimension_semantics=("parallel", "arbitrary"), vmem_limit_bytes=limit
        ),
        name=name,
    )(a1, a2, w, w, res)


def _diff_attn_kernel(*refs, n_side, tq, lam_init):
    ins, o_ref, (s_ref, p_ref), side = _split_refs(refs, 9, n_side)
    slope_ref, lq1_ref, lk1_ref, lq2_ref, lk2_ref, g_ref, q_ref, k_ref, v_ref = ins
    _cast_blocks(side)
    s_len = q_ref.shape[0]
    d = DA_HEAD_DIM
    scale2 = d ** -0.5 * LOG2_E
    slope2 = slope_ref[pl.program_id(1)] * LOG2_E
    lam = (jnp.exp(jnp.sum(lq1_ref[...] * lk1_ref[...], axis=-1, keepdims=True))
           - jnp.exp(jnp.sum(lq2_ref[...] * lk2_ref[...], axis=-1, keepdims=True))
           + lam_init)

    row = lax.broadcasted_iota(jnp.int32, (tq, tq), 0)
    col = lax.broadcasted_iota(jnp.int32, (tq, tq), 1)
    causal = col <= row
    diag_bias = slope2 * (col - (tq - 1)).astype(F32)
    kpos = lax.broadcasted_iota(jnp.int32, (1, s_len), 1).astype(F32)

    for qi in range(s_len // tq):
        lo, hi = qi * tq, (qi + 1) * tq
        key_blocks = [slice(j * tq, (j + 1) * tq) for j in range(qi + 1)]
        heads = []
        for c in range(2):
            dcols = slice(c * d, (c + 1) * d)
            qc = q_ref[lo:hi, dcols]
            m = None
            for j, cols in enumerate(key_blocks):
                s = _dot_nt(qc, k_ref[cols, dcols]) * scale2
                if j == qi:
                    s = jnp.where(causal, s + diag_bias, NEG_INF)
                else:
                    s = s + slope2 * (kpos[:, cols] - float(hi - 1))
                s_ref[c, :, cols] = s
                bm = _fold_lanes(s, jnp.maximum)
                m = bm if m is None else jnp.maximum(m, bm)
            m = jnp.max(m, axis=-1, keepdims=True)
            l = None
            for cols in key_blocks:
                p = jnp.exp2(s_ref[c, :, cols] - m)
                bl = _fold_lanes(p, jnp.add)
                l = bl if l is None else l + bl
                p_ref[c, :, cols] = p.astype(p_ref.dtype)
            l = jnp.sum(l, axis=-1, keepdims=True)
            acc = _dot(p_ref[c, :, 0:hi], v_ref[0:hi, :])
            heads.append((acc, l))
        (acc1, l1), (acc2, l2) = heads
        out = acc1 * (1.0 / l1) - acc2 * (lam / l2)
        y = _rms(out) * g_ref[...] * (1.0 - lam_init)
        o_ref[lo:hi, :] = y.astype(o_ref.dtype)


def _diff_attention(proj, slopes, lq1, lk1, lq2, lk2, g, *, batch, seq, lam_init, tq,
                    side_weights=()):
    hb = DA_V_DIM
    q_blk0, k_blk0, v_blk0 = 0, DA_HEADS, 2 * DA_HEADS
    grid = (batch, DA_HEADS)
    side = _SideCasts(side_weights, grid)
    seq_bytes = _nbytes((seq, hb), proj.dtype)
    scratch = [pltpu.VMEM((2, tq, seq), F32), pltpu.VMEM((2, tq, seq), BF16)]
    limit = _vmem_limit(
        [seq_bytes] * 4 + side.window_bytes,
        [3 * _nbytes((2, tq, seq), F32), 3 * _nbytes((2, tq, seq), BF16)],
    )
    vec = pl.BlockSpec((1, DA_HEAD_DIM), lambda b, h: (0, 0))
    kernel = functools.partial(_diff_attn_kernel, n_side=len(side), tq=tq, lam_init=lam_init)
    outs = pl.pallas_call(
        kernel,
        grid=grid,
        in_specs=[
            pl.BlockSpec(memory_space=pltpu.SMEM),
            vec, vec, vec, vec,
            pl.BlockSpec((1, hb), lambda b, h: (0, 0)),
            pl.BlockSpec((seq, hb), lambda b, h: (b, q_blk0 + h)),
            pl.BlockSpec((seq, hb), lambda b, h: (b, k_blk0 + h)),
            pl.BlockSpec((seq, hb), lambda b, h: (b, v_blk0 + h)),
        ] + side.in_specs,
        out_specs=[pl.BlockSpec((seq, hb), lambda b, h: (b, h))] + side.out_specs,
        out_shape=[jax.ShapeDtypeStruct((batch * seq, DA_HEADS * hb), BF16)] + side.out_shapes,
        scratch_shapes=scratch,
        compiler_params=pltpu.CompilerParams(
            dimension_semantics=("arbitrary", "arbitrary"), vmem_limit_bytes=limit
        ),
        name="diff_attention",
    )(slopes, lq1.reshape(1, -1), lk1.reshape(1, -1), lq2.reshape(1, -1), lk2.reshape(1, -1),
      g.reshape(1, hb), proj, proj, proj, *side_weights)
    return tuple(outs)


def _retention_kernel(*refs, n_side):
    (lg_ref, q_ref, k_ref, v_ref, gate_ref), o_ref, _, side = _split_refs(refs, 5, n_side)
    _cast_blocks(side)
    s_len = q_ref.shape[0]
    c = RET_CHUNK
    dk, dv = RET_QK_DIM, RET_V_DIM
    scale = dk ** -0.5
    lg = lg_ref[pl.program_id(1)]

    row = lax.broadcasted_iota(jnp.int32, (c, c), 0)
    col = lax.broadcasted_iota(jnp.int32, (c, c), 1)
    diff = (row - col).astype(F32)
    intra = jnp.where(diff >= 0, jnp.exp(lg * jnp.maximum(diff, 0.0)), 0.0) * scale
    row_k = lax.broadcasted_iota(jnp.int32, (c, dk), 0).astype(F32)
    k_decay = jnp.exp(lg * (float(c - 1) - row_k)) * scale
    row_v = lax.broadcasted_iota(jnp.int32, (c, dv), 0).astype(F32)
    q_decay = jnp.exp(lg * (row_v + 1.0))
    chunk_decay = jnp.exp(jnp.full((1, dv), lg * float(c), F32))

    state = jnp.zeros((dk, dv), F32)
    for i in range(s_len // c):
        lo, hi = i * c, (i + 1) * c
        q = q_ref[lo:hi, :]
        k = k_ref[lo:hi, :]
        v = v_ref[lo:hi, :]
        scores = _dot_nt(q, k) * intra
        y = _dot(scores.astype(BF16), v)
        if i > 0:
            y = y + _dot(q, state.astype(BF16)) * q_decay
        if i + 1 < s_len // c:
            kd = (k.astype(F32) * k_decay).astype(BF16)
            state = state * chunk_decay + _dot_tn(kd, v)
        gate = gate_ref[lo:hi, :].astype(F32)
        o_ref[lo:hi, :] = (_rms(y) * (gate * jax.nn.sigmoid(gate))).astype(o_ref.dtype)


def _retention(proj, log_gammas, *, batch, seq, side_weights=()):
    da_cols = 3 * DA_HEADS * DA_V_DIM
    q_blk0 = da_cols // RET_QK_DIM
    k_blk0 = q_blk0 + RET_HEADS
    v_blk0 = (da_cols + 2 * RET_HEADS * RET_QK_DIM) // RET_V_DIM
    g_blk0 = v_blk0 + RET_HEADS
    grid = (batch, RET_HEADS)
    side = _SideCasts(side_weights, grid)
    limit = _vmem_limit(
        [_nbytes((seq, RET_QK_DIM), proj.dtype)] * 2 + [_nbytes((seq, RET_V_DIM), proj.dtype)] * 3
        + side.window_bytes,
        [16 * _nbytes((RET_CHUNK, RET_V_DIM), F32) * (seq // RET_CHUNK)],
    )
    outs = pl.pallas_call(
        functools.partial(_retention_kernel, n_side=len(side)),
        grid=grid,
        in_specs=[
            pl.BlockSpec(memory_space=pltpu.SMEM),
            pl.BlockSpec((seq, RET_QK_DIM), lambda b, h: (b, q_blk0 + h)),
            pl.BlockSpec((seq, RET_QK_DIM), lambda b, h: (b, k_blk0 + h)),
            pl.BlockSpec((seq, RET_V_DIM), lambda b, h: (b, v_blk0 + h)),
            pl.BlockSpec((seq, RET_V_DIM), lambda b, h: (b, g_blk0 + h)),
        ] + side.in_specs,
        out_specs=[pl.BlockSpec((seq, RET_V_DIM), lambda b, h: (b, h))] + side.out_specs,
        out_shape=[jax.ShapeDtypeStruct((batch * seq, RET_HEADS * RET_V_DIM), BF16)] + side.out_shapes,
        compiler_params=pltpu.CompilerParams(
            dimension_semantics=("arbitrary", "arbitrary"), vmem_limit_bytes=limit
        ),
        name="retention",
    )(log_gammas, proj, proj, proj, proj, *side_weights)
    return tuple(outs)


def _xattn_kernel(*refs, n_side):
    ins, o_ref, (xo_ref,), side = _split_refs(refs, 5, n_side)
    xq_ref, xk_ref, xv_ref, res_ref, wo_ref = ins
    _cast_blocks(side)
    d_model = xq_ref.shape[1]
    hd = d_model // XATTN_HEADS
    scale2 = hd ** -0.5 * LOG2_E
    for h in range(XATTN_HEADS):
        cols = slice(h * hd, (h + 1) * hd)
        s = _dot_nt(xq_ref[:, cols], xk_ref[:, cols]) * scale2
        p = jnp.exp2(s - jnp.max(s, axis=-1, keepdims=True))
        p = p * (1.0 / jnp.sum(p, axis=-1, keepdims=True))
        xo_ref[:, cols] = _dot(p.astype(BF16), xv_ref[:, cols]).astype(xo_ref.dtype)
    o_ref[...] = res_ref[...] + _dot(xo_ref[...], wo_ref[...])


def _cross_attention(xq, xk, xv, res, wo, *, batch, seq, mem_len, tq, side_weights=()):
    d = xq.shape[1]
    nq = seq // tq
    grid = (batch, nq)
    side = _SideCasts(side_weights, grid)
    limit = _vmem_limit(
        [
            _nbytes((tq, d), xq.dtype),
            _nbytes((mem_len, d), xk.dtype),
            _nbytes((mem_len, d), xv.dtype),
            _nbytes((tq, d), F32),
            _nbytes((d, d), wo.dtype),
            _nbytes((tq, d), F32),
        ] + side.window_bytes,
        [_nbytes((tq, d), BF16), _nbytes((tq, d), F32)],
    )
    outs = pl.pallas_call(
        functools.partial(_xattn_kernel, n_side=len(side)),
        grid=grid,
        in_specs=[
            pl.BlockSpec((tq, d), lambda b, i: (b * nq + i, 0)),
            pl.BlockSpec((mem_len, d), lambda b, i: (b, 0)),
            pl.BlockSpec((mem_len, d), lambda b, i: (b, 0)),
            pl.BlockSpec((tq, d), lambda b, i: (b * nq + i, 0)),
            pl.BlockSpec((d, d), lambda b, i: (0, 0)),
        ] + side.in_specs,
        out_specs=[pl.BlockSpec((tq, d), lambda b, i: (b * nq + i, 0))] + side.out_specs,
        out_shape=[jax.ShapeDtypeStruct((batch * seq, d), F32)] + side.out_shapes,
        scratch_shapes=[pltpu.VMEM((tq, d), BF16)],
        compiler_params=pltpu.CompilerParams(
            dimension_semantics=("arbitrary", "arbitrary"), vmem_limit_bytes=limit
        ),
        name="cross_attention",
    )(xq, xk, xv, res, wo, *side_weights)
    return tuple(outs)


def _ffn_kernel(x_ref, g_ref, wg_ref, wu_ref, wd_ref, gf_ref, o_ref, h_ref, *, final_norm):
    f = pl.program_id(1)
    last = pl.num_programs(1) - 1
    blocks = _row_blocks(x_ref.shape[0])

    def partial_ffn(h):
        gate = _dot(h, wg_ref[...])
        up = _dot(h, wu_ref[...])
        act = (gate * jax.nn.sigmoid(gate)) * up
        return _dot(act.astype(BF16), wd_ref[...])

    @pl.when(f == 0)
    def _():
        for rows in blocks:
            x = x_ref[rows, :]
            h = (_rms(x) * g_ref[...]).astype(h_ref.dtype)
            h_ref[rows, :] = h
            o_ref[rows, :] = x + partial_ffn(h)

    if final_norm:
        @pl.when(jnp.logical_and(f > 0, f < last))
        def _():
            o_ref[...] += partial_ffn(h_ref[...])

        @pl.when(f == last)
        def _():
            for rows in blocks:
                y = o_ref[rows, :] + partial_ffn(h_ref[rows, :])
                o_ref[rows, :] = _rms(y) * gf_ref[...]
    else:
        @pl.when(f > 0)
        def _():
            o_ref[...] += partial_ffn(h_ref[...])


def _ffn(x, g, wg, wu, wd, gf, *, final_norm, tm, tf):
    m, d = x.shape
    d_ff = wg.shape[1]
    limit = _vmem_limit(
        [
            _nbytes((tm, d), F32),
            _nbytes((d, tf), wg.dtype),
            _nbytes((d, tf), wu.dtype),
            _nbytes((tf, d), wd.dtype),
            _nbytes((tm, d), F32),
        ],
        [_nbytes((tm, d), BF16), _nbytes((tm, d), F32), 4 * _nbytes((tm, tf), F32)],
    )
    return pl.pallas_call(
        functools.partial(_ffn_kernel, final_norm=final_norm),
        grid=(m // tm, d_ff // tf),
        in_specs=[
            pl.BlockSpec((tm, d), lambda i, f: (i, 0)),
            pl.BlockSpec((1, d), lambda i, f: (0, 0)),
            pl.BlockSpec((d, tf), lambda i, f: (0, f)),
            pl.BlockSpec((d, tf), lambda i, f: (0, f)),
            pl.BlockSpec((tf, d), lambda i, f: (f, 0)),
            pl.BlockSpec((1, d), lambda i, f: (0, 0)),
        ],
        out_specs=pl.BlockSpec((tm, d), lambda i, f: (i, 0)),
        out_shape=jax.ShapeDtypeStruct((m, d), F32),
        scratch_shapes=[pltpu.VMEM((tm, d), BF16)],
        compiler_params=pltpu.CompilerParams(
            dimension_semantics=("parallel", "arbitrary"), vmem_limit_bytes=limit
        ),
        name="swiglu_ffn",
    )(x, g.reshape(1, d), wg, wu, wd, gf.reshape(1, d))


def kernel(x, mem, norm_mix_g, w_in, lambda_q1, lambda_k1, lambda_q2, lambda_k2, da_subln_g, w_o, norm_x_g, norm_mem_g, w_xq, w_xk, w_xv, w_xo, norm_ffn_g, w_gate, w_up, w_down, norm_f_g):
    batch, seq, d_model = x.shape
    mem_len = mem.shape[1]
    depth = w_in.shape[0]
    slopes = jnp.asarray(2.0 ** (-8.0 * np.arange(1, DA_HEADS + 1) / DA_HEADS), dtype=F32)
    log_gammas = jnp.asarray(np.log(1.0 - 2.0 ** (-5.0 - np.arange(RET_HEADS))), dtype=F32)

    xf = x.reshape(batch * seq, d_model)
    memf = mem.reshape(batch * mem_len, d_model)
    for l in range(depth):
        lam_init = 0.8 - 0.6 * math.exp(-0.3 * l)
        proj, wb_o, wb_xq = _norm_matmul(
            xf, norm_mix_g[l], w_in[l], BF16, tm=1024, tn=1024, group_tiles=2, name="in_proj",
            side_weights=(w_o[l], w_xq[l]))
        da, wb_gate, wb_up, wb_xk, wb_xv, wb_xo = _diff_attention(
            proj, slopes, lambda_q1[l], lambda_k1[l], lambda_q2[l], lambda_k2[l], da_subln_g[l],
            batch=batch, seq=seq, lam_init=lam_init, tq=512,
            side_weights=(w_gate[l], w_up[l], w_xk[l], w_xv[l], w_xo[l]))
        ret, = _retention(proj, log_gammas, batch=batch, seq=seq)
        xf = _matmul2_res(da, ret, wb_o, xf, tm=1024, tn=1024, name="out_proj")

        xq, = _norm_matmul(xf, norm_x_g[l], wb_xq, BF16, tm=1024, tn=1024, name="xattn_q")
        xk, = _norm_matmul(memf, norm_mem_g[l], wb_xk, BF16, tm=1024, tn=1024, name="xattn_k")
        xv, = _norm_matmul(memf, norm_mem_g[l], wb_xv, BF16, tm=1024, tn=1024, name="xattn_v")
        xf, wb_down = _cross_attention(xq, xk, xv, xf, wb_xo, batch=batch, seq=seq, mem_len=mem_len,
                                       tq=512, side_weights=(w_down[l],))

        xf = _ffn(xf, norm_ffn_g[l], wb_gate, wb_up, wb_down, norm_f_g,
                  final_norm=(l == depth - 1), tm=1024, tf=256)
    return xf.reshape(batch, seq, d_model)
```

```python
import functools
import math

import jax
import jax.numpy as jnp
import numpy as np
from jax import lax
from jax.experimental import pallas as pl
from jax.experimental.pallas import tpu as pltpu

F32 = jnp.float32
BF16 = jnp.bfloat16

DA_HEADS = 4
DA_HEAD_DIM = 128
DA_V_DIM = 2 * DA_HEAD_DIM
RET_HEADS = 4
RET_QK_DIM = 128
RET_V_DIM = 256
XATTN_HEADS = 4
RET_CHUNK = 256
NORM_EPS = 1e-6
NEG_INF = -1e30
LOG2_E = math.log2(math.e)
NORM_BLOCK_ROWS = 256

V7X_LANES = 128
MXU_COLS = 256
BF16_TILE_ROWS = 16
V7X_VMEM_BYTES = 64 * 1024 * 1024
V7X_VMEM_USABLE_BYTES = V7X_VMEM_BYTES - 8 * 1024 * 1024
COMPILER_SCRATCH_BYTES = 4 * 1024 * 1024


def _nbytes(shape, dtype):
    return int(np.prod(shape)) * jnp.dtype(dtype).itemsize


def _vmem_limit(pipelined, resident):
    need = 2 * sum(pipelined) + sum(resident) + COMPILER_SCRATCH_BYTES
    return int(min(V7X_VMEM_USABLE_BYTES, need))


def _rms(x):
    return x * lax.rsqrt(jnp.mean(x * x, axis=-1, keepdims=True) + NORM_EPS)


def _fold_lanes(x, op):
    tiles = [x[:, i:i + V7X_LANES] for i in range(0, x.shape[1], V7X_LANES)]
    return functools.reduce(op, tiles)


def _dot(a, b):
    return jnp.dot(a, b, preferred_element_type=F32)


def _dot_nt(a, b):
    return lax.dot_general(a, b, (((1,), (1,)), ((), ())), preferred_element_type=F32)


def _dot_tn(a, b):
    return lax.dot_general(a, b, (((0,), (0,)), ((), ())), preferred_element_type=F32)


class _SideCasts:
    def __init__(self, weights, grid):
        self.weights = list(weights)
        self.grid = tuple(grid)
        n_steps = int(np.prod(self.grid))
        self.plans = []
        for w in self.weights:
            rows, n_blocks = w.shape[0], n_steps
            while rows % n_blocks or (rows // n_blocks) % BF16_TILE_ROWS:
                n_blocks -= 1
            self.plans.append((n_blocks, rows // n_blocks))

    def __len__(self):
        return len(self.weights)

    def _specs(self):
        specs = []
        for w, (n_blocks, block_rows) in zip(self.weights, self.plans):
            def index(*ids, n_blocks=n_blocks):
                step = ids[0]
                for extent, idx in zip(self.grid[1:], ids[1:]):
                    step = step * extent + idx
                return (jnp.minimum(step, n_blocks - 1), 0)
            specs.append(pl.BlockSpec((block_rows, w.shape[1]), index))
        return specs

    in_specs = property(_specs)
    out_specs = property(_specs)

    @property
    def out_shapes(self):
        return [jax.ShapeDtypeStruct(w.shape, BF16) for w in self.weights]

    @property
    def window_bytes(self):
        return [_nbytes((rows, w.shape[1]), dt)
                for w, (_, rows) in zip(self.weights, self.plans) for dt in (w.dtype, BF16)]


def _split_refs(refs, n_in, n_side):
    ins, rest = refs[:n_in], refs[n_in:]
    side_in, rest = rest[:n_side], rest[n_side:]
    out, side_out, scratch = rest[0], rest[1:1 + n_side], rest[1 + n_side:]
    return ins, out, scratch, list(zip(side_in, side_out))


def _cast_blocks(pairs):
    for src, dst in pairs:
        dst[...] = src[...].astype(dst.dtype)


def _row_blocks(n_rows):
    step = min(NORM_BLOCK_ROWS, n_rows)
    return [slice(r, r + step) for r in range(0, n_rows, step)]


def _col_chunks(n_cols):
    step = min(MXU_COLS, n_cols)
    return [slice(c, c + step) for c in range(0, n_cols, step)]


def _norm_matmul_kernel(*refs, n_side, cast_w):
    ins, o_ref, scratch, side = _split_refs(refs, 4 if cast_w else 3, n_side)
    x_ref, g_ref, w_ref = ins[:3]
    h_ref = scratch[0]
    wb_ref = scratch[1] if cast_w else w_ref
    n, t = pl.program_id(1), pl.program_id(2)

    def cast_weight(cols):
        wb_ref[:, cols] = (w_ref[:, cols] * ins[3][:, cols]).astype(wb_ref.dtype)

    @pl.when(n == 0)
    def _():
        _cast_blocks(side)
        if cast_w:
            @pl.when(t == 0)
            def _():
                cast_weight(slice(None))
        for rows in _row_blocks(x_ref.shape[0]):
            h = (_rms(x_ref[rows, :]) * g_ref[...]).astype(h_ref.dtype)
            h_ref[t, rows, :] = h
            o_ref[rows, :] = _dot(h, wb_ref[...]).astype(o_ref.dtype)

    @pl.when(n > 0)
    def _():
        _cast_blocks(side)
        if cast_w:
            @pl.when(t == 0)
            def _():
                for cols in _col_chunks(w_ref.shape[1]):
                    cast_weight(cols)
                    o_ref[:, cols] = _dot(h_ref[t], wb_ref[:, cols]).astype(o_ref.dtype)

            @pl.when(t > 0)
            def _():
                o_ref[...] = _dot(h_ref[t], wb_ref[...]).astype(o_ref.dtype)
        else:
            o_ref[...] = _dot(h_ref[t], w_ref[...]).astype(o_ref.dtype)


def _norm_matmul(x, g, w, out_dtype, *, tm, tn, name, group_tiles=1, col_scale=None,
                 side_weights=()):
    m, d = x.shape
    n = w.shape[1]
    tm, tn = min(tm, m), min(tn, n)
    gt = group_tiles
    grid = (m // (tm * gt), n // tn, gt)
    side = _SideCasts(side_weights, grid)
    cast_w = w.dtype != BF16
    assert cast_w or col_scale is None
    weight_ins, weight_specs = [w], [pl.BlockSpec((d, tn), lambda gi, j, t: (0, j))]
    if cast_w:
        cs = jnp.ones((n,), F32) if col_scale is None else col_scale
        weight_ins.append(cs.reshape(1, n))
        weight_specs.append(pl.BlockSpec((1, tn), lambda gi, j, t: (0, j)))
    scratch = [pltpu.VMEM((gt, tm, d), BF16)] + ([pltpu.VMEM((d, tn), BF16)] if cast_w else [])
    limit = _vmem_limit(
        [_nbytes((tm, d), x.dtype), _nbytes((d, tn), w.dtype), _nbytes((tm, tn), out_dtype)]
        + side.window_bytes,
        [_nbytes((gt, tm, d), BF16), _nbytes((d, tn), BF16) * cast_w],
    )
    outs = pl.pallas_call(
        functools.partial(_norm_matmul_kernel, n_side=len(side), cast_w=cast_w),
        grid=grid,
        in_specs=[
            pl.BlockSpec((tm, d), lambda gi, j, t: (gi * gt + jnp.where(j == 0, t, gt - 1), 0)),
            pl.BlockSpec((1, d), lambda gi, j, t: (0, 0)),
        ] + weight_specs + side.in_specs,
        out_specs=[pl.BlockSpec((tm, tn), lambda gi, j, t: (gi * gt + t, j))] + side.out_specs,
        out_shape=[jax.ShapeDtypeStruct((m, n), out_dtype)] + side.out_shapes,
        scratch_shapes=scratch,
        compiler_params=pltpu.CompilerParams(
            dimension_semantics=("arbitrary", "arbitrary", "arbitrary"), vmem_limit_bytes=limit
        ),
        name=name,
    )(x, g.reshape(1, d), *weight_ins, *side_weights)
    return tuple(outs)


def _matmul2_res_kernel(a1_ref, a2_ref, w1_ref, w2_ref, res_ref, o_ref):
    acc = _dot(a1_ref[...], w1_ref[...]) + _dot(a2_ref[...], w2_ref[...])
    o_ref[...] = res_ref[...] + acc


def _matmul2_res(a1, a2, w, res, *, tm, tn, name):
    m, k1 = a1.shape
    k2 = a2.shape[1]
    assert k1 == k2 and w.shape[0] == k1 + k2
    n = w.shape[1]
    limit = _vmem_limit(
        [
            _nbytes((tm, k1), a1.dtype),
            _nbytes((tm, k2), a2.dtype),
            _nbytes((k1, tn), w.dtype),
            _nbytes((k2, tn), w.dtype),
            _nbytes((tm, tn), F32),
            _nbytes((tm, tn), F32),
        ],
        [_nbytes((tm, tn), F32)],
    )
    return pl.pallas_call(
        _matmul2_res_kernel,
        grid=(m // tm, n // tn),
        in_specs=[
            pl.BlockSpec((tm, k1), lambda i, j: (i, 0)),
            pl.BlockSpec((tm, k2), lambda i, j: (i, 0)),
            pl.BlockSpec((k1, tn), lambda i, j: (0, j)),
            pl.BlockSpec((k2, tn), lambda i, j: (1, j)),
            pl.BlockSpec((tm, tn), lambda i, j: (i, j)),
        ],
        out_specs=pl.BlockSpec((tm, tn), lambda i, j: (i, j)),
        out_shape=jax.ShapeDtypeStruct((m, n), F32),
        compiler_params=pltpu.CompilerParams(
            dimension_semantics=("parallel", "arbitrary"), vmem_limit_bytes=limit
        ),
        name=name,
    )(a1, a2, w, w, res)


def _bf16_part(x):
    bits = lax.bitcast_convert_type(x, jnp.int32) & jnp.int32(-65536)
    return lax.bitcast_convert_type(bits, F32)


def _diff_attn_kernel(*refs, n_side, tq, lam_init):
    ins, o_ref, (kx_ref, s_ref, p_ref), side = _split_refs(refs, 9, n_side)
    slope_ref, lq1_ref, lk1_ref, lq2_ref, lk2_ref, g_ref, q_ref, k_ref, v_ref = ins
    _cast_blocks(side)
    s_len = q_ref.shape[0]
    d = DA_HEAD_DIM
    lam = (jnp.exp(jnp.sum(lq1_ref[...] * lk1_ref[...], axis=-1, keepdims=True))
           - jnp.exp(jnp.sum(lq2_ref[...] * lk2_ref[...], axis=-1, keepdims=True))
           + lam_init)

    lane = lax.broadcasted_iota(jnp.int32, (1, d), 1)
    slope2 = jnp.full((1, d), slope_ref[pl.program_id(1)] * LOG2_E, F32)
    piece_hi = _bf16_part(slope2)
    rest = slope2 - piece_hi
    piece_mid = _bf16_part(rest)
    piece_lo = _bf16_part(rest - piece_mid)
    piece = jnp.where((lane == 0) | (lane == 3), piece_hi,
                      jnp.where((lane == 1) | (lane == 4), piece_mid, piece_lo))
    q_extra = jnp.where(lane < 3, piece * 256.0, jnp.where(lane < 6, piece, 0.0))
    q_extra = jnp.broadcast_to(q_extra, (tq, d)).astype(BF16)
    kpos = lax.broadcasted_iota(jnp.int32, (s_len, d), 0)
    klane = lax.broadcasted_iota(jnp.int32, (s_len, d), 1)
    k_extra = jnp.where(klane < 3, kpos >> 8, jnp.where(klane < 6, kpos & 255, 0))
    kx_ref[...] = k_extra.astype(F32).astype(kx_ref.dtype)

    row = lax.broadcasted_iota(jnp.int32, (tq, tq), 0)
    col = lax.broadcasted_iota(jnp.int32, (tq, tq), 1)
    causal = col <= row

    for qi in reversed(range(s_len // tq)):
        lo, hi = qi * tq, (qi + 1) * tq
        key_blocks = [slice(j * tq, (j + 1) * tq) for j in range(qi + 1)]
        heads = []
        for c in range(2):
            dcols = slice(c * d, (c + 1) * d)
            q_aug = jnp.concatenate([q_ref[lo:hi, dcols], q_extra], axis=1)
            m = None
            for j, cols in enumerate(key_blocks):
                k_aug = jnp.concatenate([k_ref[cols, dcols], kx_ref[cols, :]], axis=1)
                s = _dot_nt(q_aug, k_aug)
                if j == qi:
                    s = jnp.where(causal, s, NEG_INF)
                s_ref[c, :, cols] = s
                bm = _fold_lanes(s, jnp.maximum)
                m = bm if m is None else jnp.maximum(m, bm)
            m = jnp.max(m, axis=-1, keepdims=True)
            l = None
            for cols in key_blocks:
                p = jnp.exp2(s_ref[c, :, cols] - m)
                bl = _fold_lanes(p, jnp.add)
                l = bl if l is None else l + bl
                p_ref[c, :, cols] = p.astype(p_ref.dtype)
            l = jnp.sum(l, axis=-1, keepdims=True)
            acc = _dot(p_ref[c, :, 0:hi], v_ref[0:hi, :])
            heads.append((acc, l))
        (acc1, l1), (acc2, l2) = heads
        out = acc1 * (1.0 / l1) - acc2 * (lam / l2)
        y = _rms(out) * g_ref[...] * (1.0 - lam_init)
        o_ref[lo:hi, :] = y.astype(o_ref.dtype)


def _diff_attention(proj, slopes, lq1, lk1, lq2, lk2, g, *, batch, seq, lam_init, tq,
                    side_weights=()):
    hb = DA_V_DIM
    q_blk0, k_blk0, v_blk0 = 0, DA_HEADS, 2 * DA_HEADS
    grid = (batch, DA_HEADS)
    side = _SideCasts(side_weights, grid)
    seq_bytes = _nbytes((seq, hb), proj.dtype)
    scratch = [
        pltpu.VMEM((seq, DA_HEAD_DIM), BF16),
        pltpu.VMEM((2, tq, seq), F32),
        pltpu.VMEM((2, tq, seq), BF16),
    ]
    limit = _vmem_limit(
        [seq_bytes] * 4 + side.window_bytes,
        [_nbytes((seq, DA_HEAD_DIM), BF16),
         3 * _nbytes((2, tq, seq), F32), 3 * _nbytes((2, tq, seq), BF16)],
    )
    vec = pl.BlockSpec((1, DA_HEAD_DIM), lambda b, h: (0, 0))
    kernel = functools.partial(_diff_attn_kernel, n_side=len(side), tq=tq, lam_init=lam_init)
    outs = pl.pallas_call(
        kernel,
        grid=grid,
        in_specs=[
            pl.BlockSpec(memory_space=pltpu.SMEM),
            vec, vec, vec, vec,
            pl.BlockSpec((1, hb), lambda b, h: (0, 0)),
            pl.BlockSpec((seq, hb), lambda b, h: (b, q_blk0 + h)),
            pl.BlockSpec((seq, hb), lambda b, h: (b, k_blk0 + h)),
            pl.BlockSpec((seq, hb), lambda b, h: (b, v_blk0 + h)),
        ] + side.in_specs,
        out_specs=[pl.BlockSpec((seq, hb), lambda b, h: (b, h))] + side.out_specs,
        out_shape=[jax.ShapeDtypeStruct((batch * seq, DA_HEADS * hb), BF16)] + side.out_shapes,
        scratch_shapes=scratch,
        compiler_params=pltpu.CompilerParams(
            dimension_semantics=("arbitrary", "arbitrary"), vmem_limit_bytes=limit
        ),
        name="diff_attention",
    )(slopes, lq1.reshape(1, -1), lk1.reshape(1, -1), lq2.reshape(1, -1), lk2.reshape(1, -1),
      g.reshape(1, hb), proj, proj, proj, *side_weights)
    return tuple(outs)


def _retention_kernel(*refs, n_side):
    (lg_ref, q_ref, k_ref, v_ref, gate_ref), o_ref, _, side = _split_refs(refs, 5, n_side)
    _cast_blocks(side)
    s_len = q_ref.shape[0]
    c = RET_CHUNK
    dk, dv = RET_QK_DIM, RET_V_DIM
    scale = dk ** -0.5
    lg = lg_ref[pl.program_id(1)]

    row = lax.broadcasted_iota(jnp.int32, (c, c), 0)
    col = lax.broadcasted_iota(jnp.int32, (c, c), 1)
    diff = (row - col).astype(F32)
    intra = jnp.where(diff >= 0, jnp.exp(lg * jnp.maximum(diff, 0.0)), 0.0) * scale
    row_k = lax.broadcasted_iota(jnp.int32, (c, dk), 0).astype(F32)
    k_decay = jnp.exp(lg * (float(c - 1) - row_k)) * scale
    row_v = lax.broadcasted_iota(jnp.int32, (c, dv), 0).astype(F32)
    q_decay = jnp.exp(lg * (row_v + 1.0))
    chunk_decay = jnp.exp(jnp.full((1, dv), lg * float(c), F32))

    state = jnp.zeros((dk, dv), F32)
    for i in range(s_len // c):
        lo, hi = i * c, (i + 1) * c
        q = q_ref[lo:hi, :]
        k = k_ref[lo:hi, :]
        v = v_ref[lo:hi, :]
        scores = _dot_nt(q, k) * intra
        y = _dot(scores.astype(BF16), v)
        if i > 0:
            y = y + _dot(q, state.astype(BF16)) * q_decay
        if i + 1 < s_len // c:
            kd = (k.astype(F32) * k_decay).astype(BF16)
            state = state * chunk_decay + _dot_tn(kd, v)
        gate = gate_ref[lo:hi, :].astype(F32)
        o_ref[lo:hi, :] = (_rms(y) * (gate * jax.nn.sigmoid(gate))).astype(o_ref.dtype)


def _retention(proj, log_gammas, *, batch, seq, side_weights=()):
    da_cols = 3 * DA_HEADS * DA_V_DIM
    q_blk0 = da_cols // RET_QK_DIM
    k_blk0 = q_blk0 + RET_HEADS
    v_blk0 = (da_cols + 2 * RET_HEADS * RET_QK_DIM) // RET_V_DIM
    g_blk0 = v_blk0 + RET_HEADS
    grid = (batch, RET_HEADS)
    side = _SideCasts(side_weights, grid)
    limit = _vmem_limit(
        [_nbytes((seq, RET_QK_DIM), proj.dtype)] * 2 + [_nbytes((seq, RET_V_DIM), proj.dtype)] * 3
        + side.window_bytes,
        [16 * _nbytes((RET_CHUNK, RET_V_DIM), F32) * (seq // RET_CHUNK)],
    )
    outs = pl.pallas_call(
        functools.partial(_retention_kernel, n_side=len(side)),
        grid=grid,
        in_specs=[
            pl.BlockSpec(memory_space=pltpu.SMEM),
            pl.BlockSpec((seq, RET_QK_DIM), lambda b, h: (b, q_blk0 + h)),
            pl.BlockSpec((seq, RET_QK_DIM), lambda b, h: (b, k_blk0 + h)),
            pl.BlockSpec((seq, RET_V_DIM), lambda b, h: (b, v_blk0 + h)),
            pl.BlockSpec((seq, RET_V_DIM), lambda b, h: (b, g_blk0 + h)),
        ] + side.in_specs,
        out_specs=[pl.BlockSpec((seq, RET_V_DIM), lambda b, h: (b, h))] + side.out_specs,
        out_shape=[jax.ShapeDtypeStruct((batch * seq, RET_HEADS * RET_V_DIM), BF16)] + side.out_shapes,
        compiler_params=pltpu.CompilerParams(
            dimension_semantics=("arbitrary", "arbitrary"), vmem_limit_bytes=limit
        ),
        name="retention",
    )(log_gammas, proj, proj, proj, proj, *side_weights)
    return tuple(outs)


def _xattn_kernel(*refs, n_side):
    ins, o_ref, (xo_ref,), side = _split_refs(refs, 5, n_side)
    xq_ref, xk_ref, xv_ref, res_ref, wo_ref = ins
    _cast_blocks(side)
    d_model = xq_ref.shape[1]
    hd = d_model // XATTN_HEADS
    scale2 = hd ** -0.5 * LOG2_E
    for h in range(XATTN_HEADS):
        cols = slice(h * hd, (h + 1) * hd)
        s = _dot_nt(xq_ref[:, cols], xk_ref[:, cols]) * scale2
        p = jnp.exp2(s - jnp.max(s, axis=-1, keepdims=True))
        p = p * (1.0 / jnp.sum(p, axis=-1, keepdims=True))
        xo_ref[:, cols] = _dot(p.astype(BF16), xv_ref[:, cols]).astype(xo_ref.dtype)
    o_ref[...] = res_ref[...] + _dot(xo_ref[...], wo_ref[...])


def _cross_attention(xq, xk, xv, res, wo, *, batch, seq, mem_len, tq, side_weights=()):
    d = xq.shape[1]
    nq = seq // tq
    grid = (batch, nq)
    side = _SideCasts(side_weights, grid)
    limit = _vmem_limit(
        [
            _nbytes((tq, d), xq.dtype),
            _nbytes((mem_len, d), xk.dtype),
            _nbytes((mem_len, d), xv.dtype),
            _nbytes((tq, d), F32),
            _nbytes((d, d), wo.dtype),
            _nbytes((tq, d), F32),
        ] + side.window_bytes,
        [_nbytes((tq, d), BF16), _nbytes((tq, d), F32)],
    )
    outs = pl.pallas_call(
        functools.partial(_xattn_kernel, n_side=len(side)),
        grid=grid,
        in_specs=[
            pl.BlockSpec((tq, d), lambda b, i: (b * nq + i, 0)),
            pl.BlockSpec((mem_len, d), lambda b, i: (b, 0)),
            pl.BlockSpec((mem_len, d), lambda b, i: (b, 0)),
            pl.BlockSpec((tq, d), lambda b, i: (b * nq + i, 0)),
            pl.BlockSpec((d, d), lambda b, i: (0, 0)),
        ] + side.in_specs,
        out_specs=[pl.BlockSpec((tq, d), lambda b, i: (b * nq + i, 0))] + side.out_specs,
        out_shape=[jax.ShapeDtypeStruct((batch * seq, d), F32)] + side.out_shapes,
        scratch_shapes=[pltpu.VMEM((tq, d), BF16)],
        compiler_params=pltpu.CompilerParams(
            dimension_semantics=("arbitrary", "arbitrary"), vmem_limit_bytes=limit
        ),
        name="cross_attention",
    )(xq, xk, xv, res, wo, *side_weights)
    return tuple(outs)


def _ffn_kernel(x_ref, g_ref, wg_ref, wu_ref, wd_ref, gf_ref, o_ref, h_ref, *, final_norm):
    f = pl.program_id(1)
    last = pl.num_programs(1) - 1
    blocks = _row_blocks(x_ref.shape[0])

    def partial_ffn(h):
        gate = _dot(h, wg_ref[...])
        up = _dot(h, wu_ref[...])
        act = (gate * jax.nn.sigmoid(gate)) * up
        return _dot(act.astype(BF16), wd_ref[...])

    @pl.when(f == 0)
    def _():
        for rows in blocks:
            x = x_ref[rows, :]
            h = (_rms(x) * g_ref[...]).astype(h_ref.dtype)
            h_ref[rows, :] = h
            o_ref[rows, :] = x + partial_ffn(h)

    if final_norm:
        @pl.when(jnp.logical_and(f > 0, f < last))
        def _():
            o_ref[...] += partial_ffn(h_ref[...])

        @pl.when(f == last)
        def _():
            for rows in blocks:
                y = o_ref[rows, :] + partial_ffn(h_ref[rows, :])
                o_ref[rows, :] = _rms(y) * gf_ref[...]
    else:
        @pl.when(f > 0)
        def _():
            o_ref[...] += partial_ffn(h_ref[...])


def _ffn(x, g, wg, wu, wd, gf, *, final_norm, tm, tf):
    m, d = x.shape
    d_ff = wg.shape[1]
    limit = _vmem_limit(
        [
            _nbytes((tm, d), F32),
            _nbytes((d, tf), wg.dtype),
            _nbytes((d, tf), wu.dtype),
            _nbytes((tf, d), wd.dtype),
            _nbytes((tm, d), F32),
        ],
        [_nbytes((tm, d), BF16), _nbytes((tm, d), F32), 4 * _nbytes((tm, tf), F32)],
    )
    return pl.pallas_call(
        functools.partial(_ffn_kernel, final_norm=final_norm),
        grid=(m // tm, d_ff // tf),
        in_specs=[
            pl.BlockSpec((tm, d), lambda i, f: (i, 0)),
            pl.BlockSpec((1, d), lambda i, f: (0, 0)),
            pl.BlockSpec((d, tf), lambda i, f: (0, f)),
            pl.BlockSpec((d, tf), lambda i, f: (0, f)),
            pl.BlockSpec((tf, d), lambda i, f: (f, 0)),
            pl.BlockSpec((1, d), lambda i, f: (0, 0)),
        ],
        out_specs=pl.BlockSpec((tm, d), lambda i, f: (i, 0)),
        out_shape=jax.ShapeDtypeStruct((m, d), F32),
        scratch_shapes=[pltpu.VMEM((tm, d), BF16)],
        compiler_params=pltpu.CompilerParams(
            dimension_semantics=("parallel", "arbitrary"), vmem_limit_bytes=limit
        ),
        name="swiglu_ffn",
    )(x, g.reshape(1, d), wg, wu, wd, gf.reshape(1, d))


def kernel(x, mem, norm_mix_g, w_in, lambda_q1, lambda_k1, lambda_q2, lambda_k2, da_subln_g, w_o, norm_x_g, norm_mem_g, w_xq, w_xk, w_xv, w_xo, norm_ffn_g, w_gate, w_up, w_down, norm_f_g):
    batch, seq, d_model = x.shape
    mem_len = mem.shape[1]
    depth = w_in.shape[0]
    slopes = jnp.asarray(2.0 ** (-8.0 * np.arange(1, DA_HEADS + 1) / DA_HEADS), dtype=F32)
    log_gammas = jnp.asarray(np.log(1.0 - 2.0 ** (-5.0 - np.arange(RET_HEADS))), dtype=F32)

    xf = x.reshape(batch * seq, d_model)
    memf = mem.reshape(batch * mem_len, d_model)
    for l in range(depth):
        lam_init = 0.8 - 0.6 * math.exp(-0.3 * l)
        n_dq = DA_HEADS * 2 * DA_HEAD_DIM
        col_scale = jnp.where(jnp.arange(w_in.shape[2]) < n_dq, DA_HEAD_DIM ** -0.5 * LOG2_E, 1.0)
        proj, wb_o, wb_xq = _norm_matmul(
            xf, norm_mix_g[l], w_in[l], BF16, tm=1024, tn=1024, group_tiles=2, name="in_proj",
            col_scale=col_scale.astype(F32), side_weights=(w_o[l], w_xq[l]))
        da, wb_gate, wb_up, wb_xk, wb_xv, wb_xo = _diff_attention(
            proj, slopes, lambda_q1[l], lambda_k1[l], lambda_q2[l], lambda_k2[l], da_subln_g[l],
            batch=batch, seq=seq, lam_init=lam_init, tq=256,
            side_weights=(w_gate[l], w_up[l], w_xk[l], w_xv[l], w_xo[l]))
        ret, = _retention(proj, log_gammas, batch=batch, seq=seq)
        xf = _matmul2_res(da, ret, wb_o, xf, tm=512, tn=d_model, name="out_proj")

        xq, = _norm_matmul(xf, norm_x_g[l], wb_xq, BF16, tm=1024, tn=1024, name="xattn_q")
        xk, = _norm_matmul(memf, norm_mem_g[l], wb_xk, BF16, tm=1024, tn=1024, name="xattn_k")
        xv, = _norm_matmul(memf, norm_mem_g[l], wb_xv, BF16, tm=1024, tn=1024, name="xattn_v")
        xf, wb_down = _cross_attention(xq, xk, xv, xf, wb_xo, batch=batch, seq=seq, mem_len=mem_len,
                                       tq=512, side_weights=(w_down[l],))

        xf = _ffn(xf, norm_ffn_g[l], wb_gate, wb_up, wb_down, norm_f_g,
                  final_norm=(l == depth - 1), tm=1024, tf=256)
    return xf.reshape(batch, seq, d_model)
```

```python
import functools
import math

import jax
import jax.numpy as jnp
import numpy as np
from jax import lax
from jax.experimental import pallas as pl
from jax.experimental.pallas import tpu as pltpu

F32 = jnp.float32
BF16 = jnp.bfloat16

DA_HEADS = 4
DA_HEAD_DIM = 128
DA_V_DIM = 2 * DA_HEAD_DIM
RET_HEADS = 4
RET_QK_DIM = 128
RET_V_DIM = 256
XATTN_HEADS = 4
RET_CHUNK = 256
NORM_EPS = 1e-6
NEG_INF = -1e30
LOG2_E = math.log2(math.e)
NORM_BLOCK_ROWS = 256

V7X_LANES = 128
MXU_COLS = 256
BF16_TILE_ROWS = 16
V7X_VMEM_BYTES = 64 * 1024 * 1024
V7X_VMEM_USABLE_BYTES = V7X_VMEM_BYTES - 8 * 1024 * 1024
COMPILER_SCRATCH_BYTES = 4 * 1024 * 1024


def _nbytes(shape, dtype):
    return int(np.prod(shape)) * jnp.dtype(dtype).itemsize


def _vmem_limit(pipelined, resident):
    need = 2 * sum(pipelined) + sum(resident) + COMPILER_SCRATCH_BYTES
    return int(min(V7X_VMEM_USABLE_BYTES, need))


def _rms(x):
    return x * lax.rsqrt(jnp.mean(x * x, axis=-1, keepdims=True) + NORM_EPS)


def _fold_lanes(x, op):
    tiles = [x[:, i:i + V7X_LANES] for i in range(0, x.shape[1], V7X_LANES)]
    return functools.reduce(op, tiles)


def _dot(a, b):
    return jnp.dot(a, b, preferred_element_type=F32)


def _dot_nt(a, b):
    return lax.dot_general(a, b, (((1,), (1,)), ((), ())), preferred_element_type=F32)


def _dot_tn(a, b):
    return lax.dot_general(a, b, (((0,), (0,)), ((), ())), preferred_element_type=F32)


class _SideCasts:
    def __init__(self, weights, grid):
        self.weights = list(weights)
        self.grid = tuple(grid)
        n_steps = int(np.prod(self.grid))
        self.plans = []
        for w in self.weights:
            rows, n_blocks = w.shape[0], n_steps
            while rows % n_blocks or (rows // n_blocks) % BF16_TILE_ROWS:
                n_blocks -= 1
            self.plans.append((n_blocks, rows // n_blocks))

    def __len__(self):
        return len(self.weights)

    def _specs(self):
        specs = []
        for w, (n_blocks, block_rows) in zip(self.weights, self.plans):
            def index(*ids, n_blocks=n_blocks):
                step = ids[0]
                for extent, idx in zip(self.grid[1:], ids[1:]):
                    step = step * extent + idx
                return (jnp.minimum(step, n_blocks - 1), 0)
            specs.append(pl.BlockSpec((block_rows, w.shape[1]), index))
        return specs

    in_specs = property(_specs)
    out_specs = property(_specs)

    @property
    def out_shapes(self):
        return [jax.ShapeDtypeStruct(w.shape, BF16) for w in self.weights]

    @property
    def window_bytes(self):
        return [_nbytes((rows, w.shape[1]), dt)
                for w, (_, rows) in zip(self.weights, self.plans) for dt in (w.dtype, BF16)]


def _split_refs(refs, n_in, n_side):
    ins, rest = refs[:n_in], refs[n_in:]
    side_in, rest = rest[:n_side], rest[n_side:]
    out, side_out, scratch = rest[0], rest[1:1 + n_side], rest[1 + n_side:]
    return ins, out, scratch, list(zip(side_in, side_out))


def _cast_blocks(pairs):
    for src, dst in pairs:
        dst[...] = src[...].astype(dst.dtype)


def _row_blocks(n_rows):
    step = min(NORM_BLOCK_ROWS, n_rows)
    return [slice(r, r + step) for r in range(0, n_rows, step)]


def _col_chunks(n_cols):
    step = min(MXU_COLS, n_cols)
    return [slice(c, c + step) for c in range(0, n_cols, step)]


def _norm_matmul_kernel(*refs, n_side, cast_w):
    ins, o_ref, scratch, side = _split_refs(refs, 4 if cast_w else 3, n_side)
    x_ref, g_ref, w_ref = ins[:3]
    h_ref = scratch[0]
    wb_ref = scratch[1] if cast_w else w_ref
    n, t = pl.program_id(1), pl.program_id(2)

    def cast_weight(cols):
        wb_ref[:, cols] = (w_ref[:, cols] * ins[3][:, cols]).astype(wb_ref.dtype)

    @pl.when(n == 0)
    def _():
        _cast_blocks(side)
        if cast_w:
            @pl.when(t == 0)
            def _():
                cast_weight(slice(None))
        for rows in _row_blocks(x_ref.shape[0]):
            h = (_rms(x_ref[rows, :]) * g_ref[...]).astype(h_ref.dtype)
            h_ref[t, rows, :] = h
            o_ref[rows, :] = _dot(h, wb_ref[...]).astype(o_ref.dtype)

    @pl.when(n > 0)
    def _():
        _cast_blocks(side)
        if cast_w:
            @pl.when(t == 0)
            def _():
                for cols in _col_chunks(w_ref.shape[1]):
                    cast_weight(cols)
                    o_ref[:, cols] = _dot(h_ref[t], wb_ref[:, cols]).astype(o_ref.dtype)

            @pl.when(t > 0)
            def _():
                o_ref[...] = _dot(h_ref[t], wb_ref[...]).astype(o_ref.dtype)
        else:
            o_ref[...] = _dot(h_ref[t], w_ref[...]).astype(o_ref.dtype)


def _norm_matmul(x, g, w, out_dtype, *, tm, tn, name, group_tiles=1, col_scale=None,
                 side_weights=()):
    m, d = x.shape
    n = w.shape[1]
    tm, tn = min(tm, m), min(tn, n)
    gt = group_tiles
    grid = (m // (tm * gt), n // tn, gt)
    side = _SideCasts(side_weights, grid)
    cast_w = w.dtype != BF16
    assert cast_w or col_scale is None
    weight_ins, weight_specs = [w], [pl.BlockSpec((d, tn), lambda gi, j, t: (0, j))]
    if cast_w:
        cs = jnp.ones((n,), F32) if col_scale is None else col_scale
        weight_ins.append(cs.reshape(1, n))
        weight_specs.append(pl.BlockSpec((1, tn), lambda gi, j, t: (0, j)))
    scratch = [pltpu.VMEM((gt, tm, d), BF16)] + ([pltpu.VMEM((d, tn), BF16)] if cast_w else [])
    limit = _vmem_limit(
        [_nbytes((tm, d), x.dtype), _nbytes((d, tn), w.dtype), _nbytes((tm, tn), out_dtype)]
        + side.window_bytes,
        [_nbytes((gt, tm, d), BF16), _nbytes((d, tn), BF16) * cast_w],
    )
    outs = pl.pallas_call(
        functools.partial(_norm_matmul_kernel, n_side=len(side), cast_w=cast_w),
        grid=grid,
        in_specs=[
            pl.BlockSpec((tm, d), lambda gi, j, t: (gi * gt + jnp.where(j == 0, t, gt - 1), 0)),
            pl.BlockSpec((1, d), lambda gi, j, t: (0, 0)),
        ] + weight_specs + side.in_specs,
        out_specs=[pl.BlockSpec((tm, tn), lambda gi, j, t: (gi * gt + t, j))] + side.out_specs,
        out_shape=[jax.ShapeDtypeStruct((m, n), out_dtype)] + side.out_shapes,
        scratch_shapes=scratch,
        compiler_params=pltpu.CompilerParams(
            dimension_semantics=("arbitrary", "arbitrary", "arbitrary"), vmem_limit_bytes=limit
        ),
        name=name,
    )(x, g.reshape(1, d), *weight_ins, *side_weights)
    return tuple(outs)


def _matmul2_res_kernel(a1_ref, a2_ref, w1_ref, w2_ref, res_ref, o_ref):
    acc = _dot(a1_ref[...], w1_ref[...]) + _dot(a2_ref[...], w2_ref[...])
    o_ref[...] = res_ref[...] + acc


def _matmul2_res(a1, a2, w, res, *, tm, tn, name):
    m, k1 = a1.shape
    k2 = a2.shape[1]
    assert k1 == k2 and w.shape[0] == k1 + k2
    n = w.shape[1]
    limit = _vmem_limit(
        [
            _nbytes((tm, k1), a1.dtype),
            _nbytes((tm, k2), a2.dtype),
            _nbytes((k1, tn), w.dtype),
            _nbytes((k2, tn), w.dtype),
            _nbytes((tm, tn), F32),
            _nbytes((tm, tn), F32),
        ],
        [_nbytes((tm, tn), F32)],
    )
    return pl.pallas_call(
        _matmul2_res_kernel,
        grid=(m // tm, n // tn),
        in_specs=[
            pl.BlockSpec((tm, k1), lambda i, j: (i, 0)),
            pl.BlockSpec((tm, k2), lambda i, j: (i, 0)),
            pl.BlockSpec((k1, tn), lambda i, j: (0, j)),
            pl.BlockSpec((k2, tn), lambda i, j: (1, j)),
            pl.BlockSpec((tm, tn), lambda i, j: (i, j)),
        ],
        out_specs=pl.BlockSpec((tm, tn), lambda i, j: (i, j)),
        out_shape=jax.ShapeDtypeStruct((m, n), F32),
        compiler_params=pltpu.CompilerParams(
            dimension_semantics=("parallel", "arbitrary"), vmem_limit_bytes=limit
        ),
        name=name,
    )(a1, a2, w, w, res)


def _bf16_part(x):
    bits = lax.bitcast_convert_type(x, jnp.int32) & jnp.int32(-65536)
    return lax.bitcast_convert_type(bits, F32)


def _diff_attn_kernel(*refs, n_side, tq, lam_init):
    ins, o_ref, (kx_ref, s_ref, p_ref), side = _split_refs(refs, 9, n_side)
    slope_ref, lq1_ref, lk1_ref, lq2_ref, lk2_ref, g_ref, q_ref, k_ref, v_ref = ins
    _cast_blocks(side)
    s_len = q_ref.shape[0]
    d = DA_HEAD_DIM
    lam = (jnp.exp(jnp.sum(lq1_ref[...] * lk1_ref[...], axis=-1, keepdims=True))
           - jnp.exp(jnp.sum(lq2_ref[...] * lk2_ref[...], axis=-1, keepdims=True))
           + lam_init)

    lane = lax.broadcasted_iota(jnp.int32, (1, d), 1)
    slope2 = jnp.full((1, d), slope_ref[pl.program_id(1)] * LOG2_E, F32)
    piece_hi = _bf16_part(slope2)
    rest = slope2 - piece_hi
    piece_mid = _bf16_part(rest)
    piece_lo = _bf16_part(rest - piece_mid)
    piece = jnp.where((lane == 0) | (lane == 3), piece_hi,
                      jnp.where((lane == 1) | (lane == 4), piece_mid, piece_lo))
    q_extra = jnp.where(lane < 3, piece * 256.0, jnp.where(lane < 6, piece, 0.0))
    q_extra = jnp.broadcast_to(q_extra, (tq, d)).astype(BF16)
    kpos = lax.broadcasted_iota(jnp.int32, (s_len, d), 0)
    klane = lax.broadcasted_iota(jnp.int32, (s_len, d), 1)
    k_extra = jnp.where(klane < 3, kpos >> 8, jnp.where(klane < 6, kpos & 255, 0))
    kx_ref[...] = k_extra.astype(F32).astype(kx_ref.dtype)

    row = lax.broadcasted_iota(jnp.int32, (tq, tq), 0)
    col = lax.broadcasted_iota(jnp.int32, (tq, tq), 1)
    causal = col <= row

    for qi in reversed(range(s_len // tq)):
        lo, hi = qi * tq, (qi + 1) * tq
        key_blocks = [slice(j * tq, (j + 1) * tq) for j in range(qi + 1)]
        heads = []
        for c in range(2):
            dcols = slice(c * d, (c + 1) * d)
            q_aug = jnp.concatenate([q_ref[lo:hi, dcols], q_extra], axis=1)
            m = None
            for j, cols in enumerate(key_blocks):
                k_aug = jnp.concatenate([k_ref[cols, dcols], kx_ref[cols, :]], axis=1)
                s = _dot_nt(q_aug, k_aug)
                if j == qi:
                    s = jnp.where(causal, s, NEG_INF)
                s_ref[c, :, cols] = s
                bm = _fold_lanes(s, jnp.maximum)
                m = bm if m is None else jnp.maximum(m, bm)
            m = jnp.max(m, axis=-1, keepdims=True)
            l = None
            for cols in key_blocks:
                p = jnp.exp2(s_ref[c, :, cols] - m)
                bl = _fold_lanes(p, jnp.add)
                l = bl if l is None else l + bl
                p_ref[c, :, cols] = p.astype(p_ref.dtype)
            l = jnp.sum(l, axis=-1, keepdims=True)
            acc = _dot(p_ref[c, :, 0:hi], v_ref[0:hi, :])
            heads.append((acc, l))
        (acc1, l1), (acc2, l2) = heads
        out = acc1 * (1.0 / l1) - acc2 * (lam / l2)
        y = _rms(out) * g_ref[...] * (1.0 - lam_init)
        o_ref[lo:hi, :] = y.astype(o_ref.dtype)


def _diff_attention(proj, slopes, lq1, lk1, lq2, lk2, g, *, batch, seq, lam_init, tq,
                    side_weights=()):
    hb = DA_V_DIM
    q_blk0, k_blk0, v_blk0 = 0, DA_HEADS, 2 * DA_HEADS
    grid = (batch, DA_HEADS)
    side = _SideCasts(side_weights, grid)
    seq_bytes = _nbytes((seq, hb), proj.dtype)
    scratch = [
        pltpu.VMEM((seq, DA_HEAD_DIM), BF16),
        pltpu.VMEM((2, tq, seq), F32),
        pltpu.VMEM((2, tq, seq), BF16),
    ]
    limit = _vmem_limit(
        [seq_bytes] * 4 + side.window_bytes,
        [_nbytes((seq, DA_HEAD_DIM), BF16),
         3 * _nbytes((2, tq, seq), F32), 3 * _nbytes((2, tq, seq), BF16)],
    )
    vec = pl.BlockSpec((1, DA_HEAD_DIM), lambda b, h: (0, 0))
    kernel = functools.partial(_diff_attn_kernel, n_side=len(side), tq=tq, lam_init=lam_init)
    outs = pl.pallas_call(
        kernel,
        grid=grid,
        in_specs=[
            pl.BlockSpec(memory_space=pltpu.SMEM),
            vec, vec, vec, vec,
            pl.BlockSpec((1, hb), lambda b, h: (0, 0)),
            pl.BlockSpec((seq, hb), lambda b, h: (b, q_blk0 + h)),
            pl.BlockSpec((seq, hb), lambda b, h: (b, k_blk0 + h)),
            pl.BlockSpec((seq, hb), lambda b, h: (b, v_blk0 + h)),
        ] + side.in_specs,
        out_specs=[pl.BlockSpec((seq, hb), lambda b, h: (b, h))] + side.out_specs,
        out_shape=[jax.ShapeDtypeStruct((batch * seq, DA_HEADS * hb), BF16)] + side.out_shapes,
        scratch_shapes=scratch,
        compiler_params=pltpu.CompilerParams(
            dimension_semantics=("arbitrary", "arbitrary"), vmem_limit_bytes=limit
        ),
        name="diff_attention",
    )(slopes, lq1.reshape(1, -1), lk1.reshape(1, -1), lq2.reshape(1, -1), lk2.reshape(1, -1),
      g.reshape(1, hb), proj, proj, proj, *side_weights)
    return tuple(outs)


def _retention_kernel(*refs, n_side):
    (lg_ref, q_ref, k_ref, v_ref, gate_ref), o_ref, _, side = _split_refs(refs, 5, n_side)
    _cast_blocks(side)
    s_len = q_ref.shape[0]
    c = RET_CHUNK
    dk, dv = RET_QK_DIM, RET_V_DIM
    scale = dk ** -0.5
    lg = lg_ref[pl.program_id(1)]

    row = lax.broadcasted_iota(jnp.int32, (c, c), 0)
    col = lax.broadcasted_iota(jnp.int32, (c, c), 1)
    diff = (row - col).astype(F32)
    intra = jnp.where(diff >= 0, jnp.exp(lg * jnp.maximum(diff, 0.0)), 0.0) * scale
    row_k = lax.broadcasted_iota(jnp.int32, (c, dk), 0).astype(F32)
    k_decay = jnp.exp(lg * (float(c - 1) - row_k)) * scale
    row_v = lax.broadcasted_iota(jnp.int32, (c, dv), 0).astype(F32)
    q_decay = jnp.exp(lg * (row_v + 1.0))
    chunk_decay = jnp.exp(jnp.full((1, dv), lg * float(c), F32))

    state = jnp.zeros((dk, dv), F32)
    for i in range(s_len // c):
        lo, hi = i * c, (i + 1) * c
        q = q_ref[lo:hi, :]
        k = k_ref[lo:hi, :]
        v = v_ref[lo:hi, :]
        scores = _dot_nt(q, k) * intra
        y = _dot(scores.astype(BF16), v)
        if i > 0:
            y = y + _dot(q, state.astype(BF16)) * q_decay
        if i + 1 < s_len // c:
            kd = (k.astype(F32) * k_decay).astype(BF16)
            state = state * chunk_decay + _dot_tn(kd, v)
        gate = gate_ref[lo:hi, :].astype(F32)
        o_ref[lo:hi, :] = (_rms(y) * (gate * jax.nn.sigmoid(gate))).astype(o_ref.dtype)


def _retention(proj, log_gammas, *, batch, seq, side_weights=()):
    da_cols = 3 * DA_HEADS * DA_V_DIM
    q_blk0 = da_cols // RET_QK_DIM
    k_blk0 = q_blk0 + RET_HEADS
    v_blk0 = (da_cols + 2 * RET_HEADS * RET_QK_DIM) // RET_V_DIM
    g_blk0 = v_blk0 + RET_HEADS
    grid = (batch, RET_HEADS)
    side = _SideCasts(side_weights, grid)
    limit = _vmem_limit(
        [_nbytes((seq, RET_QK_DIM), proj.dtype)] * 2 + [_nbytes((seq, RET_V_DIM), proj.dtype)] * 3
        + side.window_bytes,
        [16 * _nbytes((RET_CHUNK, RET_V_DIM), F32) * (seq // RET_CHUNK)],
    )
    outs = pl.pallas_call(
        functools.partial(_retention_kernel, n_side=len(side)),
        grid=grid,
        in_specs=[
            pl.BlockSpec(memory_space=pltpu.SMEM),
            pl.BlockSpec((seq, RET_QK_DIM), lambda b, h: (b, q_blk0 + h)),
            pl.BlockSpec((seq, RET_QK_DIM), lambda b, h: (b, k_blk0 + h)),
            pl.BlockSpec((seq, RET_V_DIM), lambda b, h: (b, v_blk0 + h)),
            pl.BlockSpec((seq, RET_V_DIM), lambda b, h: (b, g_blk0 + h)),
        ] + side.in_specs,
        out_specs=[pl.BlockSpec((seq, RET_V_DIM), lambda b, h: (b, h))] + side.out_specs,
        out_shape=[jax.ShapeDtypeStruct((batch * seq, RET_HEADS * RET_V_DIM), BF16)] + side.out_shapes,
        compiler_params=pltpu.CompilerParams(
            dimension_semantics=("arbitrary", "arbitrary"), vmem_limit_bytes=limit
        ),
        name="retention",
    )(log_gammas, proj, proj, proj, proj, *side_weights)
    return tuple(outs)


def _xattn_kernel(*refs, n_side):
    ins, o_ref, (xo_ref,), side = _split_refs(refs, 5, n_side)
    xq_ref, xk_ref, xv_ref, res_ref, wo_ref = ins
    _cast_blocks(side)
    d_model = xq_ref.shape[1]
    hd = d_model // XATTN_HEADS
    scale2 = hd ** -0.5 * LOG2_E
    for h in range(XATTN_HEADS):
        cols = slice(h * hd, (h + 1) * hd)
        s = _dot_nt(xq_ref[:, cols], xk_ref[:, cols]) * scale2
        p = jnp.exp2(s - jnp.max(s, axis=-1, keepdims=True))
        p = p * (1.0 / jnp.sum(p, axis=-1, keepdims=True))
        xo_ref[:, cols] = _dot(p.astype(BF16), xv_ref[:, cols]).astype(xo_ref.dtype)
    o_ref[...] = res_ref[...] + _dot(xo_ref[...], wo_ref[...])


def _cross_attention(xq, xk, xv, res, wo, *, batch, seq, mem_len, tq, side_weights=()):
    d = xq.shape[1]
    nq = seq // tq
    grid = (batch, nq)
    side = _SideCasts(side_weights, grid)
    limit = _vmem_limit(
        [
            _nbytes((tq, d), xq.dtype),
            _nbytes((mem_len, d), xk.dtype),
            _nbytes((mem_len, d), xv.dtype),
            _nbytes((tq, d), F32),
            _nbytes((d, d), wo.dtype),
            _nbytes((tq, d), F32),
        ] + side.window_bytes,
        [_nbytes((tq, d), BF16), _nbytes((tq, d), F32)],
    )
    outs = pl.pallas_call(
        functools.partial(_xattn_kernel, n_side=len(side)),
        grid=grid,
        in_specs=[
            pl.BlockSpec((tq, d), lambda b, i: (b * nq + i, 0)),
            pl.BlockSpec((mem_len, d), lambda b, i: (b, 0)),
            pl.BlockSpec((mem_len, d), lambda b, i: (b, 0)),
            pl.BlockSpec((tq, d), lambda b, i: (b * nq + i, 0)),
            pl.BlockSpec((d, d), lambda b, i: (0, 0)),
        ] + side.in_specs,
        out_specs=[pl.BlockSpec((tq, d), lambda b, i: (b * nq + i, 0))] + side.out_specs,
        out_shape=[jax.ShapeDtypeStruct((batch * seq, d), F32)] + side.out_shapes,
        scratch_shapes=[pltpu.VMEM((tq, d), BF16)],
        compiler_params=pltpu.CompilerParams(
            dimension_semantics=("arbitrary", "arbitrary"), vmem_limit_bytes=limit
        ),
        name="cross_attention",
    )(xq, xk, xv, res, wo, *side_weights)
    return tuple(outs)


def _ffn_kernel(x_ref, g_ref, wg_ref, wu_ref, wd_ref, gf_ref, o_ref, h_ref, *, final_norm):
    f = pl.program_id(1)
    last = pl.num_programs(1) - 1
    blocks = _row_blocks(x_ref.shape[0])

    def partial_ffn(h):
        gate = _dot(h, wg_ref[...])
        up = _dot(h, wu_ref[...])
        act = (gate * jax.nn.sigmoid(gate)) * up
        return _dot(act.astype(BF16), wd_ref[...])

    @pl.when(f == 0)
    def _():
        for rows in blocks:
            x = x_ref[rows, :]
            h = (_rms(x) * g_ref[...]).astype(h_ref.dtype)
            h_ref[rows, :] = h
            o_ref[rows, :] = x + partial_ffn(h)

    if final_norm:
        @pl.when(jnp.logical_and(f > 0, f < last))
        def _():
            o_ref[...] += partial_ffn(h_ref[...])

        @pl.when(f == last)
        def _():
            for rows in blocks:
                y = o_ref[rows, :] + partial_ffn(h_ref[rows, :])
                o_ref[rows, :] = _rms(y) * gf_ref[...]
    else:
        @pl.when(f > 0)
        def _():
            o_ref[...] += partial_ffn(h_ref[...])


def _ffn(x, g, wg, wu, wd, gf, *, final_norm, tm, tf):
    m, d = x.shape
    d_ff = wg.shape[1]
    limit = _vmem_limit(
        [
            _nbytes((tm, d), F32),
            _nbytes((d, tf), wg.dtype),
            _nbytes((d, tf), wu.dtype),
            _nbytes((tf, d), wd.dtype),
            _nbytes((tm, d), F32),
        ],
        [_nbytes((tm, d), BF16), _nbytes((tm, d), F32), 4 * _nbytes((tm, tf), F32)],
    )
    return pl.pallas_call(
        functools.partial(_ffn_kernel, final_norm=final_norm),
        grid=(m // tm, d_ff // tf),
        in_specs=[
            pl.BlockSpec((tm, d), lambda i, f: (i, 0)),
            pl.BlockSpec((1, d), lambda i, f: (0, 0)),
            pl.BlockSpec((d, tf), lambda i, f: (0, f)),
            pl.BlockSpec((d, tf), lambda i, f: (0, f)),
            pl.BlockSpec((tf, d), lambda i, f: (f, 0)),
            pl.BlockSpec((1, d), lambda i, f: (0, 0)),
        ],
        out_specs=pl.BlockSpec((tm, d), lambda i, f: (i, 0)),
        out_shape=jax.ShapeDtypeStruct((m, d), F32),
        scratch_shapes=[pltpu.VMEM((tm, d), BF16)],
        compiler_params=pltpu.CompilerParams(
            dimension_semantics=("parallel", "arbitrary"), vmem_limit_bytes=limit
        ),
        name="swiglu_ffn",
    )(x, g.reshape(1, d), wg, wu, wd, gf.reshape(1, d))


def kernel(x, mem, norm_mix_g, w_in, lambda_q1, lambda_k1, lambda_q2, lambda_k2, da_subln_g, w_o, norm_x_g, norm_mem_g, w_xq, w_xk, w_xv, w_xo, norm_ffn_g, w_gate, w_up, w_down, norm_f_g):
    batch, seq, d_model = x.shape
    mem_len = mem.shape[1]
    depth = w_in.shape[0]
    slopes = jnp.asarray(2.0 ** (-8.0 * np.arange(1, DA_HEADS + 1) / DA_HEADS), dtype=F32)
    log_gammas = jnp.asarray(np.log(1.0 - 2.0 ** (-5.0 - np.arange(RET_HEADS))), dtype=F32)

    xf = x.reshape(batch * seq, d_model)
    memf = mem.reshape(batch * mem_len, d_model)
    for l in range(depth):
        lam_init = 0.8 - 0.6 * math.exp(-0.3 * l)
        n_dq = DA_HEADS * 2 * DA_HEAD_DIM
        col_scale = jnp.where(jnp.arange(w_in.shape[2]) < n_dq, DA_HEAD_DIM ** -0.5 * LOG2_E, 1.0)
        proj, wb_o, wb_xq = _norm_matmul(
            xf, norm_mix_g[l], w_in[l], BF16, tm=1024, tn=1024, group_tiles=2, name="in_proj",
            col_scale=col_scale.astype(F32), side_weights=(w_o[l], w_xq[l]))
        da, wb_gate, wb_up, wb_xk, wb_xv, wb_xo = _diff_attention(
            proj, slopes, lambda_q1[l], lambda_k1[l], lambda_q2[l], lambda_k2[l], da_subln_g[l],
            batch=batch, seq=seq, lam_init=lam_init, tq=256,
            side_weights=(w_gate[l], w_up[l], w_xk[l], w_xv[l], w_xo[l]))
        ret, = _retention(proj, log_gammas, batch=batch, seq=seq)
        xf = _matmul2_res(da, ret, wb_o, xf, tm=512, tn=d_model, name="out_proj")

        xq, = _norm_matmul(xf, norm_x_g[l], wb_xq, BF16, tm=1024, tn=1024, name="xattn_q")
        xk, = _norm_matmul(memf, norm_mem_g[l], wb_xk, BF16, tm=1024, tn=1024, name="xattn_k")
        xv, = _norm_matmul(memf, norm_mem_g[l], wb_xv, BF16, tm=1024, tn=1024, name="xattn_v")
        xf, wb_down = _cross_attention(xq, xk, xv, xf, wb_xo, batch=batch, seq=seq, mem_len=mem_len,
                                       tq=512, side_weights=(w_down[l],))

        xf = _ffn(xf, norm_ffn_g[l], wb_gate, wb_up, wb_down, norm_f_g,
                  final_norm=(l == depth - 1), tm=1024, tf=512)
    return xf.reshape(batch, seq, d_model)
```

```python
import functools
import math

import jax
import jax.numpy as jnp
import numpy as np
from jax import lax
from jax.experimental import pallas as pl
from jax.experimental.pallas import tpu as pltpu

F32 = jnp.float32
BF16 = jnp.bfloat16

DA_HEADS = 4
DA_HEAD_DIM = 128
DA_V_DIM = 2 * DA_HEAD_DIM
RET_HEADS = 4
RET_QK_DIM = 128
RET_V_DIM = 256
XATTN_HEADS = 4
RET_CHUNK = 256
NORM_EPS = 1e-6
NEG_INF = -1e30
LOG2_E = math.log2(math.e)
NORM_BLOCK_ROWS = 256

V7X_LANES = 128
MXU_COLS = 256
BF16_TILE_ROWS = 16
V7X_VMEM_BYTES = 64 * 1024 * 1024
V7X_VMEM_USABLE_BYTES = V7X_VMEM_BYTES - 8 * 1024 * 1024
COMPILER_SCRATCH_BYTES = 4 * 1024 * 1024


def _nbytes(shape, dtype):
    return int(np.prod(shape)) * jnp.dtype(dtype).itemsize


def _vmem_limit(pipelined, resident):
    need = 2 * sum(pipelined) + sum(resident) + COMPILER_SCRATCH_BYTES
    return int(min(V7X_VMEM_USABLE_BYTES, need))


def _rms(x):
    return x * lax.rsqrt(jnp.mean(x * x, axis=-1, keepdims=True) + NORM_EPS)


def _fold_lanes(x, op):
    tiles = [x[:, i:i + V7X_LANES] for i in range(0, x.shape[1], V7X_LANES)]
    return functools.reduce(op, tiles)


def _dot(a, b):
    return jnp.dot(a, b, preferred_element_type=F32)


def _dot_nt(a, b):
    return lax.dot_general(a, b, (((1,), (1,)), ((), ())), preferred_element_type=F32)


def _dot_tn(a, b):
    return lax.dot_general(a, b, (((0,), (0,)), ((), ())), preferred_element_type=F32)


class _SideCasts:
    def __init__(self, weights, grid):
        self.weights = list(weights)
        self.grid = tuple(grid)
        n_steps = int(np.prod(self.grid))
        self.plans = []
        for w in self.weights:
            rows, n_blocks = w.shape[0], n_steps
            while rows % n_blocks or (rows // n_blocks) % BF16_TILE_ROWS:
                n_blocks -= 1
            self.plans.append((n_blocks, rows // n_blocks))

    def __len__(self):
        return len(self.weights)

    def _specs(self):
        specs = []
        for w, (n_blocks, block_rows) in zip(self.weights, self.plans):
            def index(*ids, n_blocks=n_blocks):
                step = ids[0]
                for extent, idx in zip(self.grid[1:], ids[1:]):
                    step = step * extent + idx
                return (jnp.minimum(step, n_blocks - 1), 0)
            specs.append(pl.BlockSpec((block_rows, w.shape[1]), index))
        return specs

    in_specs = property(_specs)
    out_specs = property(_specs)

    @property
    def out_shapes(self):
        return [jax.ShapeDtypeStruct(w.shape, BF16) for w in self.weights]

    @property
    def window_bytes(self):
        return [_nbytes((rows, w.shape[1]), dt)
                for w, (_, rows) in zip(self.weights, self.plans) for dt in (w.dtype, BF16)]


def _split_refs(refs, n_in, n_side):
    ins, rest = refs[:n_in], refs[n_in:]
    side_in, rest = rest[:n_side], rest[n_side:]
    out, side_out, scratch = rest[0], rest[1:1 + n_side], rest[1 + n_side:]
    return ins, out, scratch, list(zip(side_in, side_out))


def _cast_blocks(pairs):
    for src, dst in pairs:
        dst[...] = src[...].astype(dst.dtype)


def _row_blocks(n_rows):
    step = min(NORM_BLOCK_ROWS, n_rows)
    return [slice(r, r + step) for r in range(0, n_rows, step)]


def _col_chunks(n_cols):
    step = min(MXU_COLS, n_cols)
    return [slice(c, c + step) for c in range(0, n_cols, step)]


def _norm_matmul_kernel(*refs, n_side, cast_w):
    ins, o_ref, scratch, side = _split_refs(refs, 4 if cast_w else 3, n_side)
    x_ref, g_ref, w_ref = ins[:3]
    h_ref = scratch[0]
    wb_ref = scratch[1] if cast_w else w_ref
    n, t = pl.program_id(1), pl.program_id(2)

    def cast_weight(cols):
        wb_ref[:, cols] = (w_ref[:, cols] * ins[3][:, cols]).astype(wb_ref.dtype)

    @pl.when(n == 0)
    def _():
        _cast_blocks(side)
        if cast_w:
            @pl.when(t == 0)
            def _():
                cast_weight(slice(None))
        for rows in _row_blocks(x_ref.shape[0]):
            h = (_rms(x_ref[rows, :]) * g_ref[...]).astype(h_ref.dtype)
            h_ref[t, rows, :] = h
            o_ref[rows, :] = _dot(h, wb_ref[...]).astype(o_ref.dtype)

    @pl.when(n > 0)
    def _():
        _cast_blocks(side)
        if cast_w:
            @pl.when(t == 0)
            def _():
                for cols in _col_chunks(w_ref.shape[1]):
                    cast_weight(cols)
                    o_ref[:, cols] = _dot(h_ref[t], wb_ref[:, cols]).astype(o_ref.dtype)

            @pl.when(t > 0)
            def _():
                o_ref[...] = _dot(h_ref[t], wb_ref[...]).astype(o_ref.dtype)
        else:
            o_ref[...] = _dot(h_ref[t], w_ref[...]).astype(o_ref.dtype)


def _norm_matmul(x, g, w, out_dtype, *, tm, tn, name, group_tiles=1, col_scale=None,
                 side_weights=()):
    m, d = x.shape
    n = w.shape[1]
    tm, tn = min(tm, m), min(tn, n)
    gt = group_tiles
    grid = (m // (tm * gt), n // tn, gt)
    side = _SideCasts(side_weights, grid)
    cast_w = w.dtype != BF16
    assert cast_w or col_scale is None
    weight_ins, weight_specs = [w], [pl.BlockSpec((d, tn), lambda gi, j, t: (0, j))]
    if cast_w:
        cs = jnp.ones((n,), F32) if col_scale is None else col_scale
        weight_ins.append(cs.reshape(1, n))
        weight_specs.append(pl.BlockSpec((1, tn), lambda gi, j, t: (0, j)))
    scratch = [pltpu.VMEM((gt, tm, d), BF16)] + ([pltpu.VMEM((d, tn), BF16)] if cast_w else [])
    limit = _vmem_limit(
        [_nbytes((tm, d), x.dtype), _nbytes((d, tn), w.dtype), _nbytes((tm, tn), out_dtype)]
        + side.window_bytes,
        [_nbytes((gt, tm, d), BF16), _nbytes((d, tn), BF16) * cast_w],
    )
    outs = pl.pallas_call(
        functools.partial(_norm_matmul_kernel, n_side=len(side), cast_w=cast_w),
        grid=grid,
        in_specs=[
            pl.BlockSpec((tm, d), lambda gi, j, t: (gi * gt + jnp.where(j == 0, t, gt - 1), 0)),
            pl.BlockSpec((1, d), lambda gi, j, t: (0, 0)),
        ] + weight_specs + side.in_specs,
        out_specs=[pl.BlockSpec((tm, tn), lambda gi, j, t: (gi * gt + t, j))] + side.out_specs,
        out_shape=[jax.ShapeDtypeStruct((m, n), out_dtype)] + side.out_shapes,
        scratch_shapes=scratch,
        compiler_params=pltpu.CompilerParams(
            dimension_semantics=("arbitrary", "arbitrary", "arbitrary"), vmem_limit_bytes=limit
        ),
        name=name,
    )(x, g.reshape(1, d), *weight_ins, *side_weights)
    return tuple(outs)


def _norm_matmul_pair_kernel(x_ref, g_ref, w1_ref, w2_ref, o1_ref, o2_ref, h_ref):
    @pl.when(pl.program_id(1) == 0)
    def _():
        h_ref[...] = (_rms(x_ref[...]) * g_ref[...]).astype(h_ref.dtype)

    o1_ref[...] = _dot(h_ref[...], w1_ref[...]).astype(o1_ref.dtype)
    o2_ref[...] = _dot(h_ref[...], w2_ref[...]).astype(o2_ref.dtype)


def _norm_matmul_pair(x, g, w1, w2, out_dtype, *, tm, tn, name):
    m, d = x.shape
    n = w1.shape[1]
    assert w1.shape == w2.shape
    tm, tn = min(tm, m), min(tn, n)
    w_spec = pl.BlockSpec((d, tn), lambda i, j: (0, j))
    o_spec = pl.BlockSpec((tm, tn), lambda i, j: (i, j))
    limit = _vmem_limit(
        [_nbytes((tm, d), x.dtype)] + [_nbytes((d, tn), w1.dtype), _nbytes((tm, tn), out_dtype)] * 2,
        [_nbytes((tm, d), BF16), _nbytes((tm, d), F32)],
    )
    return pl.pallas_call(
        _norm_matmul_pair_kernel,
        grid=(m // tm, n // tn),
        in_specs=[pl.BlockSpec((tm, d), lambda i, j: (i, 0)), pl.BlockSpec((1, d), lambda i, j: (0, 0)),
                  w_spec, w_spec],
        out_specs=[o_spec, o_spec],
        out_shape=[jax.ShapeDtypeStruct((m, n), out_dtype)] * 2,
        scratch_shapes=[pltpu.VMEM((tm, d), BF16)],
        compiler_params=pltpu.CompilerParams(
            dimension_semantics=("arbitrary", "arbitrary"), vmem_limit_bytes=limit
        ),
        name=name,
    )(x, g.reshape(1, d), w1, w2)


def _matmul2_res_kernel(a1_ref, a2_ref, w1_ref, w2_ref, res_ref, o_ref):
    acc = _dot(a1_ref[...], w1_ref[...]) + _dot(a2_ref[...], w2_ref[...])
    o_ref[...] = res_ref[...] + acc


def _matmul2_res(a1, a2, w, res, *, tm, tn, name):
    m, k1 = a1.shape
    k2 = a2.shape[1]
    assert k1 == k2 and w.shape[0] == k1 + k2
    n = w.shape[1]
    limit = _vmem_limit(
        [
            _nbytes((tm, k1), a1.dtype),
            _nbytes((tm, k2), a2.dtype),
            _nbytes((k1, tn), w.dtype),
            _nbytes((k2, tn), w.dtype),
            _nbytes((tm, tn), F32),
            _nbytes((tm, tn), F32),
        ],
        [_nbytes((tm, tn), F32)],
    )
    return pl.pallas_call(
        _matmul2_res_kernel,
        grid=(m // tm, n // tn),
        in_specs=[
            pl.BlockSpec((tm, k1), lambda i, j: (i, 0)),
            pl.BlockSpec((tm, k2), lambda i, j: (i, 0)),
            pl.BlockSpec((k1, tn), lambda i, j: (0, j)),
            pl.BlockSpec((k2, tn), lambda i, j: (1, j)),
            pl.BlockSpec((tm, tn), lambda i, j: (i, j)),
        ],
        out_specs=pl.BlockSpec((tm, tn), lambda i, j: (i, j)),
        out_shape=jax.ShapeDtypeStruct((m, n), F32),
        compiler_params=pltpu.CompilerParams(
            dimension_semantics=("parallel", "arbitrary"), vmem_limit_bytes=limit
        ),
        name=name,
    )(a1, a2, w, w, res)


def _bf16_part(x):
    bits = lax.bitcast_convert_type(x, jnp.int32) & jnp.int32(-65536)
    return lax.bitcast_convert_type(bits, F32)


def _diff_attn_kernel(*refs, n_side, tq, lam_init):
    ins, o_ref, (kx_ref, s_ref, p_ref), side = _split_refs(refs, 9, n_side)
    slope_ref, lq1_ref, lk1_ref, lq2_ref, lk2_ref, g_ref, q_ref, k_ref, v_ref = ins
    _cast_blocks(side)
    s_len = q_ref.shape[0]
    d = DA_HEAD_DIM
    lam = (jnp.exp(jnp.sum(lq1_ref[...] * lk1_ref[...], axis=-1, keepdims=True))
           - jnp.exp(jnp.sum(lq2_ref[...] * lk2_ref[...], axis=-1, keepdims=True))
           + lam_init)

    lane = lax.broadcasted_iota(jnp.int32, (1, d), 1)
    slope2 = jnp.full((1, d), slope_ref[pl.program_id(1)] * LOG2_E, F32)
    piece_hi = _bf16_part(slope2)
    rest = slope2 - piece_hi
    piece_mid = _bf16_part(rest)
    piece_lo = _bf16_part(rest - piece_mid)
    piece = jnp.where((lane == 0) | (lane == 3), piece_hi,
                      jnp.where((lane == 1) | (lane == 4), piece_mid, piece_lo))
    q_extra = jnp.where(lane < 3, piece * 256.0, jnp.where(lane < 6, piece, 0.0))
    q_extra = jnp.broadcast_to(q_extra, (tq, d)).astype(BF16)
    kpos = lax.broadcasted_iota(jnp.int32, (s_len, d), 0)
    klane = lax.broadcasted_iota(jnp.int32, (s_len, d), 1)
    k_extra = jnp.where(klane < 3, kpos >> 8, jnp.where(klane < 6, kpos & 255, 0))
    kx_ref[...] = k_extra.astype(F32).astype(kx_ref.dtype)

    row = lax.broadcasted_iota(jnp.int32, (tq, tq), 0)
    col = lax.broadcasted_iota(jnp.int32, (tq, tq), 1)
    causal = col <= row

    for qi in reversed(range(s_len // tq)):
        lo, hi = qi * tq, (qi + 1) * tq
        key_blocks = [slice(j * tq, (j + 1) * tq) for j in range(qi + 1)]
        heads = []
        for c in range(2):
            dcols = slice(c * d, (c + 1) * d)
            q_aug = jnp.concatenate([q_ref[lo:hi, dcols], q_extra], axis=1)
            m = None
            for j, cols in enumerate(key_blocks):
                k_aug = jnp.concatenate([k_ref[cols, dcols], kx_ref[cols, :]], axis=1)
                s = _dot_nt(q_aug, k_aug)
                if j == qi:
                    s = jnp.where(causal, s, NEG_INF)
                s_ref[c, :, cols] = s
                bm = _fold_lanes(s, jnp.maximum)
                m = bm if m is None else jnp.maximum(m, bm)
            m = jnp.max(m, axis=-1, keepdims=True)
            l = None
            for cols in key_blocks:
                p = jnp.exp2(s_ref[c, :, cols] - m)
                bl = _fold_lanes(p, jnp.add)
                l = bl if l is None else l + bl
                p_ref[c, :, cols] = p.astype(p_ref.dtype)
            l = jnp.sum(l, axis=-1, keepdims=True)
            acc = _dot(p_ref[c, :, 0:hi], v_ref[0:hi, :])
            heads.append((acc, l))
        (acc1, l1), (acc2, l2) = heads
        out = acc1 * (1.0 / l1) - acc2 * (lam / l2)
        y = _rms(out) * g_ref[...] * (1.0 - lam_init)
        o_ref[lo:hi, :] = y.astype(o_ref.dtype)


def _diff_attention(proj, slopes, lq1, lk1, lq2, lk2, g, *, batch, seq, lam_init, tq,
                    side_weights=()):
    hb = DA_V_DIM
    q_blk0, k_blk0, v_blk0 = 0, DA_HEADS, 2 * DA_HEADS
    grid = (batch, DA_HEADS)
    side = _SideCasts(side_weights, grid)
    seq_bytes = _nbytes((seq, hb), proj.dtype)
    scratch = [
        pltpu.VMEM((seq, DA_HEAD_DIM), BF16),
        pltpu.VMEM((2, tq, seq), F32),
        pltpu.VMEM((2, tq, seq), BF16),
    ]
    limit = _vmem_limit(
        [seq_bytes] * 4 + side.window_bytes,
        [_nbytes((seq, DA_HEAD_DIM), BF16),
         3 * _nbytes((2, tq, seq), F32), 3 * _nbytes((2, tq, seq), BF16)],
    )
    vec = pl.BlockSpec((1, DA_HEAD_DIM), lambda b, h: (0, 0))
    kernel = functools.partial(_diff_attn_kernel, n_side=len(side), tq=tq, lam_init=lam_init)
    outs = pl.pallas_call(
        kernel,
        grid=grid,
        in_specs=[
            pl.BlockSpec(memory_space=pltpu.SMEM),
            vec, vec, vec, vec,
            pl.BlockSpec((1, hb), lambda b, h: (0, 0)),
            pl.BlockSpec((seq, hb), lambda b, h: (b, q_blk0 + h)),
            pl.BlockSpec((seq, hb), lambda b, h: (b, k_blk0 + h)),
            pl.BlockSpec((seq, hb), lambda b, h: (b, v_blk0 + h)),
        ] + side.in_specs,
        out_specs=[pl.BlockSpec((seq, hb), lambda b, h: (b, h))] + side.out_specs,
        out_shape=[jax.ShapeDtypeStruct((batch * seq, DA_HEADS * hb), BF16)] + side.out_shapes,
        scratch_shapes=scratch,
        compiler_params=pltpu.CompilerParams(
            dimension_semantics=("arbitrary", "arbitrary"), vmem_limit_bytes=limit
        ),
        name="diff_attention",
    )(slopes, lq1.reshape(1, -1), lk1.reshape(1, -1), lq2.reshape(1, -1), lk2.reshape(1, -1),
      g.reshape(1, hb), proj, proj, proj, *side_weights)
    return tuple(outs)


def _retention_kernel(*refs, n_side):
    (lg_ref, q_ref, k_ref, v_ref, gate_ref), o_ref, _, side = _split_refs(refs, 5, n_side)
    _cast_blocks(side)
    s_len = q_ref.shape[0]
    c = RET_CHUNK
    dk, dv = RET_QK_DIM, RET_V_DIM
    scale = dk ** -0.5
    lg = lg_ref[pl.program_id(1)]

    row = lax.broadcasted_iota(jnp.int32, (c, c), 0)
    col = lax.broadcasted_iota(jnp.int32, (c, c), 1)
    diff = (row - col).astype(F32)
    intra = jnp.where(diff >= 0, jnp.exp(lg * jnp.maximum(diff, 0.0)), 0.0) * scale
    row_k = lax.broadcasted_iota(jnp.int32, (c, dk), 0).astype(F32)
    k_decay = jnp.exp(lg * (float(c - 1) - row_k)) * scale
    row_v = lax.broadcasted_iota(jnp.int32, (c, dv), 0).astype(F32)
    q_decay = jnp.exp(lg * (row_v + 1.0))
    chunk_decay = jnp.exp(jnp.full((1, dv), lg * float(c), F32))

    state = jnp.zeros((dk, dv), F32)
    for i in range(s_len // c):
        lo, hi = i * c, (i + 1) * c
        q = q_ref[lo:hi, :]
        k = k_ref[lo:hi, :]
        v = v_ref[lo:hi, :]
        scores = _dot_nt(q, k) * intra
        y = _dot(scores.astype(BF16), v)
        if i > 0:
            y = y + _dot(q, state.astype(BF16)) * q_decay
        if i + 1 < s_len // c:
            kd = (k.astype(F32) * k_decay).astype(BF16)
            state = state * chunk_decay + _dot_tn(kd, v)
        gate = gate_ref[lo:hi, :].astype(F32)
        o_ref[lo:hi, :] = (_rms(y) * (gate * jax.nn.sigmoid(gate))).astype(o_ref.dtype)


def _retention(proj, log_gammas, *, batch, seq, side_weights=()):
    da_cols = 3 * DA_HEADS * DA_V_DIM
    q_blk0 = da_cols // RET_QK_DIM
    k_blk0 = q_blk0 + RET_HEADS
    v_blk0 = (da_cols + 2 * RET_HEADS * RET_QK_DIM) // RET_V_DIM
    g_blk0 = v_blk0 + RET_HEADS
    grid = (batch, RET_HEADS)
    side = _SideCasts(side_weights, grid)
    limit = _vmem_limit(
        [_nbytes((seq, RET_QK_DIM), proj.dtype)] * 2 + [_nbytes((seq, RET_V_DIM), proj.dtype)] * 3
        + side.window_bytes,
        [16 * _nbytes((RET_CHUNK, RET_V_DIM), F32) * (seq // RET_CHUNK)],
    )
    outs = pl.pallas_call(
        functools.partial(_retention_kernel, n_side=len(side)),
        grid=grid,
        in_specs=[
            pl.BlockSpec(memory_space=pltpu.SMEM),
            pl.BlockSpec((seq, RET_QK_DIM), lambda b, h: (b, q_blk0 + h)),
            pl.BlockSpec((seq, RET_QK_DIM), lambda b, h: (b, k_blk0 + h)),
            pl.BlockSpec((seq, RET_V_DIM), lambda b, h: (b, v_blk0 + h)),
            pl.BlockSpec((seq, RET_V_DIM), lambda b, h: (b, g_blk0 + h)),
        ] + side.in_specs,
        out_specs=[pl.BlockSpec((seq, RET_V_DIM), lambda b, h: (b, h))] + side.out_specs,
        out_shape=[jax.ShapeDtypeStruct((batch * seq, RET_HEADS * RET_V_DIM), BF16)] + side.out_shapes,
        compiler_params=pltpu.CompilerParams(
            dimension_semantics=("arbitrary", "arbitrary"), vmem_limit_bytes=limit
        ),
        name="retention",
    )(log_gammas, proj, proj, proj, proj, *side_weights)
    return tuple(outs)


def _xattn_kernel(*refs, n_side):
    ins, o_ref, (xo_ref,), side = _split_refs(refs, 5, n_side)
    xq_ref, xk_ref, xv_ref, res_ref, wo_ref = ins
    _cast_blocks(side)
    d_model = xq_ref.shape[1]
    hd = d_model // XATTN_HEADS
    scale2 = hd ** -0.5 * LOG2_E
    for h in range(XATTN_HEADS):
        cols = slice(h * hd, (h + 1) * hd)
        s = _dot_nt(xq_ref[:, cols], xk_ref[:, cols]) * scale2
        p = jnp.exp2(s - jnp.max(s, axis=-1, keepdims=True))
        p = p * (1.0 / jnp.sum(p, axis=-1, keepdims=True))
        xo_ref[:, cols] = _dot(p.astype(BF16), xv_ref[:, cols]).astype(xo_ref.dtype)
    o_ref[...] = res_ref[...] + _dot(xo_ref[...], wo_ref[...])


def _cross_attention(xq, xk, xv, res, wo, *, batch, seq, mem_len, tq, side_weights=()):
    d = xq.shape[1]
    nq = seq // tq
    grid = (batch, nq)
    side = _SideCasts(side_weights, grid)
    limit = _vmem_limit(
        [
            _nbytes((tq, d), xq.dtype),
            _nbytes((mem_len, d), xk.dtype),
            _nbytes((mem_len, d), xv.dtype),
            _nbytes((tq, d), F32),
            _nbytes((d, d), wo.dtype),
            _nbytes((tq, d), F32),
        ] + side.window_bytes,
        [_nbytes((tq, d), BF16), _nbytes((tq, d), F32)],
    )
    outs = pl.pallas_call(
        functools.partial(_xattn_kernel, n_side=len(side)),
        grid=grid,
        in_specs=[
            pl.BlockSpec((tq, d), lambda b, i: (b * nq + i, 0)),
            pl.BlockSpec((mem_len, d), lambda b, i: (b, 0)),
            pl.BlockSpec((mem_len, d), lambda b, i: (b, 0)),
            pl.BlockSpec((tq, d), lambda b, i: (b * nq + i, 0)),
            pl.BlockSpec((d, d), lambda b, i: (0, 0)),
        ] + side.in_specs,
        out_specs=[pl.BlockSpec((tq, d), lambda b, i: (b * nq + i, 0))] + side.out_specs,
        out_shape=[jax.ShapeDtypeStruct((batch * seq, d), F32)] + side.out_shapes,
        scratch_shapes=[pltpu.VMEM((tq, d), BF16)],
        compiler_params=pltpu.CompilerParams(
            dimension_semantics=("arbitrary", "arbitrary"), vmem_limit_bytes=limit
        ),
        name="cross_attention",
    )(xq, xk, xv, res, wo, *side_weights)
    return tuple(outs)


def _ffn_kernel(x_ref, g_ref, wg_ref, wu_ref, wd_ref, gf_ref, o_ref, h_ref, *, final_norm):
    f = pl.program_id(1)
    last = pl.num_programs(1) - 1
    blocks = _row_blocks(x_ref.shape[0])

    def partial_ffn(h):
        gate = _dot(h, wg_ref[...])
        up = _dot(h, wu_ref[...])
        act = (gate * jax.nn.sigmoid(gate)) * up
        return _dot(act.astype(BF16), wd_ref[...])

    @pl.when(f == 0)
    def _():
        for rows in blocks:
            x = x_ref[rows, :]
            h = (_rms(x) * g_ref[...]).astype(h_ref.dtype)
            h_ref[rows, :] = h
            o_ref[rows, :] = x + partial_ffn(h)

    if final_norm:
        @pl.when(jnp.logical_and(f > 0, f < last))
        def _():
            o_ref[...] += partial_ffn(h_ref[...])

        @pl.when(f == last)
        def _():
            for rows in blocks:
                y = o_ref[rows, :] + partial_ffn(h_ref[rows, :])
                o_ref[rows, :] = _rms(y) * gf_ref[...]
    else:
        @pl.when(f > 0)
        def _():
            o_ref[...] += partial_ffn(h_ref[...])


def _ffn(x, g, wg, wu, wd, gf, *, final_norm, tm, tf):
    m, d = x.shape
    d_ff = wg.shape[1]
    limit = _vmem_limit(
        [
            _nbytes((tm, d), F32),
            _nbytes((d, tf), wg.dtype),
            _nbytes((d, tf), wu.dtype),
            _nbytes((tf, d), wd.dtype),
            _nbytes((tm, d), F32),
        ],
        [_nbytes((tm, d), BF16), _nbytes((tm, d), F32), 4 * _nbytes((tm, tf), F32)],
    )
    return pl.pallas_call(
        functools.partial(_ffn_kernel, final_norm=final_norm),
        grid=(m // tm, d_ff // tf),
        in_specs=[
            pl.BlockSpec((tm, d), lambda i, f: (i, 0)),
            pl.BlockSpec((1, d), lambda i, f: (0, 0)),
            pl.BlockSpec((d, tf), lambda i, f: (0, f)),
            pl.BlockSpec((d, tf), lambda i, f: (0, f)),
            pl.BlockSpec((tf, d), lambda i, f: (f, 0)),
            pl.BlockSpec((1, d), lambda i, f: (0, 0)),
        ],
        out_specs=pl.BlockSpec((tm, d), lambda i, f: (i, 0)),
        out_shape=jax.ShapeDtypeStruct((m, d), F32),
        scratch_shapes=[pltpu.VMEM((tm, d), BF16)],
        compiler_params=pltpu.CompilerParams(
            dimension_semantics=("parallel", "arbitrary"), vmem_limit_bytes=limit
        ),
        name="swiglu_ffn",
    )(x, g.reshape(1, d), wg, wu, wd, gf.reshape(1, d))


def kernel(x, mem, norm_mix_g, w_in, lambda_q1, lambda_k1, lambda_q2, lambda_k2, da_subln_g, w_o, norm_x_g, norm_mem_g, w_xq, w_xk, w_xv, w_xo, norm_ffn_g, w_gate, w_up, w_down, norm_f_g):
    batch, seq, d_model = x.shape
    mem_len = mem.shape[1]
    depth = w_in.shape[0]
    slopes = jnp.asarray(2.0 ** (-8.0 * np.arange(1, DA_HEADS + 1) / DA_HEADS), dtype=F32)
    log_gammas = jnp.asarray(np.log(1.0 - 2.0 ** (-5.0 - np.arange(RET_HEADS))), dtype=F32)

    xf = x.reshape(batch * seq, d_model)
    memf = mem.reshape(batch * mem_len, d_model)
    for l in range(depth):
        lam_init = 0.8 - 0.6 * math.exp(-0.3 * l)
        n_dq = DA_HEADS * 2 * DA_HEAD_DIM
        col_scale = jnp.where(jnp.arange(w_in.shape[2]) < n_dq, DA_HEAD_DIM ** -0.5 * LOG2_E, 1.0)
        proj, wb_o, wb_xq = _norm_matmul(
            xf, norm_mix_g[l], w_in[l], BF16, tm=1024, tn=1024, group_tiles=2, name="in_proj",
            col_scale=col_scale.astype(F32), side_weights=(w_o[l], w_xq[l]))
        da, wb_gate, wb_up, wb_xk, wb_xv, wb_xo = _diff_attention(
            proj, slopes, lambda_q1[l], lambda_k1[l], lambda_q2[l], lambda_k2[l], da_subln_g[l],
            batch=batch, seq=seq, lam_init=lam_init, tq=256,
            side_weights=(w_gate[l], w_up[l], w_xk[l], w_xv[l], w_xo[l]))
        ret, = _retention(proj, log_gammas, batch=batch, seq=seq)
        xf = _matmul2_res(da, ret, wb_o, xf, tm=512, tn=d_model, name="out_proj")

        xq, = _norm_matmul(xf, norm_x_g[l], wb_xq, BF16, tm=1024, tn=d_model, name="xattn_q")
        xk, xv = _norm_matmul_pair(memf, norm_mem_g[l], wb_xk, wb_xv, BF16, tm=1024, tn=1024,
                                   name="xattn_kv")
        xf, wb_down = _cross_attention(xq, xk, xv, xf, wb_xo, batch=batch, seq=seq, mem_len=mem_len,
                                       tq=512, side_weights=(w_down[l],))

        xf = _ffn(xf, norm_ffn_g[l], wb_gate, wb_up, wb_down, norm_f_g,
                  final_norm=(l == depth - 1), tm=1024, tf=512)
    return xf.reshape(batch, seq, d_model)
```

```python
import functools
import math

import jax
import jax.numpy as jnp
import numpy as np
from jax import lax
from jax.experimental import pallas as pl
from jax.experimental.pallas import tpu as pltpu

F32 = jnp.float32
BF16 = jnp.bfloat16

DA_HEADS = 4
DA_HEAD_DIM = 128
DA_V_DIM = 2 * DA_HEAD_DIM
RET_HEADS = 4
RET_QK_DIM = 128
RET_V_DIM = 256
XATTN_HEADS = 4
RET_CHUNK = 256
NORM_EPS = 1e-6
NEG_INF = -1e30
LOG2_E = math.log2(math.e)
NORM_BLOCK_ROWS = 256

V7X_LANES = 128
MXU_COLS = 256
BF16_TILE_ROWS = 16
V7X_VMEM_BYTES = 64 * 1024 * 1024
V7X_VMEM_USABLE_BYTES = V7X_VMEM_BYTES - 8 * 1024 * 1024
COMPILER_SCRATCH_BYTES = 4 * 1024 * 1024


def _nbytes(shape, dtype):
    return int(np.prod(shape)) * jnp.dtype(dtype).itemsize


def _vmem_limit(pipelined, resident):
    need = 2 * sum(pipelined) + sum(resident) + COMPILER_SCRATCH_BYTES
    return int(min(V7X_VMEM_USABLE_BYTES, need))


def _rms(x):
    return x * lax.rsqrt(jnp.mean(x * x, axis=-1, keepdims=True) + NORM_EPS)


def _fold_lanes(x, op):
    tiles = [x[:, i:i + V7X_LANES] for i in range(0, x.shape[1], V7X_LANES)]
    return functools.reduce(op, tiles)


def _dot(a, b):
    return jnp.dot(a, b, preferred_element_type=F32)


def _dot_nt(a, b):
    return lax.dot_general(a, b, (((1,), (1,)), ((), ())), preferred_element_type=F32)


def _dot_tn(a, b):
    return lax.dot_general(a, b, (((0,), (0,)), ((), ())), preferred_element_type=F32)


class _SideCasts:
    def __init__(self, weights, grid):
        self.weights = list(weights)
        self.grid = tuple(grid)
        n_steps = int(np.prod(self.grid))
        self.plans = []
        for w in self.weights:
            rows, n_blocks = w.shape[0], n_steps
            while rows % n_blocks or (rows // n_blocks) % BF16_TILE_ROWS:
                n_blocks -= 1
            self.plans.append((n_blocks, rows // n_blocks))

    def __len__(self):
        return len(self.weights)

    def _specs(self):
        specs = []
        for w, (n_blocks, block_rows) in zip(self.weights, self.plans):
            def index(*ids, n_blocks=n_blocks):
                step = ids[0]
                for extent, idx in zip(self.grid[1:], ids[1:]):
                    step = step * extent + idx
                return (jnp.minimum(step, n_blocks - 1), 0)
            specs.append(pl.BlockSpec((block_rows, w.shape[1]), index))
        return specs

    in_specs = property(_specs)
    out_specs = property(_specs)

    @property
    def out_shapes(self):
        return [jax.ShapeDtypeStruct(w.shape, BF16) for w in self.weights]

    @property
    def window_bytes(self):
        return [_nbytes((rows, w.shape[1]), dt)
                for w, (_, rows) in zip(self.weights, self.plans) for dt in (w.dtype, BF16)]


def _split_refs(refs, n_in, n_side):
    ins, rest = refs[:n_in], refs[n_in:]
    side_in, rest = rest[:n_side], rest[n_side:]
    out, side_out, scratch = rest[0], rest[1:1 + n_side], rest[1 + n_side:]
    return ins, out, scratch, list(zip(side_in, side_out))


def _cast_blocks(pairs):
    for src, dst in pairs:
        dst[...] = src[...].astype(dst.dtype)


def _row_blocks(n_rows):
    step = min(NORM_BLOCK_ROWS, n_rows)
    return [slice(r, r + step) for r in range(0, n_rows, step)]


def _col_chunks(n_cols):
    step = min(MXU_COLS, n_cols)
    return [slice(c, c + step) for c in range(0, n_cols, step)]


def _norm_matmul_kernel(*refs, n_side, cast_w):
    ins, o_ref, scratch, side = _split_refs(refs, 4 if cast_w else 3, n_side)
    x_ref, g_ref, w_ref = ins[:3]
    h_ref = scratch[0]
    wb_ref = scratch[1] if cast_w else w_ref
    n, t = pl.program_id(1), pl.program_id(2)

    def cast_weight(cols):
        wb_ref[:, cols] = (w_ref[:, cols] * ins[3][:, cols]).astype(wb_ref.dtype)

    @pl.when(n == 0)
    def _():
        _cast_blocks(side)
        if cast_w:
            @pl.when(t == 0)
            def _():
                cast_weight(slice(None))
        for rows in _row_blocks(x_ref.shape[0]):
            h = (_rms(x_ref[rows, :]) * g_ref[...]).astype(h_ref.dtype)
            h_ref[t, rows, :] = h
            o_ref[rows, :] = _dot(h, wb_ref[...]).astype(o_ref.dtype)

    @pl.when(n > 0)
    def _():
        _cast_blocks(side)
        if cast_w:
            @pl.when(t == 0)
            def _():
                for cols in _col_chunks(w_ref.shape[1]):
                    cast_weight(cols)
                    o_ref[:, cols] = _dot(h_ref[t], wb_ref[:, cols]).astype(o_ref.dtype)

            @pl.when(t > 0)
            def _():
                o_ref[...] = _dot(h_ref[t], wb_ref[...]).astype(o_ref.dtype)
        else:
            o_ref[...] = _dot(h_ref[t], w_ref[...]).astype(o_ref.dtype)


def _norm_matmul(x, g, w, out_dtype, *, tm, tn, name, group_tiles=1, col_scale=None,
                 side_weights=()):
    m, d = x.shape
    n = w.shape[1]
    tm, tn = min(tm, m), min(tn, n)
    gt = group_tiles
    grid = (m // (tm * gt), n // tn, gt)
    side = _SideCasts(side_weights, grid)
    cast_w = w.dtype != BF16
    assert cast_w or col_scale is None
    weight_ins, weight_specs = [w], [pl.BlockSpec((d, tn), lambda gi, j, t: (0, j))]
    if cast_w:
        cs = jnp.ones((n,), F32) if col_scale is None else col_scale
        weight_ins.append(cs.reshape(1, n))
        weight_specs.append(pl.BlockSpec((1, tn), lambda gi, j, t: (0, j)))
    scratch = [pltpu.VMEM((gt, tm, d), BF16)] + ([pltpu.VMEM((d, tn), BF16)] if cast_w else [])
    limit = _vmem_limit(
        [_nbytes((tm, d), x.dtype), _nbytes((d, tn), w.dtype), _nbytes((tm, tn), out_dtype)]
        + side.window_bytes,
        [_nbytes((gt, tm, d), BF16), _nbytes((d, tn), BF16) * cast_w],
    )
    outs = pl.pallas_call(
        functools.partial(_norm_matmul_kernel, n_side=len(side), cast_w=cast_w),
        grid=grid,
        in_specs=[
            pl.BlockSpec((tm, d), lambda gi, j, t: (gi * gt + jnp.where(j == 0, t, gt - 1), 0)),
            pl.BlockSpec((1, d), lambda gi, j, t: (0, 0)),
        ] + weight_specs + side.in_specs,
        out_specs=[pl.BlockSpec((tm, tn), lambda gi, j, t: (gi * gt + t, j))] + side.out_specs,
        out_shape=[jax.ShapeDtypeStruct((m, n), out_dtype)] + side.out_shapes,
        scratch_shapes=scratch,
        compiler_params=pltpu.CompilerParams(
            dimension_semantics=("arbitrary", "arbitrary", "arbitrary"), vmem_limit_bytes=limit
        ),
        name=name,
    )(x, g.reshape(1, d), *weight_ins, *side_weights)
    return tuple(outs)


def _norm_matmul_pair_kernel(x_ref, g_ref, w1_ref, w2_ref, o1_ref, o2_ref, h_ref):
    j = pl.program_id(1)

    @pl.when(j == 0)
    def _():
        for rows in _row_blocks(x_ref.shape[0]):
            h = (_rms(x_ref[rows, :]) * g_ref[...]).astype(h_ref.dtype)
            h_ref[rows, :] = h
            o1_ref[rows, :] = _dot(h, w1_ref[...]).astype(o1_ref.dtype)
            o2_ref[rows, :] = _dot(h, w2_ref[...]).astype(o2_ref.dtype)

    @pl.when(j > 0)
    def _():
        o1_ref[...] = _dot(h_ref[...], w1_ref[...]).astype(o1_ref.dtype)
        o2_ref[...] = _dot(h_ref[...], w2_ref[...]).astype(o2_ref.dtype)


def _norm_matmul_pair(x, g, w1, w2, out_dtype, *, tm, tn, name):
    m, d = x.shape
    n = w1.shape[1]
    assert w1.shape == w2.shape
    tm, tn = min(tm, m), min(tn, n)
    w_spec = pl.BlockSpec((d, tn), lambda i, j: (0, j))
    o_spec = pl.BlockSpec((tm, tn), lambda i, j: (i, j))
    limit = _vmem_limit(
        [_nbytes((tm, d), x.dtype)] + [_nbytes((d, tn), w1.dtype), _nbytes((tm, tn), out_dtype)] * 2,
        [_nbytes((tm, d), BF16), _nbytes((tm, d), F32)],
    )
    return pl.pallas_call(
        _norm_matmul_pair_kernel,
        grid=(m // tm, n // tn),
        in_specs=[pl.BlockSpec((tm, d), lambda i, j: (i, 0)), pl.BlockSpec((1, d), lambda i, j: (0, 0)),
                  w_spec, w_spec],
        out_specs=[o_spec, o_spec],
        out_shape=[jax.ShapeDtypeStruct((m, n), out_dtype)] * 2,
        scratch_shapes=[pltpu.VMEM((tm, d), BF16)],
        compiler_params=pltpu.CompilerParams(
            dimension_semantics=("arbitrary", "arbitrary"), vmem_limit_bytes=limit
        ),
        name=name,
    )(x, g.reshape(1, d), w1, w2)


def _matmul2_res_kernel(a1_ref, a2_ref, w1_ref, w2_ref, res_ref, o_ref):
    acc = _dot(a1_ref[...], w1_ref[...]) + _dot(a2_ref[...], w2_ref[...])
    o_ref[...] = res_ref[...] + acc


def _matmul2_res(a1, a2, w, res, *, tm, tn, name):
    m, k1 = a1.shape
    k2 = a2.shape[1]
    assert k1 == k2 and w.shape[0] == k1 + k2
    n = w.shape[1]
    limit = _vmem_limit(
        [
            _nbytes((tm, k1), a1.dtype),
            _nbytes((tm, k2), a2.dtype),
            _nbytes((k1, tn), w.dtype),
            _nbytes((k2, tn), w.dtype),
            _nbytes((tm, tn), F32),
            _nbytes((tm, tn), F32),
        ],
        [_nbytes((tm, tn), F32)],
    )
    return pl.pallas_call(
        _matmul2_res_kernel,
        grid=(m // tm, n // tn),
        in_specs=[
            pl.BlockSpec((tm, k1), lambda i, j: (i, 0)),
            pl.BlockSpec((tm, k2), lambda i, j: (i, 0)),
            pl.BlockSpec((k1, tn), lambda i, j: (0, j)),
            pl.BlockSpec((k2, tn), lambda i, j: (1, j)),
            pl.BlockSpec((tm, tn), lambda i, j: (i, j)),
        ],
        out_specs=pl.BlockSpec((tm, tn), lambda i, j: (i, j)),
        out_shape=jax.ShapeDtypeStruct((m, n), F32),
        compiler_params=pltpu.CompilerParams(
            dimension_semantics=("parallel", "arbitrary"), vmem_limit_bytes=limit
        ),
        name=name,
    )(a1, a2, w, w, res)


def _bf16_part(x):
    bits = lax.bitcast_convert_type(x, jnp.int32) & jnp.int32(-65536)
    return lax.bitcast_convert_type(bits, F32)


def _diff_attn_body(ins, o_ref, scratch, *, tq, lam_init):
    slope_ref, lq1_ref, lk1_ref, lq2_ref, lk2_ref, g_ref, q_ref, k_ref, v_ref = ins
    kx_ref, s_ref, p_ref = scratch
    s_len = q_ref.shape[0]
    d = DA_HEAD_DIM
    lam = (jnp.exp(jnp.sum(lq1_ref[...] * lk1_ref[...], axis=-1, keepdims=True))
           - jnp.exp(jnp.sum(lq2_ref[...] * lk2_ref[...], axis=-1, keepdims=True))
           + lam_init)

    lane = lax.broadcasted_iota(jnp.int32, (1, d), 1)
    slope2 = jnp.full((1, d), slope_ref[pl.program_id(1)] * LOG2_E, F32)
    piece_hi = _bf16_part(slope2)
    rest = slope2 - piece_hi
    piece_mid = _bf16_part(rest)
    piece_lo = _bf16_part(rest - piece_mid)
    piece = jnp.where((lane == 0) | (lane == 3), piece_hi,
                      jnp.where((lane == 1) | (lane == 4), piece_mid, piece_lo))
    q_extra = jnp.where(lane < 3, piece * 256.0, jnp.where(lane < 6, piece, 0.0))
    q_extra = jnp.broadcast_to(q_extra, (tq, d)).astype(BF16)
    kpos = lax.broadcasted_iota(jnp.int32, (s_len, d), 0)
    klane = lax.broadcasted_iota(jnp.int32, (s_len, d), 1)
    k_extra = jnp.where(klane < 3, kpos >> 8, jnp.where(klane < 6, kpos & 255, 0))
    kx_ref[...] = k_extra.astype(F32).astype(kx_ref.dtype)

    row = lax.broadcasted_iota(jnp.int32, (tq, tq), 0)
    col = lax.broadcasted_iota(jnp.int32, (tq, tq), 1)
    causal = col <= row

    for qi in reversed(range(s_len // tq)):
        lo, hi = qi * tq, (qi + 1) * tq
        key_blocks = [slice(j * tq, (j + 1) * tq) for j in range(qi + 1)]
        heads = []
        for c in range(2):
            dcols = slice(c * d, (c + 1) * d)
            q_aug = jnp.concatenate([q_ref[lo:hi, dcols], q_extra], axis=1)
            m = None
            for j, cols in enumerate(key_blocks):
                k_aug = jnp.concatenate([k_ref[cols, dcols], kx_ref[cols, :]], axis=1)
                s = _dot_nt(q_aug, k_aug)
                if j == qi:
                    s = jnp.where(causal, s, NEG_INF)
                s_ref[c, :, cols] = s
                bm = _fold_lanes(s, jnp.maximum)
                m = bm if m is None else jnp.maximum(m, bm)
            m = jnp.max(m, axis=-1, keepdims=True)
            l = None
            for cols in key_blocks:
                p = jnp.exp2(s_ref[c, :, cols] - m)
                bl = _fold_lanes(p, jnp.add)
                l = bl if l is None else l + bl
                p_ref[c, :, cols] = p.astype(p_ref.dtype)
            l = jnp.sum(l, axis=-1, keepdims=True)
            acc = _dot(p_ref[c, :, 0:hi], v_ref[0:hi, :])
            heads.append((acc, l))
        (acc1, l1), (acc2, l2) = heads
        out = acc1 * (1.0 / l1) - acc2 * (lam / l2)
        y = _rms(out) * g_ref[...] * (1.0 - lam_init)
        o_ref[lo:hi, :] = y.astype(o_ref.dtype)
        yield


def _retention_body(ins, o_ref):
    lg_ref, q_ref, k_ref, v_ref, gate_ref = ins
    s_len = q_ref.shape[0]
    c = RET_CHUNK
    dk, dv = RET_QK_DIM, RET_V_DIM
    scale = dk ** -0.5
    lg = lg_ref[pl.program_id(1)]

    row = lax.broadcasted_iota(jnp.int32, (c, c), 0)
    col = lax.broadcasted_iota(jnp.int32, (c, c), 1)
    diff = (row - col).astype(F32)
    intra = jnp.where(diff >= 0, jnp.exp(lg * jnp.maximum(diff, 0.0)), 0.0) * scale
    row_k = lax.broadcasted_iota(jnp.int32, (c, dk), 0).astype(F32)
    k_decay = jnp.exp(lg * (float(c - 1) - row_k)) * scale
    row_v = lax.broadcasted_iota(jnp.int32, (c, dv), 0).astype(F32)
    q_decay = jnp.exp(lg * (row_v + 1.0))
    chunk_decay = jnp.exp(jnp.full((1, dv), lg * float(c), F32))

    state = jnp.zeros((dk, dv), F32)
    for i in range(s_len // c):
        lo, hi = i * c, (i + 1) * c
        q = q_ref[lo:hi, :]
        k = k_ref[lo:hi, :]
        v = v_ref[lo:hi, :]
        scores = _dot_nt(q, k) * intra
        y = _dot(scores.astype(BF16), v)
        if i > 0:
            y = y + _dot(q, state.astype(BF16)) * q_decay
        if i + 1 < s_len // c:
            kd = (k.astype(F32) * k_decay).astype(BF16)
            state = state * chunk_decay + _dot_tn(kd, v)
        gate = gate_ref[lo:hi, :].astype(F32)
        o_ref[lo:hi, :] = (_rms(y) * (gate * jax.nn.sigmoid(gate))).astype(o_ref.dtype)
        yield


N_ATTN_INS, N_RET_INS = 9, 5


def _mixers_kernel(*refs, n_side, tq, lam_init):
    ins, rest = refs[:N_ATTN_INS + N_RET_INS], refs[N_ATTN_INS + N_RET_INS:]
    side_in, rest = rest[:n_side], rest[n_side:]
    da_ref, ret_ref = rest[:2]
    side_out, scratch = rest[2:2 + n_side], rest[2 + n_side:]
    _cast_blocks(zip(side_in, side_out))
    pending = [_diff_attn_body(ins[:N_ATTN_INS], da_ref, scratch, tq=tq, lam_init=lam_init),
               _retention_body(ins[N_ATTN_INS:], ret_ref)]
    while pending:
        pending = [body for body in pending if next(body, StopIteration) is not StopIteration]


def _mixers(proj, slopes, lq1, lk1, lq2, lk2, g, log_gammas, *, batch, seq, lam_init, tq,
            side_weights=()):
    assert DA_HEADS == RET_HEADS
    hb = DA_V_DIM
    q_blk0, k_blk0, v_blk0 = 0, DA_HEADS, 2 * DA_HEADS
    da_cols = 3 * DA_HEADS * DA_V_DIM
    rq_blk0 = da_cols // RET_QK_DIM
    rk_blk0 = rq_blk0 + RET_HEADS
    rv_blk0 = (da_cols + 2 * RET_HEADS * RET_QK_DIM) // RET_V_DIM
    rg_blk0 = rv_blk0 + RET_HEADS
    grid = (batch, DA_HEADS)
    side = _SideCasts(side_weights, grid)
    scratch = [
        pltpu.VMEM((seq, DA_HEAD_DIM), BF16),
        pltpu.VMEM((2, tq, seq), F32),
        pltpu.VMEM((2, tq, seq), BF16),
    ]
    limit = _vmem_limit(
        [_nbytes((seq, hb), proj.dtype)] * 4
        + [_nbytes((seq, RET_QK_DIM), proj.dtype)] * 2 + [_nbytes((seq, RET_V_DIM), proj.dtype)] * 3
        + side.window_bytes,
        [_nbytes((seq, DA_HEAD_DIM), BF16),
         3 * _nbytes((2, tq, seq), F32), 3 * _nbytes((2, tq, seq), BF16)],
    )
    vec = pl.BlockSpec((1, DA_HEAD_DIM), lambda b, h: (0, 0))
    smem = pl.BlockSpec(memory_space=pltpu.SMEM)

    def head_cols(width, blk0):
        return pl.BlockSpec((seq, width), lambda b, h: (b, blk0 + h))

    outs = pl.pallas_call(
        functools.partial(_mixers_kernel, n_side=len(side), tq=tq, lam_init=lam_init),
        grid=grid,
        in_specs=[
            smem, vec, vec, vec, vec,
            pl.BlockSpec((1, hb), lambda b, h: (0, 0)),
            head_cols(hb, q_blk0), head_cols(hb, k_blk0), head_cols(hb, v_blk0),
            smem,
            head_cols(RET_QK_DIM, rq_blk0), head_cols(RET_QK_DIM, rk_blk0),
            head_cols(RET_V_DIM, rv_blk0), head_cols(RET_V_DIM, rg_blk0),
        ] + side.in_specs,
        out_specs=[head_cols(hb, 0), head_cols(RET_V_DIM, 0)] + side.out_specs,
        out_shape=[jax.ShapeDtypeStruct((batch * seq, DA_HEADS * hb), BF16),
                   jax.ShapeDtypeStruct((batch * seq, RET_HEADS * RET_V_DIM), BF16)] + side.out_shapes,
        scratch_shapes=scratch,
        compiler_params=pltpu.CompilerParams(
            dimension_semantics=("arbitrary", "arbitrary"), vmem_limit_bytes=limit
        ),
        name="token_mixers",
    )(slopes, lq1.reshape(1, -1), lk1.reshape(1, -1), lq2.reshape(1, -1), lk2.reshape(1, -1),
      g.reshape(1, hb), proj, proj, proj, log_gammas, proj, proj, proj, proj, *side_weights)
    return tuple(outs)


def _xattn_kernel(*refs, n_side):
    ins, o_ref, (xo_ref,), side = _split_refs(refs, 5, n_side)
    xq_ref, xk_ref, xv_ref, res_ref, wo_ref = ins
    _cast_blocks(side)
    d_model = xq_ref.shape[1]
    hd = d_model // XATTN_HEADS
    scale2 = hd ** -0.5 * LOG2_E
    for h in range(XATTN_HEADS):
        cols = slice(h * hd, (h + 1) * hd)
        s = _dot_nt(xq_ref[:, cols], xk_ref[:, cols]) * scale2
        p = jnp.exp2(s - jnp.max(s, axis=-1, keepdims=True))
        p = p * (1.0 / jnp.sum(p, axis=-1, keepdims=True))
        xo_ref[:, cols] = _dot(p.astype(BF16), xv_ref[:, cols]).astype(xo_ref.dtype)
    o_ref[...] = res_ref[...] + _dot(xo_ref[...], wo_ref[...])


def _cross_attention(xq, xk, xv, res, wo, *, batch, seq, mem_len, tq, side_weights=()):
    d = xq.shape[1]
    nq = seq // tq
    grid = (batch, nq)
    side = _SideCasts(side_weights, grid)
    limit = _vmem_limit(
        [
            _nbytes((tq, d), xq.dtype),
            _nbytes((mem_len, d), xk.dtype),
            _nbytes((mem_len, d), xv.dtype),
            _nbytes((tq, d), F32),
            _nbytes((d, d), wo.dtype),
            _nbytes((tq, d), F32),
        ] + side.window_bytes,
        [_nbytes((tq, d), BF16), _nbytes((tq, d), F32)],
    )
    outs = pl.pallas_call(
        functools.partial(_xattn_kernel, n_side=len(side)),
        grid=grid,
        in_specs=[
            pl.BlockSpec((tq, d), lambda b, i: (b * nq + i, 0)),
            pl.BlockSpec((mem_len, d), lambda b, i: (b, 0)),
            pl.BlockSpec((mem_len, d), lambda b, i: (b, 0)),
            pl.BlockSpec((tq, d), lambda b, i: (b * nq + i, 0)),
            pl.BlockSpec((d, d), lambda b, i: (0, 0)),
        ] + side.in_specs,
        out_specs=[pl.BlockSpec((tq, d), lambda b, i: (b * nq + i, 0))] + side.out_specs,
        out_shape=[jax.ShapeDtypeStruct((batch * seq, d), F32)] + side.out_shapes,
        scratch_shapes=[pltpu.VMEM((tq, d), BF16)],
        compiler_params=pltpu.CompilerParams(
            dimension_semantics=("arbitrary", "arbitrary"), vmem_limit_bytes=limit
        ),
        name="cross_attention",
    )(xq, xk, xv, res, wo, *side_weights)
    return tuple(outs)


def _ffn_kernel(x_ref, g_ref, wg_ref, wu_ref, wd_ref, gf_ref, o_ref, h_ref, *, final_norm):
    f = pl.program_id(1)
    last = pl.num_programs(1) - 1
    blocks = _row_blocks(x_ref.shape[0])

    def partial_ffn(h):
        gate = _dot(h, wg_ref[...])
        up = _dot(h, wu_ref[...])
        act = (gate * jax.nn.sigmoid(gate)) * up
        return _dot(act.astype(BF16), wd_ref[...])

    @pl.when(f == 0)
    def _():
        for rows in blocks:
            x = x_ref[rows, :]
            h = (_rms(x) * g_ref[...]).astype(h_ref.dtype)
            h_ref[rows, :] = h
            o_ref[rows, :] = x + partial_ffn(h)

    if final_norm:
        @pl.when(jnp.logical_and(f > 0, f < last))
        def _():
            o_ref[...] += partial_ffn(h_ref[...])

        @pl.when(f == last)
        def _():
            for rows in blocks:
                y = o_ref[rows, :] + partial_ffn(h_ref[rows, :])
                o_ref[rows, :] = _rms(y) * gf_ref[...]
    else:
        @pl.when(f > 0)
        def _():
            o_ref[...] += partial_ffn(h_ref[...])


def _ffn(x, g, wg, wu, wd, gf, *, final_norm, tm, tf):
    m, d = x.shape
    d_ff = wg.shape[1]
    limit = _vmem_limit(
        [
            _nbytes((tm, d), F32),
            _nbytes((d, tf), wg.dtype),
            _nbytes((d, tf), wu.dtype),
            _nbytes((tf, d), wd.dtype),
            _nbytes((tm, d), F32),
        ],
        [_nbytes((tm, d), BF16), _nbytes((tm, d), F32), 4 * _nbytes((tm, tf), F32)],
    )
    return pl.pallas_call(
        functools.partial(_ffn_kernel, final_norm=final_norm),
        grid=(m // tm, d_ff // tf),
        in_specs=[
            pl.BlockSpec((tm, d), lambda i, f: (i, 0)),
            pl.BlockSpec((1, d), lambda i, f: (0, 0)),
            pl.BlockSpec((d, tf), lambda i, f: (0, f)),
            pl.BlockSpec((d, tf), lambda i, f: (0, f)),
            pl.BlockSpec((tf, d), lambda i, f: (f, 0)),
            pl.BlockSpec((1, d), lambda i, f: (0, 0)),
        ],
        out_specs=pl.BlockSpec((tm, d), lambda i, f: (i, 0)),
        out_shape=jax.ShapeDtypeStruct((m, d), F32),
        scratch_shapes=[pltpu.VMEM((tm, d), BF16)],
        compiler_params=pltpu.CompilerParams(
            dimension_semantics=("parallel", "arbitrary"), vmem_limit_bytes=limit
        ),
        name="swiglu_ffn",
    )(x, g.reshape(1, d), wg, wu, wd, gf.reshape(1, d))


def kernel(x, mem, norm_mix_g, w_in, lambda_q1, lambda_k1, lambda_q2, lambda_k2, da_subln_g, w_o, norm_x_g, norm_mem_g, w_xq, w_xk, w_xv, w_xo, norm_ffn_g, w_gate, w_up, w_down, norm_f_g):
    batch, seq, d_model = x.shape
    mem_len = mem.shape[1]
    depth = w_in.shape[0]
    slopes = jnp.asarray(2.0 ** (-8.0 * np.arange(1, DA_HEADS + 1) / DA_HEADS), dtype=F32)
    log_gammas = jnp.asarray(np.log(1.0 - 2.0 ** (-5.0 - np.arange(RET_HEADS))), dtype=F32)

    xf = x.reshape(batch * seq, d_model)
    memf = mem.reshape(batch * mem_len, d_model)
    for l in range(depth):
        lam_init = 0.8 - 0.6 * math.exp(-0.3 * l)
        n_dq = DA_HEADS * 2 * DA_HEAD_DIM
        col_scale = jnp.where(jnp.arange(w_in.shape[2]) < n_dq, DA_HEAD_DIM ** -0.5 * LOG2_E, 1.0)
        proj, wb_o, wb_xq = _norm_matmul(
            xf, norm_mix_g[l], w_in[l], BF16, tm=1024, tn=1024, group_tiles=2, name="in_proj",
            col_scale=col_scale.astype(F32), side_weights=(w_o[l], w_xq[l]))
        da, ret, wb_up, wb_xk, wb_xv, wb_xo = _mixers(
            proj, slopes, lambda_q1[l], lambda_k1[l], lambda_q2[l], lambda_k2[l], da_subln_g[l],
            log_gammas, batch=batch, seq=seq, lam_init=lam_init, tq=256,
            side_weights=(w_up[l], w_xk[l], w_xv[l], w_xo[l]))
        xf = _matmul2_res(da, ret, wb_o, xf, tm=512, tn=d_model, name="out_proj")

        xq, wb_gate = _norm_matmul(xf, norm_x_g[l], wb_xq, BF16, tm=1024, tn=d_model, name="xattn_q",
                                   side_weights=(w_gate[l],))
        xk, xv = _norm_matmul_pair(memf, norm_mem_g[l], wb_xk, wb_xv, BF16, tm=1024, tn=1024,
                                   name="xattn_kv")
        xf, wb_down = _cross_attention(xq, xk, xv, xf, wb_xo, batch=batch, seq=seq, mem_len=mem_len,
                                       tq=512, side_weights=(w_down[l],))

        xf = _ffn(xf, norm_ffn_g[l], wb_gate, wb_up, wb_down, norm_f_g,
                  final_norm=(l == depth - 1), tm=1024, tf=512)
    return xf.reshape(batch, seq, d_model)
```

```python
import functools
import math

import jax
import jax.numpy as jnp
import numpy as np
from jax import lax
from jax.experimental import pallas as pl
from jax.experimental.pallas import tpu as pltpu

F32 = jnp.float32
BF16 = jnp.bfloat16

DA_HEADS = 4
DA_HEAD_DIM = 128
DA_V_DIM = 2 * DA_HEAD_DIM
RET_HEADS = 4
RET_QK_DIM = 128
RET_V_DIM = 256
XATTN_HEADS = 4
RET_CHUNK = 256
NORM_EPS = 1e-6
NEG_INF = -1e30
LOG2_E = math.log2(math.e)
NORM_BLOCK_ROWS = 256

V7X_LANES = 128
MXU_COLS = 256
BF16_TILE_ROWS = 16
V7X_VMEM_BYTES = 64 * 1024 * 1024
V7X_VMEM_USABLE_BYTES = V7X_VMEM_BYTES - 8 * 1024 * 1024
COMPILER_SCRATCH_BYTES = 4 * 1024 * 1024


def _nbytes(shape, dtype):
    return int(np.prod(shape)) * jnp.dtype(dtype).itemsize


def _vmem_limit(pipelined, resident):
    need = 2 * sum(pipelined) + sum(resident) + COMPILER_SCRATCH_BYTES
    return int(min(V7X_VMEM_USABLE_BYTES, need))


def _rms(x):
    return x * lax.rsqrt(jnp.mean(x * x, axis=-1, keepdims=True) + NORM_EPS)


def _fold_lanes(x, op):
    tiles = [x[:, i:i + V7X_LANES] for i in range(0, x.shape[1], V7X_LANES)]
    return functools.reduce(op, tiles)


def _dot(a, b):
    return jnp.dot(a, b, preferred_element_type=F32)


def _dot_nt(a, b):
    return lax.dot_general(a, b, (((1,), (1,)), ((), ())), preferred_element_type=F32)


def _dot_tn(a, b):
    return lax.dot_general(a, b, (((0,), (0,)), ((), ())), preferred_element_type=F32)


class _SideCasts:
    def __init__(self, weights, grid):
        self.weights = list(weights)
        self.grid = tuple(grid)
        n_steps = int(np.prod(self.grid))
        self.plans = []
        for w in self.weights:
            rows, n_blocks = w.shape[0], n_steps
            while rows % n_blocks or (rows // n_blocks) % BF16_TILE_ROWS:
                n_blocks -= 1
            self.plans.append((n_blocks, rows // n_blocks))

    def __len__(self):
        return len(self.weights)

    def _specs(self):
        specs = []
        for w, (n_blocks, block_rows) in zip(self.weights, self.plans):
            def index(*ids, n_blocks=n_blocks):
                step = ids[0]
                for extent, idx in zip(self.grid[1:], ids[1:]):
                    step = step * extent + idx
                return (jnp.minimum(step, n_blocks - 1), 0)
            specs.append(pl.BlockSpec((block_rows, w.shape[1]), index))
        return specs

    in_specs = property(_specs)
    out_specs = property(_specs)

    @property
    def out_shapes(self):
        return [jax.ShapeDtypeStruct(w.shape, BF16) for w in self.weights]

    @property
    def window_bytes(self):
        return [_nbytes((rows, w.shape[1]), dt)
                for w, (_, rows) in zip(self.weights, self.plans) for dt in (w.dtype, BF16)]


def _split_refs(refs, n_in, n_side):
    ins, rest = refs[:n_in], refs[n_in:]
    side_in, rest = rest[:n_side], rest[n_side:]
    out, side_out, scratch = rest[0], rest[1:1 + n_side], rest[1 + n_side:]
    return ins, out, scratch, list(zip(side_in, side_out))


def _cast_blocks(pairs):
    for src, dst in pairs:
        dst[...] = src[...].astype(dst.dtype)


def _row_blocks(n_rows):
    step = min(NORM_BLOCK_ROWS, n_rows)
    return [slice(r, r + step) for r in range(0, n_rows, step)]


def _col_chunks(n_cols):
    step = min(MXU_COLS, n_cols)
    return [slice(c, c + step) for c in range(0, n_cols, step)]


def _norm_matmul_kernel(*refs, n_side, cast_w):
    ins, o_ref, scratch, side = _split_refs(refs, 4 if cast_w else 3, n_side)
    x_ref, g_ref, w_ref = ins[:3]
    h_ref = scratch[0]
    wb_ref = scratch[1] if cast_w else w_ref
    n, t = pl.program_id(1), pl.program_id(2)

    def cast_weight(cols):
        wb_ref[:, cols] = (w_ref[:, cols] * ins[3][:, cols]).astype(wb_ref.dtype)

    @pl.when(n == 0)
    def _():
        _cast_blocks(side)
        if cast_w:
            @pl.when(t == 0)
            def _():
                cast_weight(slice(None))
        for rows in _row_blocks(x_ref.shape[0]):
            h = (_rms(x_ref[rows, :]) * g_ref[...]).astype(h_ref.dtype)
            h_ref[t, rows, :] = h
            o_ref[rows, :] = _dot(h, wb_ref[...]).astype(o_ref.dtype)

    @pl.when(n > 0)
    def _():
        _cast_blocks(side)
        if cast_w:
            @pl.when(t == 0)
            def _():
                for cols in _col_chunks(w_ref.shape[1]):
                    cast_weight(cols)
                    o_ref[:, cols] = _dot(h_ref[t], wb_ref[:, cols]).astype(o_ref.dtype)

            @pl.when(t > 0)
            def _():
                o_ref[...] = _dot(h_ref[t], wb_ref[...]).astype(o_ref.dtype)
        else:
            o_ref[...] = _dot(h_ref[t], w_ref[...]).astype(o_ref.dtype)


def _norm_matmul(x, g, w, out_dtype, *, tm, tn, name, group_tiles=1, col_scale=None,
                 side_weights=()):
    m, d = x.shape
    n = w.shape[1]
    tm, tn = min(tm, m), min(tn, n)
    gt = group_tiles
    grid = (m // (tm * gt), n // tn, gt)
    side = _SideCasts(side_weights, grid)
    cast_w = w.dtype != BF16
    assert cast_w or col_scale is None
    weight_ins, weight_specs = [w], [pl.BlockSpec((d, tn), lambda gi, j, t: (0, j))]
    if cast_w:
        cs = jnp.ones((n,), F32) if col_scale is None else col_scale
        weight_ins.append(cs.reshape(1, n))
        weight_specs.append(pl.BlockSpec((1, tn), lambda gi, j, t: (0, j)))
    scratch = [pltpu.VMEM((gt, tm, d), BF16)] + ([pltpu.VMEM((d, tn), BF16)] if cast_w else [])
    limit = _vmem_limit(
        [_nbytes((tm, d), x.dtype), _nbytes((d, tn), w.dtype), _nbytes((tm, tn), out_dtype)]
        + side.window_bytes,
        [_nbytes((gt, tm, d), BF16), _nbytes((d, tn), BF16) * cast_w],
    )
    outs = pl.pallas_call(
        functools.partial(_norm_matmul_kernel, n_side=len(side), cast_w=cast_w),
        grid=grid,
        in_specs=[
            pl.BlockSpec((tm, d), lambda gi, j, t: (gi * gt + jnp.where(j == 0, t, gt - 1), 0)),
            pl.BlockSpec((1, d), lambda gi, j, t: (0, 0)),
        ] + weight_specs + side.in_specs,
        out_specs=[pl.BlockSpec((tm, tn), lambda gi, j, t: (gi * gt + t, j))] + side.out_specs,
        out_shape=[jax.ShapeDtypeStruct((m, n), out_dtype)] + side.out_shapes,
        scratch_shapes=scratch,
        compiler_params=pltpu.CompilerParams(
            dimension_semantics=("arbitrary", "arbitrary", "arbitrary"), vmem_limit_bytes=limit
        ),
        name=name,
    )(x, g.reshape(1, d), *weight_ins, *side_weights)
    return tuple(outs)


def _norm_matmul_pair_kernel(x_ref, g_ref, w1_ref, w2_ref, o1_ref, o2_ref, h_ref):
    j = pl.program_id(1)

    @pl.when(j == 0)
    def _():
        for rows in _row_blocks(x_ref.shape[0]):
            h = (_rms(x_ref[rows, :]) * g_ref[...]).astype(h_ref.dtype)
            h_ref[rows, :] = h
            o1_ref[rows, :] = _dot(h, w1_ref[...]).astype(o1_ref.dtype)
            o2_ref[rows, :] = _dot(h, w2_ref[...]).astype(o2_ref.dtype)

    @pl.when(j > 0)
    def _():
        o1_ref[...] = _dot(h_ref[...], w1_ref[...]).astype(o1_ref.dtype)
        o2_ref[...] = _dot(h_ref[...], w2_ref[...]).astype(o2_ref.dtype)


def _norm_matmul_pair(x, g, w1, w2, out_dtype, *, tm, tn, name):
    m, d = x.shape
    n = w1.shape[1]
    assert w1.shape == w2.shape
    tm, tn = min(tm, m), min(tn, n)
    w_spec = pl.BlockSpec((d, tn), lambda i, j: (0, j))
    o_spec = pl.BlockSpec((tm, tn), lambda i, j: (i, j))
    limit = _vmem_limit(
        [_nbytes((tm, d), x.dtype)] + [_nbytes((d, tn), w1.dtype), _nbytes((tm, tn), out_dtype)] * 2,
        [_nbytes((tm, d), BF16), _nbytes((tm, d), F32)],
    )
    return pl.pallas_call(
        _norm_matmul_pair_kernel,
        grid=(m // tm, n // tn),
        in_specs=[pl.BlockSpec((tm, d), lambda i, j: (i, 0)), pl.BlockSpec((1, d), lambda i, j: (0, 0)),
                  w_spec, w_spec],
        out_specs=[o_spec, o_spec],
        out_shape=[jax.ShapeDtypeStruct((m, n), out_dtype)] * 2,
        scratch_shapes=[pltpu.VMEM((tm, d), BF16)],
        compiler_params=pltpu.CompilerParams(
            dimension_semantics=("arbitrary", "arbitrary"), vmem_limit_bytes=limit
        ),
        name=name,
    )(x, g.reshape(1, d), w1, w2)


def _matmul2_res_kernel(a1_ref, a2_ref, w1_ref, w2_ref, res_ref, o_ref):
    acc = _dot(a1_ref[...], w1_ref[...]) + _dot(a2_ref[...], w2_ref[...])
    o_ref[...] = res_ref[...] + acc


def _matmul2_res(a1, a2, w, res, *, tm, tn, name):
    m, k1 = a1.shape
    k2 = a2.shape[1]
    assert k1 == k2 and w.shape[0] == k1 + k2
    n = w.shape[1]
    limit = _vmem_limit(
        [
            _nbytes((tm, k1), a1.dtype),
            _nbytes((tm, k2), a2.dtype),
            _nbytes((k1, tn), w.dtype),
            _nbytes((k2, tn), w.dtype),
            _nbytes((tm, tn), F32),
            _nbytes((tm, tn), F32),
        ],
        [_nbytes((tm, tn), F32)],
    )
    return pl.pallas_call(
        _matmul2_res_kernel,
        grid=(m // tm, n // tn),
        in_specs=[
            pl.BlockSpec((tm, k1), lambda i, j: (i, 0)),
            pl.BlockSpec((tm, k2), lambda i, j: (i, 0)),
            pl.BlockSpec((k1, tn), lambda i, j: (0, j)),
            pl.BlockSpec((k2, tn), lambda i, j: (1, j)),
            pl.BlockSpec((tm, tn), lambda i, j: (i, j)),
        ],
        out_specs=pl.BlockSpec((tm, tn), lambda i, j: (i, j)),
        out_shape=jax.ShapeDtypeStruct((m, n), F32),
        compiler_params=pltpu.CompilerParams(
            dimension_semantics=("parallel", "arbitrary"), vmem_limit_bytes=limit
        ),
        name=name,
    )(a1, a2, w, w, res)


def _bf16_part(x):
    bits = lax.bitcast_convert_type(x, jnp.int32) & jnp.int32(-65536)
    return lax.bitcast_convert_type(bits, F32)


def _diff_attn_body(ins, o_ref, scratch, *, tq, lam_init):
    slope_ref, lq1_ref, lk1_ref, lq2_ref, lk2_ref, g_ref, q_ref, k_ref, v_ref = ins
    kx_ref, s_ref, p_ref = scratch
    s_len = q_ref.shape[0]
    d = DA_HEAD_DIM
    lam = (jnp.exp(jnp.sum(lq1_ref[...] * lk1_ref[...], axis=-1, keepdims=True))
           - jnp.exp(jnp.sum(lq2_ref[...] * lk2_ref[...], axis=-1, keepdims=True))
           + lam_init)

    lane = lax.broadcasted_iota(jnp.int32, (1, d), 1)
    slope2 = jnp.full((1, d), slope_ref[pl.program_id(1)] * LOG2_E, F32)
    piece_hi = _bf16_part(slope2)
    rest = slope2 - piece_hi
    piece_mid = _bf16_part(rest)
    piece_lo = _bf16_part(rest - piece_mid)
    piece = jnp.where((lane == 0) | (lane == 3), piece_hi,
                      jnp.where((lane == 1) | (lane == 4), piece_mid, piece_lo))
    q_extra = jnp.where(lane < 3, piece * 256.0, jnp.where(lane < 6, piece, 0.0))
    q_extra = jnp.broadcast_to(q_extra, (tq, d)).astype(BF16)
    kpos = lax.broadcasted_iota(jnp.int32, (s_len, d), 0)
    klane = lax.broadcasted_iota(jnp.int32, (s_len, d), 1)
    k_extra = jnp.where(klane < 3, kpos >> 8, jnp.where(klane < 6, kpos & 255, 0))
    kx_ref[...] = k_extra.astype(F32).astype(kx_ref.dtype)

    row = lax.broadcasted_iota(jnp.int32, (tq, tq), 0)
    col = lax.broadcasted_iota(jnp.int32, (tq, tq), 1)
    causal = col <= row

    for qi in reversed(range(s_len // tq)):
        lo, hi = qi * tq, (qi + 1) * tq
        key_blocks = [slice(j * tq, (j + 1) * tq) for j in range(qi + 1)]
        heads = []
        for c in range(2):
            dcols = slice(c * d, (c + 1) * d)
            q_aug = jnp.concatenate([q_ref[lo:hi, dcols], q_extra], axis=1)
            m = None
            for j, cols in enumerate(key_blocks):
                k_aug = jnp.concatenate([k_ref[cols, dcols], kx_ref[cols, :]], axis=1)
                s = _dot_nt(q_aug, k_aug)
                if j == qi:
                    s = jnp.where(causal, s, NEG_INF)
                s_ref[c, :, cols] = s
                bm = _fold_lanes(s, jnp.maximum)
                m = bm if m is None else jnp.maximum(m, bm)
            m = jnp.max(m, axis=-1, keepdims=True)
            l = None
            for cols in key_blocks:
                p = jnp.exp2(s_ref[c, :, cols] - m)
                bl = _fold_lanes(p, jnp.add)
                l = bl if l is None else l + bl
                p_ref[c, :, cols] = p.astype(p_ref.dtype)
            l = jnp.sum(l, axis=-1, keepdims=True)
            acc = _dot(p_ref[c, :, 0:hi], v_ref[0:hi, :])
            heads.append((acc, l))
        (acc1, l1), (acc2, l2) = heads
        out = acc1 * (1.0 / l1) - acc2 * (lam / l2)
        y = _rms(out) * g_ref[...] * (1.0 - lam_init)
        o_ref[lo:hi, :] = y.astype(o_ref.dtype)
        yield


def _retention_body(ins, o_ref):
    lg_ref, q_ref, k_ref, v_ref, gate_ref = ins
    s_len = q_ref.shape[0]
    c = RET_CHUNK
    dk, dv = RET_QK_DIM, RET_V_DIM
    scale = dk ** -0.5
    lg = lg_ref[pl.program_id(1)]

    row = lax.broadcasted_iota(jnp.int32, (c, c), 0)
    col = lax.broadcasted_iota(jnp.int32, (c, c), 1)
    diff = (row - col).astype(F32)
    intra = jnp.where(diff >= 0, jnp.exp(lg * jnp.maximum(diff, 0.0)), 0.0) * scale
    row_k = lax.broadcasted_iota(jnp.int32, (c, dk), 0).astype(F32)
    k_decay = jnp.exp(lg * (float(c - 1) - row_k)) * scale
    row_v = lax.broadcasted_iota(jnp.int32, (c, dv), 0).astype(F32)
    q_decay = jnp.exp(lg * (row_v + 1.0))
    chunk_decay = jnp.exp(jnp.full((1, dv), lg * float(c), F32))

    state = jnp.zeros((dk, dv), F32)
    for i in range(s_len // c):
        lo, hi = i * c, (i + 1) * c
        q = q_ref[lo:hi, :]
        k = k_ref[lo:hi, :]
        v = v_ref[lo:hi, :]
        scores = _dot_nt(q, k) * intra
        y = _dot(scores.astype(BF16), v)
        if i > 0:
            y = y + _dot(q, state.astype(BF16)) * q_decay
        if i + 1 < s_len // c:
            kd = (k.astype(F32) * k_decay).astype(BF16)
            state = state * chunk_decay + _dot_tn(kd, v)
        gate = gate_ref[lo:hi, :].astype(F32)
        o_ref[lo:hi, :] = (_rms(y) * (gate * jax.nn.sigmoid(gate))).astype(o_ref.dtype)
        yield


N_ATTN_INS, N_RET_INS = 9, 5


def _mixers_kernel(*refs, n_side, tq, lam_init):
    ins, rest = refs[:N_ATTN_INS + N_RET_INS], refs[N_ATTN_INS + N_RET_INS:]
    side_in, rest = rest[:n_side], rest[n_side:]
    da_ref, ret_ref = rest[:2]
    side_out, scratch = rest[2:2 + n_side], rest[2 + n_side:]
    _cast_blocks(zip(side_in, side_out))
    pending = [_diff_attn_body(ins[:N_ATTN_INS], da_ref, scratch, tq=tq, lam_init=lam_init),
               _retention_body(ins[N_ATTN_INS:], ret_ref)]
    while pending:
        pending = [body for body in pending if next(body, StopIteration) is not StopIteration]


def _mixers(proj, slopes, lq1, lk1, lq2, lk2, g, log_gammas, *, batch, seq, lam_init, tq,
            side_weights=()):
    assert DA_HEADS == RET_HEADS
    hb = DA_V_DIM
    q_blk0, k_blk0, v_blk0 = 0, DA_HEADS, 2 * DA_HEADS
    da_cols = 3 * DA_HEADS * DA_V_DIM
    rq_blk0 = da_cols // RET_QK_DIM
    rk_blk0 = rq_blk0 + RET_HEADS
    rv_blk0 = (da_cols + 2 * RET_HEADS * RET_QK_DIM) // RET_V_DIM
    rg_blk0 = rv_blk0 + RET_HEADS
    grid = (batch, DA_HEADS)
    side = _SideCasts(side_weights, grid)
    scratch = [
        pltpu.VMEM((seq, DA_HEAD_DIM), BF16),
        pltpu.VMEM((2, tq, seq), F32),
        pltpu.VMEM((2, tq, seq), BF16),
    ]
    limit = _vmem_limit(
        [_nbytes((seq, hb), proj.dtype)] * 4
        + [_nbytes((seq, RET_QK_DIM), proj.dtype)] * 2 + [_nbytes((seq, RET_V_DIM), proj.dtype)] * 3
        + side.window_bytes,
        [_nbytes((seq, DA_HEAD_DIM), BF16),
         3 * _nbytes((2, tq, seq), F32), 3 * _nbytes((2, tq, seq), BF16)],
    )
    vec = pl.BlockSpec((1, DA_HEAD_DIM), lambda b, h: (0, 0))
    smem = pl.BlockSpec(memory_space=pltpu.SMEM)

    def head_cols(width, blk0):
        return pl.BlockSpec((seq, width), lambda b, h: (b, blk0 + h))

    outs = pl.pallas_call(
        functools.partial(_mixers_kernel, n_side=len(side), tq=tq, lam_init=lam_init),
        grid=grid,
        in_specs=[
            smem, vec, vec, vec, vec,
            pl.BlockSpec((1, hb), lambda b, h: (0, 0)),
            head_cols(hb, q_blk0), head_cols(hb, k_blk0), head_cols(hb, v_blk0),
            smem,
            head_cols(RET_QK_DIM, rq_blk0), head_cols(RET_QK_DIM, rk_blk0),
            head_cols(RET_V_DIM, rv_blk0), head_cols(RET_V_DIM, rg_blk0),
        ] + side.in_specs,
        out_specs=[head_cols(hb, 0), head_cols(RET_V_DIM, 0)] + side.out_specs,
        out_shape=[jax.ShapeDtypeStruct((batch * seq, DA_HEADS * hb), BF16),
                   jax.ShapeDtypeStruct((batch * seq, RET_HEADS * RET_V_DIM), BF16)] + side.out_shapes,
        scratch_shapes=scratch,
        compiler_params=pltpu.CompilerParams(
            dimension_semantics=("arbitrary", "arbitrary"), vmem_limit_bytes=limit
        ),
        name="token_mixers",
    )(slopes, lq1.reshape(1, -1), lk1.reshape(1, -1), lq2.reshape(1, -1), lk2.reshape(1, -1),
      g.reshape(1, hb), proj, proj, proj, log_gammas, proj, proj, proj, proj, *side_weights)
    return tuple(outs)


def _xattn_kernel(*refs, n_side):
    ins, o_ref, (xo_ref,), side = _split_refs(refs, 5, n_side)
    xq_ref, xk_ref, xv_ref, res_ref, wo_ref = ins
    step = pl.program_id(0)
    d_model = xq_ref.shape[1]
    hd = d_model // XATTN_HEADS
    scale2 = hd ** -0.5 * LOG2_E

    @pl.when(step == 0)
    def _():
        xo_ref[...] = jnp.zeros_like(xo_ref)

    def run(prev, cur):
        _cast_blocks(side)
        o_ref[...] = res_ref[...] + _dot(xo_ref[prev], wo_ref[...])
        for h in range(XATTN_HEADS):
            cols = slice(h * hd, (h + 1) * hd)
            s = _dot_nt(xq_ref[:, cols], xk_ref[:, cols]) * scale2
            p = jnp.exp2(s - jnp.max(s, axis=-1, keepdims=True))
            p = p * (1.0 / jnp.sum(p, axis=-1, keepdims=True))
            xo_ref[cur, :, cols] = _dot(p.astype(BF16), xv_ref[:, cols]).astype(xo_ref.dtype)

    @pl.when(lax.rem(step, 2) == 0)
    def _():
        run(prev=1, cur=0)

    @pl.when(lax.rem(step, 2) == 1)
    def _():
        run(prev=0, cur=1)


def _cross_attention(xq, xk, xv, res, wo, *, batch, seq, mem_len, tq, side_weights=()):
    d = xq.shape[1]
    nq = seq // tq
    n_tiles = batch * nq
    grid = (n_tiles + 1,)
    side = _SideCasts(side_weights, grid)

    def attn_tile(s):
        return jnp.minimum(s, n_tiles - 1)

    def proj_tile(s):
        return jnp.maximum(s - 1, 0)
    limit = _vmem_limit(
        [
            _nbytes((tq, d), xq.dtype),
            _nbytes((mem_len, d), xk.dtype),
            _nbytes((mem_len, d), xv.dtype),
            _nbytes((tq, d), F32),
            _nbytes((d, d), wo.dtype),
            _nbytes((tq, d), F32),
        ] + side.window_bytes,
        [_nbytes((2, tq, d), BF16), _nbytes((tq, d), F32)],
    )
    outs = pl.pallas_call(
        functools.partial(_xattn_kernel, n_side=len(side)),
        grid=grid,
        in_specs=[
            pl.BlockSpec((tq, d), lambda s: (attn_tile(s), 0)),
            pl.BlockSpec((mem_len, d), lambda s: (attn_tile(s) // nq, 0)),
            pl.BlockSpec((mem_len, d), lambda s: (attn_tile(s) // nq, 0)),
            pl.BlockSpec((tq, d), lambda s: (proj_tile(s), 0)),
            pl.BlockSpec((d, d), lambda s: (0, 0)),
        ] + side.in_specs,
        out_specs=[pl.BlockSpec((tq, d), lambda s: (proj_tile(s), 0))] + side.out_specs,
        out_shape=[jax.ShapeDtypeStruct((batch * seq, d), F32)] + side.out_shapes,
        scratch_shapes=[pltpu.VMEM((2, tq, d), BF16)],
        compiler_params=pltpu.CompilerParams(
            dimension_semantics=("arbitrary",), vmem_limit_bytes=limit
        ),
        name="cross_attention",
    )(xq, xk, xv, res, wo, *side_weights)
    return tuple(outs)


def _ffn_kernel(x_ref, g_ref, wg_ref, wu_ref, wd_ref, gf_ref, o_ref, h_ref, *, final_norm):
    f = pl.program_id(1)
    last = pl.num_programs(1) - 1
    blocks = _row_blocks(x_ref.shape[0])

    def partial_ffn(h):
        gate = _dot(h, wg_ref[...])
        up = _dot(h, wu_ref[...])
        act = (gate * jax.nn.sigmoid(gate)) * up
        return _dot(act.astype(BF16), wd_ref[...])

    @pl.when(f == 0)
    def _():
        for rows in blocks:
            x = x_ref[rows, :]
            h = (_rms(x) * g_ref[...]).astype(h_ref.dtype)
            h_ref[rows, :] = h
            o_ref[rows, :] = x + partial_ffn(h)

    if final_norm:
        @pl.when(jnp.logical_and(f > 0, f < last))
        def _():
            o_ref[...] += partial_ffn(h_ref[...])

        @pl.when(f == last)
        def _():
            for rows in blocks:
                y = o_ref[rows, :] + partial_ffn(h_ref[rows, :])
                o_ref[rows, :] = _rms(y) * gf_ref[...]
    else:
        @pl.when(f > 0)
        def _():
            o_ref[...] += partial_ffn(h_ref[...])


def _ffn(x, g, wg, wu, wd, gf, *, final_norm, tm, tf):
    m, d = x.shape
    d_ff = wg.shape[1]
    limit = _vmem_limit(
        [
            _nbytes((tm, d), F32),
            _nbytes((d, tf), wg.dtype),
            _nbytes((d, tf), wu.dtype),
            _nbytes((tf, d), wd.dtype),
            _nbytes((tm, d), F32),
        ],
        [_nbytes((tm, d), BF16), _nbytes((tm, d), F32), 4 * _nbytes((tm, tf), F32)],
    )
    return pl.pallas_call(
        functools.partial(_ffn_kernel, final_norm=final_norm),
        grid=(m // tm, d_ff // tf),
        in_specs=[
            pl.BlockSpec((tm, d), lambda i, f: (i, 0)),
            pl.BlockSpec((1, d), lambda i, f: (0, 0)),
            pl.BlockSpec((d, tf), lambda i, f: (0, f)),
            pl.BlockSpec((d, tf), lambda i, f: (0, f)),
            pl.BlockSpec((tf, d), lambda i, f: (f, 0)),
            pl.BlockSpec((1, d), lambda i, f: (0, 0)),
        ],
        out_specs=pl.BlockSpec((tm, d), lambda i, f: (i, 0)),
        out_shape=jax.ShapeDtypeStruct((m, d), F32),
        scratch_shapes=[pltpu.VMEM((tm, d), BF16)],
        compiler_params=pltpu.CompilerParams(
            dimension_semantics=("parallel", "arbitrary"), vmem_limit_bytes=limit
        ),
        name="swiglu_ffn",
    )(x, g.reshape(1, d), wg, wu, wd, gf.reshape(1, d))


def kernel(x, mem, norm_mix_g, w_in, lambda_q1, lambda_k1, lambda_q2, lambda_k2, da_subln_g, w_o, norm_x_g, norm_mem_g, w_xq, w_xk, w_xv, w_xo, norm_ffn_g, w_gate, w_up, w_down, norm_f_g):
    batch, seq, d_model = x.shape
    mem_len = mem.shape[1]
    depth = w_in.shape[0]
    slopes = jnp.asarray(2.0 ** (-8.0 * np.arange(1, DA_HEADS + 1) / DA_HEADS), dtype=F32)
    log_gammas = jnp.asarray(np.log(1.0 - 2.0 ** (-5.0 - np.arange(RET_HEADS))), dtype=F32)

    xf = x.reshape(batch * seq, d_model)
    memf = mem.reshape(batch * mem_len, d_model)
    for l in range(depth):
        lam_init = 0.8 - 0.6 * math.exp(-0.3 * l)
        n_dq = DA_HEADS * 2 * DA_HEAD_DIM
        col_scale = jnp.where(jnp.arange(w_in.shape[2]) < n_dq, DA_HEAD_DIM ** -0.5 * LOG2_E, 1.0)
        proj, wb_o, wb_xq = _norm_matmul(
            xf, norm_mix_g[l], w_in[l], BF16, tm=1024, tn=1024, group_tiles=2, name="in_proj",
            col_scale=col_scale.astype(F32), side_weights=(w_o[l], w_xq[l]))
        da, ret, wb_up, wb_xk, wb_xv, wb_xo = _mixers(
            proj, slopes, lambda_q1[l], lambda_k1[l], lambda_q2[l], lambda_k2[l], da_subln_g[l],
            log_gammas, batch=batch, seq=seq, lam_init=lam_init, tq=256,
            side_weights=(w_up[l], w_xk[l], w_xv[l], w_xo[l]))
        xf = _matmul2_res(da, ret, wb_o, xf, tm=512, tn=d_model, name="out_proj")

        xq, wb_gate = _norm_matmul(xf, norm_x_g[l], wb_xq, BF16, tm=1024, tn=d_model, name="xattn_q",
                                   side_weights=(w_gate[l],))
        xk, xv = _norm_matmul_pair(memf, norm_mem_g[l], wb_xk, wb_xv, BF16, tm=1024, tn=1024,
                                   name="xattn_kv")
        xf, wb_down = _cross_attention(xq, xk, xv, xf, wb_xo, batch=batch, seq=seq, mem_len=mem_len,
                                       tq=512, side_weights=(w_down[l],))

        xf = _ffn(xf, norm_ffn_g[l], wb_gate, wb_up, wb_down, norm_f_g,
                  final_norm=(l == depth - 1), tm=1024, tf=512)
    return xf.reshape(batch, seq, d_model)
```

```python
import functools
import math

import jax
import jax.numpy as jnp
import numpy as np
from jax import lax
from jax.experimental import pallas as pl
from jax.experimental.pallas import tpu as pltpu

F32 = jnp.float32
BF16 = jnp.bfloat16

DA_HEADS = 4
DA_HEAD_DIM = 128
DA_V_DIM = 2 * DA_HEAD_DIM
RET_HEADS = 4
RET_QK_DIM = 128
RET_V_DIM = 256
XATTN_HEADS = 4
RET_CHUNK = 256
NORM_EPS = 1e-6
NEG_INF = -1e30
LOG2_E = math.log2(math.e)
NORM_BLOCK_ROWS = 256

ROW_TILE = 1024
COL_TILE = 1024
IN_PROJ_GROUP_TILES = 2
OUT_PROJ_ROW_TILE = 512
ATTN_Q_TILE = 256
XATTN_Q_TILE = 512
FFN_ROW_TILE, FFN_COL_TILE = 1024, 512

V7X_LANES = 128
MXU_COLS = 256
BF16_TILE_ROWS = 16
V7X_VMEM_BYTES = 64 * 1024 * 1024
V7X_VMEM_USABLE_BYTES = V7X_VMEM_BYTES - 8 * 1024 * 1024
COMPILER_SCRATCH_BYTES = 4 * 1024 * 1024


def _nbytes(shape, dtype):
    return int(np.prod(shape)) * jnp.dtype(dtype).itemsize


def _vmem_limit(pipelined, resident):
    need = 2 * sum(pipelined) + sum(resident) + COMPILER_SCRATCH_BYTES
    return int(min(V7X_VMEM_USABLE_BYTES, need))


def _rms(x):
    return x * lax.rsqrt(jnp.mean(x * x, axis=-1, keepdims=True) + NORM_EPS)


def _fold_lanes(x, op):
    tiles = [x[:, i:i + V7X_LANES] for i in range(0, x.shape[1], V7X_LANES)]
    return functools.reduce(op, tiles)


def _dot(a, b):
    return jnp.dot(a, b, preferred_element_type=F32)


def _dot_nt(a, b):
    return lax.dot_general(a, b, (((1,), (1,)), ((), ())), preferred_element_type=F32)


def _dot_tn(a, b):
    return lax.dot_general(a, b, (((0,), (0,)), ((), ())), preferred_element_type=F32)


class _SideCasts:
    def __init__(self, weights, grid):
        self.weights = list(weights)
        self.grid = tuple(grid)
        n_steps = int(np.prod(self.grid))
        self.plans = []
        for w in self.weights:
            rows, n_blocks = w.shape[0], n_steps
            while rows % n_blocks or (rows // n_blocks) % BF16_TILE_ROWS:
                n_blocks -= 1
            self.plans.append((n_blocks, rows // n_blocks))

    def __len__(self):
        return len(self.weights)

    def _specs(self):
        specs = []
        for w, (n_blocks, block_rows) in zip(self.weights, self.plans):
            def index(*ids, n_blocks=n_blocks):
                step = ids[0]
                for extent, idx in zip(self.grid[1:], ids[1:]):
                    step = step * extent + idx
                return (jnp.minimum(step, n_blocks - 1), 0)
            specs.append(pl.BlockSpec((block_rows, w.shape[1]), index))
        return specs

    in_specs = property(_specs)
    out_specs = property(_specs)

    @property
    def out_shapes(self):
        return [jax.ShapeDtypeStruct(w.shape, BF16) for w in self.weights]

    @property
    def window_bytes(self):
        return [_nbytes((rows, w.shape[1]), dt)
                for w, (_, rows) in zip(self.weights, self.plans) for dt in (w.dtype, BF16)]


def _split_refs(refs, n_in, n_side):
    ins, rest = refs[:n_in], refs[n_in:]
    side_in, rest = rest[:n_side], rest[n_side:]
    out, side_out, scratch = rest[0], rest[1:1 + n_side], rest[1 + n_side:]
    return ins, out, scratch, list(zip(side_in, side_out))


def _cast_blocks(pairs):
    for src, dst in pairs:
        dst[...] = src[...].astype(dst.dtype)


def _row_blocks(n_rows):
    step = min(NORM_BLOCK_ROWS, n_rows)
    return [slice(r, r + step) for r in range(0, n_rows, step)]


def _col_chunks(n_cols):
    step = min(MXU_COLS, n_cols)
    return [slice(c, c + step) for c in range(0, n_cols, step)]


def _norm_matmul_kernel(*refs, n_side, cast_w):
    ins, o_ref, scratch, side = _split_refs(refs, 4 if cast_w else 3, n_side)
    x_ref, g_ref, w_ref = ins[:3]
    h_ref = scratch[0]
    wb_ref = scratch[1] if cast_w else w_ref
    n, t = pl.program_id(1), pl.program_id(2)

    def cast_weight(cols):
        wb_ref[:, cols] = (w_ref[:, cols] * ins[3][:, cols]).astype(wb_ref.dtype)

    @pl.when(n == 0)
    def _():
        _cast_blocks(side)
        if cast_w:
            @pl.when(t == 0)
            def _():
                cast_weight(slice(None))
        for rows in _row_blocks(x_ref.shape[0]):
            h = (_rms(x_ref[rows, :]) * g_ref[...]).astype(h_ref.dtype)
            h_ref[t, rows, :] = h
            o_ref[rows, :] = _dot(h, wb_ref[...]).astype(o_ref.dtype)

    @pl.when(n > 0)
    def _():
        _cast_blocks(side)
        if cast_w:
            @pl.when(t == 0)
            def _():
                for cols in _col_chunks(w_ref.shape[1]):
                    cast_weight(cols)
                    o_ref[:, cols] = _dot(h_ref[t], wb_ref[:, cols]).astype(o_ref.dtype)

            @pl.when(t > 0)
            def _():
                o_ref[...] = _dot(h_ref[t], wb_ref[...]).astype(o_ref.dtype)
        else:
            o_ref[...] = _dot(h_ref[t], w_ref[...]).astype(o_ref.dtype)


def _norm_matmul(x, g, w, out_dtype, *, tm, tn, name, group_tiles=1, col_scale=None,
                 side_weights=()):
    m, d = x.shape
    n = w.shape[1]
    tm, tn = min(tm, m), min(tn, n)
    gt = group_tiles
    grid = (m // (tm * gt), n // tn, gt)
    side = _SideCasts(side_weights, grid)
    cast_w = w.dtype != BF16
    assert cast_w or col_scale is None
    weight_ins, weight_specs = [w], [pl.BlockSpec((d, tn), lambda gi, j, t: (0, j))]
    if cast_w:
        cs = jnp.ones((n,), F32) if col_scale is None else col_scale
        weight_ins.append(cs.reshape(1, n))
        weight_specs.append(pl.BlockSpec((1, tn), lambda gi, j, t: (0, j)))
    scratch = [pltpu.VMEM((gt, tm, d), BF16)] + ([pltpu.VMEM((d, tn), BF16)] if cast_w else [])
    limit = _vmem_limit(
        [_nbytes((tm, d), x.dtype), _nbytes((d, tn), w.dtype), _nbytes((tm, tn), out_dtype)]
        + side.window_bytes,
        [_nbytes((gt, tm, d), BF16), _nbytes((d, tn), BF16) * cast_w],
    )
    outs = pl.pallas_call(
        functools.partial(_norm_matmul_kernel, n_side=len(side), cast_w=cast_w),
        grid=grid,
        in_specs=[
            pl.BlockSpec((tm, d), lambda gi, j, t: (gi * gt + jnp.where(j == 0, t, gt - 1), 0)),
            pl.BlockSpec((1, d), lambda gi, j, t: (0, 0)),
        ] + weight_specs + side.in_specs,
        out_specs=[pl.BlockSpec((tm, tn), lambda gi, j, t: (gi * gt + t, j))] + side.out_specs,
        out_shape=[jax.ShapeDtypeStruct((m, n), out_dtype)] + side.out_shapes,
        scratch_shapes=scratch,
        compiler_params=pltpu.CompilerParams(
            dimension_semantics=("arbitrary", "arbitrary", "arbitrary"), vmem_limit_bytes=limit
        ),
        name=name,
    )(x, g.reshape(1, d), *weight_ins, *side_weights)
    return tuple(outs)


def _norm_matmul_pair_kernel(x_ref, g_ref, w1_ref, w2_ref, o1_ref, o2_ref, h_ref):
    j = pl.program_id(1)

    @pl.when(j == 0)
    def _():
        for rows in _row_blocks(x_ref.shape[0]):
            h = (_rms(x_ref[rows, :]) * g_ref[...]).astype(h_ref.dtype)
            h_ref[rows, :] = h
            o1_ref[rows, :] = _dot(h, w1_ref[...]).astype(o1_ref.dtype)
            o2_ref[rows, :] = _dot(h, w2_ref[...]).astype(o2_ref.dtype)

    @pl.when(j > 0)
    def _():
        o1_ref[...] = _dot(h_ref[...], w1_ref[...]).astype(o1_ref.dtype)
        o2_ref[...] = _dot(h_ref[...], w2_ref[...]).astype(o2_ref.dtype)


def _norm_matmul_pair(x, g, w1, w2, out_dtype, *, tm, tn, name):
    m, d = x.shape
    n = w1.shape[1]
    assert w1.shape == w2.shape
    tm, tn = min(tm, m), min(tn, n)
    w_spec = pl.BlockSpec((d, tn), lambda i, j: (0, j))
    o_spec = pl.BlockSpec((tm, tn), lambda i, j: (i, j))
    limit = _vmem_limit(
        [_nbytes((tm, d), x.dtype)] + [_nbytes((d, tn), w1.dtype), _nbytes((tm, tn), out_dtype)] * 2,
        [_nbytes((tm, d), BF16), _nbytes((tm, d), F32)],
    )
    return pl.pallas_call(
        _norm_matmul_pair_kernel,
        grid=(m // tm, n // tn),
        in_specs=[pl.BlockSpec((tm, d), lambda i, j: (i, 0)), pl.BlockSpec((1, d), lambda i, j: (0, 0)),
                  w_spec, w_spec],
        out_specs=[o_spec, o_spec],
        out_shape=[jax.ShapeDtypeStruct((m, n), out_dtype)] * 2,
        scratch_shapes=[pltpu.VMEM((tm, d), BF16)],
        compiler_params=pltpu.CompilerParams(
            dimension_semantics=("arbitrary", "arbitrary"), vmem_limit_bytes=limit
        ),
        name=name,
    )(x, g.reshape(1, d), w1, w2)


def _matmul2_res_kernel(a1_ref, a2_ref, w1_ref, w2_ref, res_ref, o_ref):
    acc = _dot(a1_ref[...], w1_ref[...]) + _dot(a2_ref[...], w2_ref[...])
    o_ref[...] = res_ref[...] + acc


def _matmul2_res(a1, a2, w, res, *, tm, tn, name):
    m, k1 = a1.shape
    k2 = a2.shape[1]
    assert k1 == k2 and w.shape[0] == k1 + k2
    n = w.shape[1]
    limit = _vmem_limit(
        [
            _nbytes((tm, k1), a1.dtype),
            _nbytes((tm, k2), a2.dtype),
            _nbytes((k1, tn), w.dtype),
            _nbytes((k2, tn), w.dtype),
            _nbytes((tm, tn), F32),
            _nbytes((tm, tn), F32),
        ],
        [_nbytes((tm, tn), F32)],
    )
    return pl.pallas_call(
        _matmul2_res_kernel,
        grid=(m // tm, n // tn),
        in_specs=[
            pl.BlockSpec((tm, k1), lambda i, j: (i, 0)),
            pl.BlockSpec((tm, k2), lambda i, j: (i, 0)),
            pl.BlockSpec((k1, tn), lambda i, j: (0, j)),
            pl.BlockSpec((k2, tn), lambda i, j: (1, j)),
            pl.BlockSpec((tm, tn), lambda i, j: (i, j)),
        ],
        out_specs=pl.BlockSpec((tm, tn), lambda i, j: (i, j)),
        out_shape=jax.ShapeDtypeStruct((m, n), F32),
        compiler_params=pltpu.CompilerParams(
            dimension_semantics=("parallel", "arbitrary"), vmem_limit_bytes=limit
        ),
        name=name,
    )(a1, a2, w, w, res)


def _bf16_part(x):
    bits = lax.bitcast_convert_type(x, jnp.int32) & jnp.int32(-65536)
    return lax.bitcast_convert_type(bits, F32)


def _diff_attn_body(ins, o_ref, scratch, *, tq, lam_init):
    slope_ref, lq1_ref, lk1_ref, lq2_ref, lk2_ref, g_ref, q_ref, k_ref, v_ref = ins
    kx_ref, s_ref, p_ref = scratch
    s_len = q_ref.shape[0]
    d = DA_HEAD_DIM
    lam = (jnp.exp(jnp.sum(lq1_ref[...] * lk1_ref[...], axis=-1, keepdims=True))
           - jnp.exp(jnp.sum(lq2_ref[...] * lk2_ref[...], axis=-1, keepdims=True))
           + lam_init)

    lane = lax.broadcasted_iota(jnp.int32, (1, d), 1)
    slope2 = jnp.full((1, d), slope_ref[pl.program_id(1)] * LOG2_E, F32)
    piece_hi = _bf16_part(slope2)
    rest = slope2 - piece_hi
    piece_mid = _bf16_part(rest)
    piece_lo = _bf16_part(rest - piece_mid)
    piece = jnp.where((lane == 0) | (lane == 3), piece_hi,
                      jnp.where((lane == 1) | (lane == 4), piece_mid, piece_lo))
    q_extra = jnp.where(lane < 3, piece * 256.0, jnp.where(lane < 6, piece, 0.0))
    q_extra = jnp.broadcast_to(q_extra, (tq, d)).astype(BF16)
    kpos = lax.broadcasted_iota(jnp.int32, (s_len, d), 0)
    klane = lax.broadcasted_iota(jnp.int32, (s_len, d), 1)
    k_extra = jnp.where(klane < 3, kpos >> 8, jnp.where(klane < 6, kpos & 255, 0))
    kx_ref[...] = k_extra.astype(F32).astype(kx_ref.dtype)

    row = lax.broadcasted_iota(jnp.int32, (tq, tq), 0)
    col = lax.broadcasted_iota(jnp.int32, (tq, tq), 1)
    causal = col <= row

    for qi in reversed(range(s_len // tq)):
        lo, hi = qi * tq, (qi + 1) * tq
        key_blocks = [slice(j * tq, (j + 1) * tq) for j in range(qi + 1)]
        heads = []
        for c in range(2):
            dcols = slice(c * d, (c + 1) * d)
            q_aug = jnp.concatenate([q_ref[lo:hi, dcols], q_extra], axis=1)
            m = None
            for j, cols in enumerate(key_blocks):
                k_aug = jnp.concatenate([k_ref[cols, dcols], kx_ref[cols, :]], axis=1)
                s = _dot_nt(q_aug, k_aug)
                if j == qi:
                    s = jnp.where(causal, s, NEG_INF)
                s_ref[c, :, cols] = s
                bm = _fold_lanes(s, jnp.maximum)
                m = bm if m is None else jnp.maximum(m, bm)
            m = jnp.max(m, axis=-1, keepdims=True)
            l = None
            for cols in key_blocks:
                p = jnp.exp2(s_ref[c, :, cols] - m)
                bl = _fold_lanes(p, jnp.add)
                l = bl if l is None else l + bl
                p_ref[c, :, cols] = p.astype(p_ref.dtype)
            l = jnp.sum(l, axis=-1, keepdims=True)
            acc = _dot(p_ref[c, :, 0:hi], v_ref[0:hi, :])
            heads.append((acc, l))
        (acc1, l1), (acc2, l2) = heads
        out = acc1 * (1.0 / l1) - acc2 * (lam / l2)
        y = _rms(out) * g_ref[...] * (1.0 - lam_init)
        o_ref[lo:hi, :] = y.astype(o_ref.dtype)
        yield


def _retention_body(ins, o_ref):
    lg_ref, q_ref, k_ref, v_ref, gate_ref = ins
    s_len = q_ref.shape[0]
    c = RET_CHUNK
    dk, dv = RET_QK_DIM, RET_V_DIM
    scale = dk ** -0.5
    lg = lg_ref[pl.program_id(1)]

    row = lax.broadcasted_iota(jnp.int32, (c, c), 0)
    col = lax.broadcasted_iota(jnp.int32, (c, c), 1)
    diff = (row - col).astype(F32)
    intra = jnp.where(diff >= 0, jnp.exp(lg * jnp.maximum(diff, 0.0)), 0.0) * scale
    row_k = lax.broadcasted_iota(jnp.int32, (c, dk), 0).astype(F32)
    k_decay = jnp.exp(lg * (float(c - 1) - row_k)) * scale
    row_v = lax.broadcasted_iota(jnp.int32, (c, dv), 0).astype(F32)
    q_decay = jnp.exp(lg * (row_v + 1.0))
    chunk_decay = jnp.exp(jnp.full((1, dv), lg * float(c), F32))

    state = jnp.zeros((dk, dv), F32)
    for i in range(s_len // c):
        lo, hi = i * c, (i + 1) * c
        q = q_ref[lo:hi, :]
        k = k_ref[lo:hi, :]
        v = v_ref[lo:hi, :]
        scores = _dot_nt(q, k) * intra
        y = _dot(scores.astype(BF16), v)
        if i > 0:
            y = y + _dot(q, state.astype(BF16)) * q_decay
        if i + 1 < s_len // c:
            kd = (k.astype(F32) * k_decay).astype(BF16)
            state = state * chunk_decay + _dot_tn(kd, v)
        gate = gate_ref[lo:hi, :].astype(F32)
        o_ref[lo:hi, :] = (_rms(y) * (gate * jax.nn.sigmoid(gate))).astype(o_ref.dtype)
        yield


N_ATTN_INS, N_RET_INS = 9, 5


def _mixers_kernel(*refs, n_side, tq, lam_init):
    ins, rest = refs[:N_ATTN_INS + N_RET_INS], refs[N_ATTN_INS + N_RET_INS:]
    side_in, rest = rest[:n_side], rest[n_side:]
    da_ref, ret_ref = rest[:2]
    side_out, scratch = rest[2:2 + n_side], rest[2 + n_side:]
    _cast_blocks(zip(side_in, side_out))
    pending = [_retention_body(ins[N_ATTN_INS:], ret_ref),
               _diff_attn_body(ins[:N_ATTN_INS], da_ref, scratch, tq=tq, lam_init=lam_init)]
    while pending:
        pending = [body for body in pending if next(body, StopIteration) is not StopIteration]


def _mixers(proj, slopes, lq1, lk1, lq2, lk2, g, log_gammas, *, batch, seq, lam_init, tq,
            side_weights=()):
    assert DA_HEADS == RET_HEADS
    hb = DA_V_DIM
    q_blk0, k_blk0, v_blk0 = 0, DA_HEADS, 2 * DA_HEADS
    da_cols = 3 * DA_HEADS * DA_V_DIM
    rq_blk0 = da_cols // RET_QK_DIM
    rk_blk0 = rq_blk0 + RET_HEADS
    rv_blk0 = (da_cols + 2 * RET_HEADS * RET_QK_DIM) // RET_V_DIM
    rg_blk0 = rv_blk0 + RET_HEADS
    grid = (batch, DA_HEADS)
    side = _SideCasts(side_weights, grid)
    scratch = [
        pltpu.VMEM((seq, DA_HEAD_DIM), BF16),
        pltpu.VMEM((2, tq, seq), F32),
        pltpu.VMEM((2, tq, seq), BF16),
    ]
    limit = _vmem_limit(
        [_nbytes((seq, hb), proj.dtype)] * 4
        + [_nbytes((seq, RET_QK_DIM), proj.dtype)] * 2 + [_nbytes((seq, RET_V_DIM), proj.dtype)] * 3
        + side.window_bytes,
        [_nbytes((seq, DA_HEAD_DIM), BF16),
         3 * _nbytes((2, tq, seq), F32), 3 * _nbytes((2, tq, seq), BF16)],
    )
    vec = pl.BlockSpec((1, DA_HEAD_DIM), lambda b, h: (0, 0))
    smem = pl.BlockSpec(memory_space=pltpu.SMEM)

    def head_cols(width, blk0):
        return pl.BlockSpec((seq, width), lambda b, h: (b, blk0 + h))

    outs = pl.pallas_call(
        functools.partial(_mixers_kernel, n_side=len(side), tq=tq, lam_init=lam_init),
        grid=grid,
        in_specs=[
            smem, vec, vec, vec, vec,
            pl.BlockSpec((1, hb), lambda b, h: (0, 0)),
            head_cols(hb, q_blk0), head_cols(hb, k_blk0), head_cols(hb, v_blk0),
            smem,
            head_cols(RET_QK_DIM, rq_blk0), head_cols(RET_QK_DIM, rk_blk0),
            head_cols(RET_V_DIM, rv_blk0), head_cols(RET_V_DIM, rg_blk0),
        ] + side.in_specs,
        out_specs=[head_cols(hb, 0), head_cols(RET_V_DIM, 0)] + side.out_specs,
        out_shape=[jax.ShapeDtypeStruct((batch * seq, DA_HEADS * hb), BF16),
                   jax.ShapeDtypeStruct((batch * seq, RET_HEADS * RET_V_DIM), BF16)] + side.out_shapes,
        scratch_shapes=scratch,
        compiler_params=pltpu.CompilerParams(
            dimension_semantics=("arbitrary", "arbitrary"), vmem_limit_bytes=limit
        ),
        name="token_mixers",
    )(slopes, lq1.reshape(1, -1), lk1.reshape(1, -1), lq2.reshape(1, -1), lk2.reshape(1, -1),
      g.reshape(1, hb), proj, proj, proj, log_gammas, proj, proj, proj, proj, *side_weights)
    return tuple(outs)


def _xattn_kernel(*refs, n_side):
    ins, o_ref, (xo_ref,), side = _split_refs(refs, 5, n_side)
    xq_ref, xk_ref, xv_ref, res_ref, wo_ref = ins
    _cast_blocks(side)
    d_model = xq_ref.shape[1]
    hd = d_model // XATTN_HEADS
    scale2 = hd ** -0.5 * LOG2_E
    for h in range(XATTN_HEADS):
        cols = slice(h * hd, (h + 1) * hd)
        s = _dot_nt(xq_ref[:, cols], xk_ref[:, cols]) * scale2
        p = jnp.exp2(s - jnp.max(s, axis=-1, keepdims=True))
        p = p * (1.0 / jnp.sum(p, axis=-1, keepdims=True))
        xo_ref[:, cols] = _dot(p.astype(BF16), xv_ref[:, cols]).astype(xo_ref.dtype)
    o_ref[...] = res_ref[...] + _dot(xo_ref[...], wo_ref[...])


def _cross_attention(xq, xk, xv, res, wo, *, batch, seq, mem_len, tq, side_weights=()):
    d = xq.shape[1]
    nq = seq // tq
    grid = (batch, nq)
    side = _SideCasts(side_weights, grid)
    limit = _vmem_limit(
        [
            _nbytes((tq, d), xq.dtype),
            _nbytes((mem_len, d), xk.dtype),
            _nbytes((mem_len, d), xv.dtype),
            _nbytes((tq, d), F32),
            _nbytes((d, d), wo.dtype),
            _nbytes((tq, d), F32),
        ] + side.window_bytes,
        [_nbytes((tq, d), BF16), _nbytes((tq, d), F32)],
    )
    outs = pl.pallas_call(
        functools.partial(_xattn_kernel, n_side=len(side)),
        grid=grid,
        in_specs=[
            pl.BlockSpec((tq, d), lambda b, i: (b * nq + i, 0)),
            pl.BlockSpec((mem_len, d), lambda b, i: (b, 0)),
            pl.BlockSpec((mem_len, d), lambda b, i: (b, 0)),
            pl.BlockSpec((tq, d), lambda b, i: (b * nq + i, 0)),
            pl.BlockSpec((d, d), lambda b, i: (0, 0)),
        ] + side.in_specs,
        out_specs=[pl.BlockSpec((tq, d), lambda b, i: (b * nq + i, 0))] + side.out_specs,
        out_shape=[jax.ShapeDtypeStruct((batch * seq, d), F32)] + side.out_shapes,
        scratch_shapes=[pltpu.VMEM((tq, d), BF16)],
        compiler_params=pltpu.CompilerParams(
            dimension_semantics=("arbitrary", "arbitrary"), vmem_limit_bytes=limit
        ),
        name="cross_attention",
    )(xq, xk, xv, res, wo, *side_weights)
    return tuple(outs)


def _ffn_kernel(x_ref, g_ref, wg_ref, wu_ref, wd_ref, gf_ref, o_ref, h_ref, *, final_norm):
    f = pl.program_id(1)
    last = pl.num_programs(1) - 1
    blocks = _row_blocks(x_ref.shape[0])

    def partial_ffn(h):
        gate = _dot(h, wg_ref[...])
        up = _dot(h, wu_ref[...])
        act = (gate * jax.nn.sigmoid(gate)) * up
        return _dot(act.astype(BF16), wd_ref[...])

    @pl.when(f == 0)
    def _():
        for rows in blocks:
            x = x_ref[rows, :]
            h = (_rms(x) * g_ref[...]).astype(h_ref.dtype)
            h_ref[rows, :] = h
            o_ref[rows, :] = x + partial_ffn(h)

    if final_norm:
        @pl.when(jnp.logical_and(f > 0, f < last))
        def _():
            o_ref[...] += partial_ffn(h_ref[...])

        @pl.when(f == last)
        def _():
            for rows in blocks:
                y = o_ref[rows, :] + partial_ffn(h_ref[rows, :])
                o_ref[rows, :] = _rms(y) * gf_ref[...]
    else:
        @pl.when(f > 0)
        def _():
            o_ref[...] += partial_ffn(h_ref[...])


def _ffn(x, g, wg, wu, wd, gf, *, final_norm, tm, tf):
    m, d = x.shape
    d_ff = wg.shape[1]
    limit = _vmem_limit(
        [
            _nbytes((tm, d), F32),
            _nbytes((d, tf), wg.dtype),
            _nbytes((d, tf), wu.dtype),
            _nbytes((tf, d), wd.dtype),
            _nbytes((tm, d), F32),
        ],
        [_nbytes((tm, d), BF16), _nbytes((tm, d), F32), 4 * _nbytes((tm, tf), F32)],
    )
    return pl.pallas_call(
        functools.partial(_ffn_kernel, final_norm=final_norm),
        grid=(m // tm, d_ff // tf),
        in_specs=[
            pl.BlockSpec((tm, d), lambda i, f: (i, 0)),
            pl.BlockSpec((1, d), lambda i, f: (0, 0)),
            pl.BlockSpec((d, tf), lambda i, f: (0, f)),
            pl.BlockSpec((d, tf), lambda i, f: (0, f)),
            pl.BlockSpec((tf, d), lambda i, f: (f, 0)),
            pl.BlockSpec((1, d), lambda i, f: (0, 0)),
        ],
        out_specs=pl.BlockSpec((tm, d), lambda i, f: (i, 0)),
        out_shape=jax.ShapeDtypeStruct((m, d), F32),
        scratch_shapes=[pltpu.VMEM((tm, d), BF16)],
        compiler_params=pltpu.CompilerParams(
            dimension_semantics=("parallel", "arbitrary"), vmem_limit_bytes=limit
        ),
        name="swiglu_ffn",
    )(x, g.reshape(1, d), wg, wu, wd, gf.reshape(1, d))


def kernel(x, mem, norm_mix_g, w_in, lambda_q1, lambda_k1, lambda_q2, lambda_k2, da_subln_g, w_o, norm_x_g, norm_mem_g, w_xq, w_xk, w_xv, w_xo, norm_ffn_g, w_gate, w_up, w_down, norm_f_g):
    batch, seq, d_model = x.shape
    mem_len = mem.shape[1]
    depth = w_in.shape[0]
    slopes = jnp.asarray(2.0 ** (-8.0 * np.arange(1, DA_HEADS + 1) / DA_HEADS), dtype=F32)
    log_gammas = jnp.asarray(np.log(1.0 - 2.0 ** (-5.0 - np.arange(RET_HEADS))), dtype=F32)

    xf = x.reshape(batch * seq, d_model)
    memf = mem.reshape(batch * mem_len, d_model)
    for l in range(depth):
        lam_init = 0.8 - 0.6 * math.exp(-0.3 * l)
        n_dq = DA_HEADS * 2 * DA_HEAD_DIM
        col_scale = jnp.where(jnp.arange(w_in.shape[2]) < n_dq, DA_HEAD_DIM ** -0.5 * LOG2_E, 1.0)
        proj, wb_o, wb_xq = _norm_matmul(
            xf, norm_mix_g[l], w_in[l], BF16, tm=ROW_TILE, tn=COL_TILE,
            group_tiles=IN_PROJ_GROUP_TILES, name="in_proj",
            col_scale=col_scale.astype(F32), side_weights=(w_o[l], w_xq[l]))
        da, ret, wb_up, wb_xk, wb_xv, wb_xo = _mixers(
            proj, slopes, lambda_q1[l], lambda_k1[l], lambda_q2[l], lambda_k2[l], da_subln_g[l],
            log_gammas, batch=batch, seq=seq, lam_init=lam_init, tq=ATTN_Q_TILE,
            side_weights=(w_up[l], w_xk[l], w_xv[l], w_xo[l]))
        xf = _matmul2_res(da, ret, wb_o, xf, tm=OUT_PROJ_ROW_TILE, tn=d_model, name="out_proj")

        xq, wb_gate = _norm_matmul(xf, norm_x_g[l], wb_xq, BF16, tm=ROW_TILE, tn=d_model,
                                   name="xattn_q", side_weights=(w_gate[l],))
        xk, xv = _norm_matmul_pair(memf, norm_mem_g[l], wb_xk, wb_xv, BF16, tm=ROW_TILE, tn=COL_TILE,
                                   name="xattn_kv")
        xf, wb_down = _cross_attention(xq, xk, xv, xf, wb_xo, batch=batch, seq=seq, mem_len=mem_len,
                                       tq=XATTN_Q_TILE, side_weights=(w_down[l],))

        xf = _ffn(xf, norm_ffn_g[l], wb_gate, wb_up, wb_down, norm_f_g,
                  final_norm=(l == depth - 1), tm=FFN_ROW_TILE, tf=FFN_COL_TILE)
    return xf.reshape(batch, seq, d_model)
```

```python
import functools
import math

import jax
import jax.numpy as jnp
import numpy as np
from jax import lax
from jax.experimental import pallas as pl
from jax.experimental.pallas import tpu as pltpu

F32 = jnp.float32
BF16 = jnp.bfloat16

DA_HEADS = 4
DA_HEAD_DIM = 128
DA_V_DIM = 2 * DA_HEAD_DIM
RET_HEADS = 4
RET_QK_DIM = 128
RET_V_DIM = 256
XATTN_HEADS = 4
RET_CHUNK = 256
NORM_EPS = 1e-6
NEG_INF = -1e30
LOG2_E = math.log2(math.e)
NORM_BLOCK_ROWS = 256

ROW_TILE = 1024
COL_TILE = 1024
IN_PROJ_GROUP_TILES = 2
OUT_PROJ_ROW_TILE = 512
ATTN_Q_TILE = 512
XATTN_Q_TILE = 512
FFN_ROW_TILE, FFN_COL_TILE = 1024, 512

V7X_LANES = 128
MXU_COLS = 256
BF16_TILE_ROWS = 16
V7X_VMEM_BYTES = 64 * 1024 * 1024
V7X_VMEM_USABLE_BYTES = V7X_VMEM_BYTES - 8 * 1024 * 1024
COMPILER_SCRATCH_BYTES = 4 * 1024 * 1024


def _nbytes(shape, dtype):
    return int(np.prod(shape)) * jnp.dtype(dtype).itemsize


def _vmem_limit(pipelined, resident):
    need = 2 * sum(pipelined) + sum(resident) + COMPILER_SCRATCH_BYTES
    return int(min(V7X_VMEM_USABLE_BYTES, need))


def _rms(x):
    return x * lax.rsqrt(jnp.mean(x * x, axis=-1, keepdims=True) + NORM_EPS)


def _fold_lanes(x, op):
    tiles = [x[:, i:i + V7X_LANES] for i in range(0, x.shape[1], V7X_LANES)]
    return functools.reduce(op, tiles)


def _dot(a, b):
    return jnp.dot(a, b, preferred_element_type=F32)


def _dot_nt(a, b):
    return lax.dot_general(a, b, (((1,), (1,)), ((), ())), preferred_element_type=F32)


def _dot_tn(a, b):
    return lax.dot_general(a, b, (((0,), (0,)), ((), ())), preferred_element_type=F32)


class _SideCasts:
    def __init__(self, weights, grid):
        self.weights = list(weights)
        self.grid = tuple(grid)
        n_steps = int(np.prod(self.grid))
        self.plans = []
        for w in self.weights:
            rows, n_blocks = w.shape[0], n_steps
            while rows % n_blocks or (rows // n_blocks) % BF16_TILE_ROWS:
                n_blocks -= 1
            self.plans.append((n_blocks, rows // n_blocks))

    def __len__(self):
        return len(self.weights)

    def _specs(self):
        specs = []
        for w, (n_blocks, block_rows) in zip(self.weights, self.plans):
            def index(*ids, n_blocks=n_blocks):
                step = ids[0]
                for extent, idx in zip(self.grid[1:], ids[1:]):
                    step = step * extent + idx
                return (jnp.minimum(step, n_blocks - 1), 0)
            specs.append(pl.BlockSpec((block_rows, w.shape[1]), index))
        return specs

    in_specs = property(_specs)
    out_specs = property(_specs)

    @property
    def out_shapes(self):
        return [jax.ShapeDtypeStruct(w.shape, BF16) for w in self.weights]

    @property
    def window_bytes(self):
        return [_nbytes((rows, w.shape[1]), dt)
                for w, (_, rows) in zip(self.weights, self.plans) for dt in (w.dtype, BF16)]


def _split_refs(refs, n_in, n_side):
    ins, rest = refs[:n_in], refs[n_in:]
    side_in, rest = rest[:n_side], rest[n_side:]
    out, side_out, scratch = rest[0], rest[1:1 + n_side], rest[1 + n_side:]
    return ins, out, scratch, list(zip(side_in, side_out))


def _cast_blocks(pairs):
    for src, dst in pairs:
        dst[...] = src[...].astype(dst.dtype)


def _row_blocks(n_rows):
    step = min(NORM_BLOCK_ROWS, n_rows)
    return [slice(r, r + step) for r in range(0, n_rows, step)]


def _col_chunks(n_cols):
    step = min(MXU_COLS, n_cols)
    return [slice(c, c + step) for c in range(0, n_cols, step)]


def _norm_matmul_kernel(*refs, n_side, cast_w):
    ins, o_ref, scratch, side = _split_refs(refs, 4 if cast_w else 3, n_side)
    x_ref, g_ref, w_ref = ins[:3]
    h_ref = scratch[0]
    wb_ref = scratch[1] if cast_w else w_ref
    n, t = pl.program_id(1), pl.program_id(2)

    def cast_weight(cols):
        wb_ref[:, cols] = (w_ref[:, cols] * ins[3][:, cols]).astype(wb_ref.dtype)

    @pl.when(n == 0)
    def _():
        _cast_blocks(side)
        if cast_w:
            @pl.when(t == 0)
            def _():
                cast_weight(slice(None))
        for rows in _row_blocks(x_ref.shape[0]):
            h = (_rms(x_ref[rows, :]) * g_ref[...]).astype(h_ref.dtype)
            h_ref[t, rows, :] = h
            o_ref[rows, :] = _dot(h, wb_ref[...]).astype(o_ref.dtype)

    @pl.when(n > 0)
    def _():
        _cast_blocks(side)
        if cast_w:
            @pl.when(t == 0)
            def _():
                for cols in _col_chunks(w_ref.shape[1]):
                    cast_weight(cols)
                    o_ref[:, cols] = _dot(h_ref[t], wb_ref[:, cols]).astype(o_ref.dtype)

            @pl.when(t > 0)
            def _():
                o_ref[...] = _dot(h_ref[t], wb_ref[...]).astype(o_ref.dtype)
        else:
            o_ref[...] = _dot(h_ref[t], w_ref[...]).astype(o_ref.dtype)


def _norm_matmul(x, g, w, out_dtype, *, tm, tn, name, group_tiles=1, col_scale=None,
                 side_weights=()):
    m, d = x.shape
    n = w.shape[1]
    tm, tn = min(tm, m), min(tn, n)
    gt = group_tiles
    grid = (m // (tm * gt), n // tn, gt)
    side = _SideCasts(side_weights, grid)
    cast_w = w.dtype != BF16
    assert cast_w or col_scale is None
    weight_ins, weight_specs = [w], [pl.BlockSpec((d, tn), lambda gi, j, t: (0, j))]
    if cast_w:
        cs = jnp.ones((n,), F32) if col_scale is None else col_scale
        weight_ins.append(cs.reshape(1, n))
        weight_specs.append(pl.BlockSpec((1, tn), lambda gi, j, t: (0, j)))
    scratch = [pltpu.VMEM((gt, tm, d), BF16)] + ([pltpu.VMEM((d, tn), BF16)] if cast_w else [])
    limit = _vmem_limit(
        [_nbytes((tm, d), x.dtype), _nbytes((d, tn), w.dtype), _nbytes((tm, tn), out_dtype)]
        + side.window_bytes,
        [_nbytes((gt, tm, d), BF16), _nbytes((d, tn), BF16) * cast_w],
    )
    outs = pl.pallas_call(
        functools.partial(_norm_matmul_kernel, n_side=len(side), cast_w=cast_w),
        grid=grid,
        in_specs=[
            pl.BlockSpec((tm, d), lambda gi, j, t: (gi * gt + jnp.where(j == 0, t, gt - 1), 0)),
            pl.BlockSpec((1, d), lambda gi, j, t: (0, 0)),
        ] + weight_specs + side.in_specs,
        out_specs=[pl.BlockSpec((tm, tn), lambda gi, j, t: (gi * gt + t, j))] + side.out_specs,
        out_shape=[jax.ShapeDtypeStruct((m, n), out_dtype)] + side.out_shapes,
        scratch_shapes=scratch,
        compiler_params=pltpu.CompilerParams(
            dimension_semantics=("arbitrary", "arbitrary", "arbitrary"), vmem_limit_bytes=limit
        ),
        name=name,
    )(x, g.reshape(1, d), *weight_ins, *side_weights)
    return tuple(outs)


def _norm_matmul_pair_kernel(x_ref, g_ref, w1_ref, w2_ref, o1_ref, o2_ref, h_ref):
    j = pl.program_id(1)

    @pl.when(j == 0)
    def _():
        for rows in _row_blocks(x_ref.shape[0]):
            h = (_rms(x_ref[rows, :]) * g_ref[...]).astype(h_ref.dtype)
            h_ref[rows, :] = h
            o1_ref[rows, :] = _dot(h, w1_ref[...]).astype(o1_ref.dtype)
            o2_ref[rows, :] = _dot(h, w2_ref[...]).astype(o2_ref.dtype)

    @pl.when(j > 0)
    def _():
        o1_ref[...] = _dot(h_ref[...], w1_ref[...]).astype(o1_ref.dtype)
        o2_ref[...] = _dot(h_ref[...], w2_ref[...]).astype(o2_ref.dtype)


def _norm_matmul_pair(x, g, w1, w2, out_dtype, *, tm, tn, name):
    m, d = x.shape
    n = w1.shape[1]
    assert w1.shape == w2.shape
    tm, tn = min(tm, m), min(tn, n)
    w_spec = pl.BlockSpec((d, tn), lambda i, j: (0, j))
    o_spec = pl.BlockSpec((tm, tn), lambda i, j: (i, j))
    limit = _vmem_limit(
        [_nbytes((tm, d), x.dtype)] + [_nbytes((d, tn), w1.dtype), _nbytes((tm, tn), out_dtype)] * 2,
        [_nbytes((tm, d), BF16), _nbytes((tm, d), F32)],
    )
    return pl.pallas_call(
        _norm_matmul_pair_kernel,
        grid=(m // tm, n // tn),
        in_specs=[pl.BlockSpec((tm, d), lambda i, j: (i, 0)), pl.BlockSpec((1, d), lambda i, j: (0, 0)),
                  w_spec, w_spec],
        out_specs=[o_spec, o_spec],
        out_shape=[jax.ShapeDtypeStruct((m, n), out_dtype)] * 2,
        scratch_shapes=[pltpu.VMEM((tm, d), BF16)],
        compiler_params=pltpu.CompilerParams(
            dimension_semantics=("arbitrary", "arbitrary"), vmem_limit_bytes=limit
        ),
        name=name,
    )(x, g.reshape(1, d), w1, w2)


def _matmul2_res_kernel(a1_ref, a2_ref, w1_ref, w2_ref, res_ref, o_ref):
    acc = _dot(a1_ref[...], w1_ref[...]) + _dot(a2_ref[...], w2_ref[...])
    o_ref[...] = res_ref[...] + acc


def _matmul2_res(a1, a2, w, res, *, tm, tn, name):
    m, k1 = a1.shape
    k2 = a2.shape[1]
    assert k1 == k2 and w.shape[0] == k1 + k2
    n = w.shape[1]
    limit = _vmem_limit(
        [
            _nbytes((tm, k1), a1.dtype),
            _nbytes((tm, k2), a2.dtype),
            _nbytes((k1, tn), w.dtype),
            _nbytes((k2, tn), w.dtype),
            _nbytes((tm, tn), F32),
            _nbytes((tm, tn), F32),
        ],
        [_nbytes((tm, tn), F32)],
    )
    return pl.pallas_call(
        _matmul2_res_kernel,
        grid=(m // tm, n // tn),
        in_specs=[
            pl.BlockSpec((tm, k1), lambda i, j: (i, 0)),
            pl.BlockSpec((tm, k2), lambda i, j: (i, 0)),
            pl.BlockSpec((k1, tn), lambda i, j: (0, j)),
            pl.BlockSpec((k2, tn), lambda i, j: (1, j)),
            pl.BlockSpec((tm, tn), lambda i, j: (i, j)),
        ],
        out_specs=pl.BlockSpec((tm, tn), lambda i, j: (i, j)),
        out_shape=jax.ShapeDtypeStruct((m, n), F32),
        compiler_params=pltpu.CompilerParams(
            dimension_semantics=("parallel", "arbitrary"), vmem_limit_bytes=limit
        ),
        name=name,
    )(a1, a2, w, w, res)


def _bf16_part(x):
    bits = lax.bitcast_convert_type(x, jnp.int32) & jnp.int32(-65536)
    return lax.bitcast_convert_type(bits, F32)


def _diff_attn_body(ins, o_ref, scratch, *, tq, lam_init):
    slope_ref, lq1_ref, lk1_ref, lq2_ref, lk2_ref, g_ref, q_ref, k_ref, v_ref = ins
    kx_ref, s_ref, p_ref = scratch
    s_len = q_ref.shape[0]
    d = DA_HEAD_DIM
    lam = (jnp.exp(jnp.sum(lq1_ref[...] * lk1_ref[...], axis=-1, keepdims=True))
           - jnp.exp(jnp.sum(lq2_ref[...] * lk2_ref[...], axis=-1, keepdims=True))
           + lam_init)

    lane = lax.broadcasted_iota(jnp.int32, (1, d), 1)
    slope2 = jnp.full((1, d), slope_ref[pl.program_id(1)] * LOG2_E, F32)
    piece_hi = _bf16_part(slope2)
    rest = slope2 - piece_hi
    piece_mid = _bf16_part(rest)
    piece_lo = _bf16_part(rest - piece_mid)
    piece = jnp.where((lane == 0) | (lane == 3), piece_hi,
                      jnp.where((lane == 1) | (lane == 4), piece_mid, piece_lo))
    q_extra = jnp.where(lane < 3, piece * 256.0, jnp.where(lane < 6, piece, 0.0))
    q_extra = jnp.broadcast_to(q_extra, (tq, d)).astype(BF16)
    kpos = lax.broadcasted_iota(jnp.int32, (s_len, d), 0)
    klane = lax.broadcasted_iota(jnp.int32, (s_len, d), 1)
    k_extra = jnp.where(klane < 3, kpos >> 8, jnp.where(klane < 6, kpos & 255, 0))
    kx_ref[...] = k_extra.astype(F32).astype(kx_ref.dtype)

    row = lax.broadcasted_iota(jnp.int32, (tq, tq), 0)
    col = lax.broadcasted_iota(jnp.int32, (tq, tq), 1)
    causal = col <= row

    for qi in reversed(range(s_len // tq)):
        lo, hi = qi * tq, (qi + 1) * tq
        key_blocks = [slice(j * tq, (j + 1) * tq) for j in range(qi + 1)]
        heads = []
        for c in range(2):
            dcols = slice(c * d, (c + 1) * d)
            q_aug = jnp.concatenate([q_ref[lo:hi, dcols], q_extra], axis=1)
            m = None
            for j, cols in enumerate(key_blocks):
                k_aug = jnp.concatenate([k_ref[cols, dcols], kx_ref[cols, :]], axis=1)
                s = _dot_nt(q_aug, k_aug)
                if j == qi:
                    s = jnp.where(causal, s, NEG_INF)
                s_ref[c, :, cols] = s
                bm = _fold_lanes(s, jnp.maximum)
                m = bm if m is None else jnp.maximum(m, bm)
            m = jnp.max(m, axis=-1, keepdims=True)
            l = None
            for cols in key_blocks:
                p = jnp.exp2(s_ref[c, :, cols] - m)
                bl = _fold_lanes(p, jnp.add)
                l = bl if l is None else l + bl
                p_ref[c, :, cols] = p.astype(p_ref.dtype)
            l = jnp.sum(l, axis=-1, keepdims=True)
            acc = _dot(p_ref[c, :, 0:hi], v_ref[0:hi, :])
            heads.append((acc, l))
        (acc1, l1), (acc2, l2) = heads
        out = acc1 * (1.0 / l1) - acc2 * (lam / l2)
        y = _rms(out) * g_ref[...] * (1.0 - lam_init)
        o_ref[lo:hi, :] = y.astype(o_ref.dtype)
        yield


def _retention_body(ins, o_ref):
    lg_ref, q_ref, k_ref, v_ref, gate_ref = ins
    s_len = q_ref.shape[0]
    c = RET_CHUNK
    dk, dv = RET_QK_DIM, RET_V_DIM
    scale = dk ** -0.5
    lg = lg_ref[pl.program_id(1)]

    row = lax.broadcasted_iota(jnp.int32, (c, c), 0)
    col = lax.broadcasted_iota(jnp.int32, (c, c), 1)
    diff = (row - col).astype(F32)
    intra = jnp.where(diff >= 0, jnp.exp(lg * jnp.maximum(diff, 0.0)), 0.0) * scale
    row_k = lax.broadcasted_iota(jnp.int32, (c, dk), 0).astype(F32)
    k_decay = jnp.exp(lg * (float(c - 1) - row_k)) * scale
    row_v = lax.broadcasted_iota(jnp.int32, (c, dv), 0).astype(F32)
    q_decay = jnp.exp(lg * (row_v + 1.0))
    chunk_decay = jnp.exp(jnp.full((1, dv), lg * float(c), F32))

    state = jnp.zeros((dk, dv), F32)
    for i in range(s_len // c):
        lo, hi = i * c, (i + 1) * c
        q = q_ref[lo:hi, :]
        k = k_ref[lo:hi, :]
        v = v_ref[lo:hi, :]
        scores = _dot_nt(q, k) * intra
        y = _dot(scores.astype(BF16), v)
        if i > 0:
            y = y + _dot(q, state.astype(BF16)) * q_decay
        if i + 1 < s_len // c:
            kd = (k.astype(F32) * k_decay).astype(BF16)
            state = state * chunk_decay + _dot_tn(kd, v)
        gate = gate_ref[lo:hi, :].astype(F32)
        o_ref[lo:hi, :] = (_rms(y) * (gate * jax.nn.sigmoid(gate))).astype(o_ref.dtype)
        yield


N_ATTN_INS, N_RET_INS = 9, 5


def _mixers_kernel(*refs, n_side, tq, lam_init):
    ins, rest = refs[:N_ATTN_INS + N_RET_INS], refs[N_ATTN_INS + N_RET_INS:]
    side_in, rest = rest[:n_side], rest[n_side:]
    da_ref, ret_ref = rest[:2]
    side_out, scratch = rest[2:2 + n_side], rest[2 + n_side:]
    _cast_blocks(zip(side_in, side_out))
    pending = [_retention_body(ins[N_ATTN_INS:], ret_ref),
               _diff_attn_body(ins[:N_ATTN_INS], da_ref, scratch, tq=tq, lam_init=lam_init)]
    while pending:
        pending = [body for body in pending if next(body, StopIteration) is not StopIteration]


def _mixers(proj, slopes, lq1, lk1, lq2, lk2, g, log_gammas, *, batch, seq, lam_init, tq,
            side_weights=()):
    assert DA_HEADS == RET_HEADS
    hb = DA_V_DIM
    q_blk0, k_blk0, v_blk0 = 0, DA_HEADS, 2 * DA_HEADS
    da_cols = 3 * DA_HEADS * DA_V_DIM
    rq_blk0 = da_cols // RET_QK_DIM
    rk_blk0 = rq_blk0 + RET_HEADS
    rv_blk0 = (da_cols + 2 * RET_HEADS * RET_QK_DIM) // RET_V_DIM
    rg_blk0 = rv_blk0 + RET_HEADS
    grid = (batch, DA_HEADS)
    side = _SideCasts(side_weights, grid)
    scratch = [
        pltpu.VMEM((seq, DA_HEAD_DIM), BF16),
        pltpu.VMEM((2, tq, seq), F32),
        pltpu.VMEM((2, tq, seq), BF16),
    ]
    limit = _vmem_limit(
        [_nbytes((seq, hb), proj.dtype)] * 4
        + [_nbytes((seq, RET_QK_DIM), proj.dtype)] * 2 + [_nbytes((seq, RET_V_DIM), proj.dtype)] * 3
        + side.window_bytes,
        [_nbytes((seq, DA_HEAD_DIM), BF16),
         3 * _nbytes((2, tq, seq), F32), 3 * _nbytes((2, tq, seq), BF16)],
    )
    vec = pl.BlockSpec((1, DA_HEAD_DIM), lambda b, h: (0, 0))
    smem = pl.BlockSpec(memory_space=pltpu.SMEM)

    def head_cols(width, blk0):
        return pl.BlockSpec((seq, width), lambda b, h: (b, blk0 + h))

    outs = pl.pallas_call(
        functools.partial(_mixers_kernel, n_side=len(side), tq=tq, lam_init=lam_init),
        grid=grid,
        in_specs=[
            smem, vec, vec, vec, vec,
            pl.BlockSpec((1, hb), lambda b, h: (0, 0)),
            head_cols(hb, q_blk0), head_cols(hb, k_blk0), head_cols(hb, v_blk0),
            smem,
            head_cols(RET_QK_DIM, rq_blk0), head_cols(RET_QK_DIM, rk_blk0),
            head_cols(RET_V_DIM, rv_blk0), head_cols(RET_V_DIM, rg_blk0),
        ] + side.in_specs,
        out_specs=[head_cols(hb, 0), head_cols(RET_V_DIM, 0)] + side.out_specs,
        out_shape=[jax.ShapeDtypeStruct((batch * seq, DA_HEADS * hb), BF16),
                   jax.ShapeDtypeStruct((batch * seq, RET_HEADS * RET_V_DIM), BF16)] + side.out_shapes,
        scratch_shapes=scratch,
        compiler_params=pltpu.CompilerParams(
            dimension_semantics=("arbitrary", "arbitrary"), vmem_limit_bytes=limit
        ),
        name="token_mixers",
    )(slopes, lq1.reshape(1, -1), lk1.reshape(1, -1), lq2.reshape(1, -1), lk2.reshape(1, -1),
      g.reshape(1, hb), proj, proj, proj, log_gammas, proj, proj, proj, proj, *side_weights)
    return tuple(outs)


def _xattn_kernel(*refs, n_side):
    ins, o_ref, (xo_ref,), side = _split_refs(refs, 5, n_side)
    xq_ref, xk_ref, xv_ref, res_ref, wo_ref = ins
    _cast_blocks(side)
    d_model = xq_ref.shape[1]
    hd = d_model // XATTN_HEADS
    scale2 = hd ** -0.5 * LOG2_E
    for h in range(XATTN_HEADS):
        cols = slice(h * hd, (h + 1) * hd)
        s = _dot_nt(xq_ref[:, cols], xk_ref[:, cols]) * scale2
        p = jnp.exp2(s - jnp.max(s, axis=-1, keepdims=True))
        p = p * (1.0 / jnp.sum(p, axis=-1, keepdims=True))
        xo_ref[:, cols] = _dot(p.astype(BF16), xv_ref[:, cols]).astype(xo_ref.dtype)
    o_ref[...] = res_ref[...] + _dot(xo_ref[...], wo_ref[...])


def _cross_attention(xq, xk, xv, res, wo, *, batch, seq, mem_len, tq, side_weights=()):
    d = xq.shape[1]
    nq = seq // tq
    grid = (batch, nq)
    side = _SideCasts(side_weights, grid)
    limit = _vmem_limit(
        [
            _nbytes((tq, d), xq.dtype),
            _nbytes((mem_len, d), xk.dtype),
            _nbytes((mem_len, d), xv.dtype),
            _nbytes((tq, d), F32),
            _nbytes((d, d), wo.dtype),
            _nbytes((tq, d), F32),
        ] + side.window_bytes,
        [_nbytes((tq, d), BF16), _nbytes((tq, d), F32)],
    )
    outs = pl.pallas_call(
        functools.partial(_xattn_kernel, n_side=len(side)),
        grid=grid,
        in_specs=[
            pl.BlockSpec((tq, d), lambda b, i: (b * nq + i, 0)),
            pl.BlockSpec((mem_len, d), lambda b, i: (b, 0)),
            pl.BlockSpec((mem_len, d), lambda b, i: (b, 0)),
            pl.BlockSpec((tq, d), lambda b, i: (b * nq + i, 0)),
            pl.BlockSpec((d, d), lambda b, i: (0, 0)),
        ] + side.in_specs,
        out_specs=[pl.BlockSpec((tq, d), lambda b, i: (b * nq + i, 0))] + side.out_specs,
        out_shape=[jax.ShapeDtypeStruct((batch * seq, d), F32)] + side.out_shapes,
        scratch_shapes=[pltpu.VMEM((tq, d), BF16)],
        compiler_params=pltpu.CompilerParams(
            dimension_semantics=("arbitrary", "arbitrary"), vmem_limit_bytes=limit
        ),
        name="cross_attention",
    )(xq, xk, xv, res, wo, *side_weights)
    return tuple(outs)


def _ffn_kernel(x_ref, g_ref, wg_ref, wu_ref, wd_ref, gf_ref, o_ref, h_ref, *, final_norm):
    f = pl.program_id(1)
    last = pl.num_programs(1) - 1
    blocks = _row_blocks(x_ref.shape[0])

    def partial_ffn(h):
        gate = _dot(h, wg_ref[...])
        up = _dot(h, wu_ref[...])
        act = (gate * jax.nn.sigmoid(gate)) * up
        return _dot(act.astype(BF16), wd_ref[...])

    @pl.when(f == 0)
    def _():
        for rows in blocks:
            x = x_ref[rows, :]
            h = (_rms(x) * g_ref[...]).astype(h_ref.dtype)
            h_ref[rows, :] = h
            o_ref[rows, :] = x + partial_ffn(h)

    if final_norm:
        @pl.when(jnp.logical_and(f > 0, f < last))
        def _():
            o_ref[...] += partial_ffn(h_ref[...])

        @pl.when(f == last)
        def _():
            for rows in blocks:
                y = o_ref[rows, :] + partial_ffn(h_ref[rows, :])
                o_ref[rows, :] = _rms(y) * gf_ref[...]
    else:
        @pl.when(f > 0)
        def _():
            o_ref[...] += partial_ffn(h_ref[...])


def _ffn(x, g, wg, wu, wd, gf, *, final_norm, tm, tf):
    m, d = x.shape
    d_ff = wg.shape[1]
    limit = _vmem_limit(
        [
            _nbytes((tm, d), F32),
            _nbytes((d, tf), wg.dtype),
            _nbytes((d, tf), wu.dtype),
            _nbytes((tf, d), wd.dtype),
            _nbytes((tm, d), F32),
        ],
        [_nbytes((tm, d), BF16), _nbytes((tm, d), F32), 4 * _nbytes((tm, tf), F32)],
    )
    return pl.pallas_call(
        functools.partial(_ffn_kernel, final_norm=final_norm),
        grid=(m // tm, d_ff // tf),
        in_specs=[
            pl.BlockSpec((tm, d), lambda i, f: (i, 0)),
            pl.BlockSpec((1, d), lambda i, f: (0, 0)),
            pl.BlockSpec((d, tf), lambda i, f: (0, f)),
            pl.BlockSpec((d, tf), lambda i, f: (0, f)),
            pl.BlockSpec((tf, d), lambda i, f: (f, 0)),
            pl.BlockSpec((1, d), lambda i, f: (0, 0)),
        ],
        out_specs=pl.BlockSpec((tm, d), lambda i, f: (i, 0)),
        out_shape=jax.ShapeDtypeStruct((m, d), F32),
        scratch_shapes=[pltpu.VMEM((tm, d), BF16)],
        compiler_params=pltpu.CompilerParams(
            dimension_semantics=("parallel", "arbitrary"), vmem_limit_bytes=limit
        ),
        name="swiglu_ffn",
    )(x, g.reshape(1, d), wg, wu, wd, gf.reshape(1, d))


def kernel(x, mem, norm_mix_g, w_in, lambda_q1, lambda_k1, lambda_q2, lambda_k2, da_subln_g, w_o, norm_x_g, norm_mem_g, w_xq, w_xk, w_xv, w_xo, norm_ffn_g, w_gate, w_up, w_down, norm_f_g):
    batch, seq, d_model = x.shape
    mem_len = mem.shape[1]
    depth = w_in.shape[0]
    slopes = jnp.asarray(2.0 ** (-8.0 * np.arange(1, DA_HEADS + 1) / DA_HEADS), dtype=F32)
    log_gammas = jnp.asarray(np.log(1.0 - 2.0 ** (-5.0 - np.arange(RET_HEADS))), dtype=F32)

    xf = x.reshape(batch * seq, d_model)
    memf = mem.reshape(batch * mem_len, d_model)
    for l in range(depth):
        lam_init = 0.8 - 0.6 * math.exp(-0.3 * l)
        n_dq = DA_HEADS * 2 * DA_HEAD_DIM
        col_scale = jnp.where(jnp.arange(w_in.shape[2]) < n_dq, DA_HEAD_DIM ** -0.5 * LOG2_E, 1.0)
        proj, wb_o, wb_xq = _norm_matmul(
            xf, norm_mix_g[l], w_in[l], BF16, tm=ROW_TILE, tn=COL_TILE,
            group_tiles=IN_PROJ_GROUP_TILES, name="in_proj",
            col_scale=col_scale.astype(F32), side_weights=(w_o[l], w_xq[l]))
        da, ret, wb_up, wb_xk, wb_xv, wb_xo = _mixers(
            proj, slopes, lambda_q1[l], lambda_k1[l], lambda_q2[l], lambda_k2[l], da_subln_g[l],
            log_gammas, batch=batch, seq=seq, lam_init=lam_init, tq=ATTN_Q_TILE,
            side_weights=(w_up[l], w_xk[l], w_xv[l], w_xo[l]))
        xf = _matmul2_res(da, ret, wb_o, xf, tm=OUT_PROJ_ROW_TILE, tn=d_model, name="out_proj")

        xq, wb_gate = _norm_matmul(xf, norm_x_g[l], wb_xq, BF16, tm=ROW_TILE, tn=d_model,
                                   name="xattn_q", side_weights=(w_gate[l],))
        xk, xv = _norm_matmul_pair(memf, norm_mem_g[l], wb_xk, wb_xv, BF16, tm=ROW_TILE, tn=COL_TILE,
                                   name="xattn_kv")
        xf, wb_down = _cross_attention(xq, xk, xv, xf, wb_xo, batch=batch, seq=seq, mem_len=mem_len,
                                       tq=XATTN_Q_TILE, side_weights=(w_down[l],))

        xf = _ffn(xf, norm_ffn_g[l], wb_gate, wb_up, wb_down, norm_f_g,
                  final_norm=(l == depth - 1), tm=FFN_ROW_TILE, tf=FFN_COL_TILE)
    return xf.reshape(batch, seq, d_model)
```

```python
import functools
import math

import jax
import jax.numpy as jnp
import numpy as np
from jax import lax
from jax.experimental import pallas as pl
from jax.experimental.pallas import tpu as pltpu

F32 = jnp.float32
BF16 = jnp.bfloat16

DA_HEADS = 4
DA_HEAD_DIM = 128
DA_V_DIM = 2 * DA_HEAD_DIM
RET_HEADS = 4
RET_QK_DIM = 128
RET_V_DIM = 256
XATTN_HEADS = 4
RET_CHUNK = 256
NORM_EPS = 1e-6
NEG_INF = -1e30
LOG2_E = math.log2(math.e)
NORM_BLOCK_ROWS = 256

ROW_TILE = 1024
COL_TILE = 1024
IN_PROJ_GROUP_TILES = 2
OUT_PROJ_ROW_TILE = 512
ATTN_Q_TILE = 512
XATTN_Q_TILE = 512
FFN_ROW_TILE, FFN_COL_TILE = 1024, 512

V7X_LANES = 128
MXU_COLS = 256
BF16_TILE_ROWS = 16
V7X_VMEM_BYTES = 64 * 1024 * 1024
V7X_VMEM_USABLE_BYTES = V7X_VMEM_BYTES - 8 * 1024 * 1024
COMPILER_SCRATCH_BYTES = 4 * 1024 * 1024


def _nbytes(shape, dtype):
    return int(np.prod(shape)) * jnp.dtype(dtype).itemsize


def _vmem_limit(pipelined, resident):
    need = 2 * sum(pipelined) + sum(resident) + COMPILER_SCRATCH_BYTES
    return int(min(V7X_VMEM_USABLE_BYTES, need))


def _rms(x):
    return x * lax.rsqrt(jnp.mean(x * x, axis=-1, keepdims=True) + NORM_EPS)


def _fold_lanes(x, op):
    tiles = [x[:, i:i + V7X_LANES] for i in range(0, x.shape[1], V7X_LANES)]
    return functools.reduce(op, tiles)


def _dot(a, b):
    return jnp.dot(a, b, preferred_element_type=F32)


def _dot_nt(a, b):
    return lax.dot_general(a, b, (((1,), (1,)), ((), ())), preferred_element_type=F32)


def _dot_tn(a, b):
    return lax.dot_general(a, b, (((0,), (0,)), ((), ())), preferred_element_type=F32)


class _SideCasts:
    def __init__(self, weights, grid):
        self.weights = list(weights)
        self.grid = tuple(grid)
        n_steps = int(np.prod(self.grid))
        self.plans = []
        for w in self.weights:
            rows, n_blocks = w.shape[0], n_steps
            while rows % n_blocks or (rows // n_blocks) % BF16_TILE_ROWS:
                n_blocks -= 1
            self.plans.append((n_blocks, rows // n_blocks))

    def __len__(self):
        return len(self.weights)

    def _specs(self):
        specs = []
        for w, (n_blocks, block_rows) in zip(self.weights, self.plans):
            def index(*ids, n_blocks=n_blocks):
                step = ids[0]
                for extent, idx in zip(self.grid[1:], ids[1:]):
                    step = step * extent + idx
                return (jnp.minimum(step, n_blocks - 1), 0)
            specs.append(pl.BlockSpec((block_rows, w.shape[1]), index))
        return specs

    in_specs = property(_specs)
    out_specs = property(_specs)

    @property
    def out_shapes(self):
        return [jax.ShapeDtypeStruct(w.shape, BF16) for w in self.weights]

    @property
    def window_bytes(self):
        return [_nbytes((rows, w.shape[1]), dt)
                for w, (_, rows) in zip(self.weights, self.plans) for dt in (w.dtype, BF16)]


def _split_refs(refs, n_in, n_side):
    ins, rest = refs[:n_in], refs[n_in:]
    side_in, rest = rest[:n_side], rest[n_side:]
    out, side_out, scratch = rest[0], rest[1:1 + n_side], rest[1 + n_side:]
    return ins, out, scratch, list(zip(side_in, side_out))


def _cast_blocks(pairs):
    for src, dst in pairs:
        dst[...] = src[...].astype(dst.dtype)


def _row_blocks(n_rows):
    step = min(NORM_BLOCK_ROWS, n_rows)
    return [slice(r, r + step) for r in range(0, n_rows, step)]


def _col_chunks(n_cols):
    step = min(MXU_COLS, n_cols)
    return [slice(c, c + step) for c in range(0, n_cols, step)]


def _norm_matmul_kernel(*refs, n_side, cast_w):
    ins, o_ref, scratch, side = _split_refs(refs, 4 if cast_w else 3, n_side)
    x_ref, g_ref, w_ref = ins[:3]
    h_ref = scratch[0]
    wb_ref = scratch[1] if cast_w else w_ref
    n, t = pl.program_id(1), pl.program_id(2)

    def cast_weight(cols):
        wb_ref[:, cols] = (w_ref[:, cols] * ins[3][:, cols]).astype(wb_ref.dtype)

    @pl.when(n == 0)
    def _():
        _cast_blocks(side)
        if cast_w:
            @pl.when(t == 0)
            def _():
                cast_weight(slice(None))
        for rows in _row_blocks(x_ref.shape[0]):
            h = (_rms(x_ref[rows, :]) * g_ref[...]).astype(h_ref.dtype)
            h_ref[t, rows, :] = h
            o_ref[rows, :] = _dot(h, wb_ref[...]).astype(o_ref.dtype)

    @pl.when(n > 0)
    def _():
        _cast_blocks(side)
        if cast_w:
            @pl.when(t == 0)
            def _():
                for cols in _col_chunks(w_ref.shape[1]):
                    cast_weight(cols)
                    o_ref[:, cols] = _dot(h_ref[t], wb_ref[:, cols]).astype(o_ref.dtype)

            @pl.when(t > 0)
            def _():
                o_ref[...] = _dot(h_ref[t], wb_ref[...]).astype(o_ref.dtype)
        else:
            o_ref[...] = _dot(h_ref[t], w_ref[...]).astype(o_ref.dtype)


def _norm_matmul(x, g, w, out_dtype, *, tm, tn, name, group_tiles=1, col_scale=None,
                 side_weights=()):
    m, d = x.shape
    n = w.shape[1]
    tm, tn = min(tm, m), min(tn, n)
    gt = group_tiles
    grid = (m // (tm * gt), n // tn, gt)
    side = _SideCasts(side_weights, grid)
    cast_w = w.dtype != BF16
    assert cast_w or col_scale is None
    weight_ins, weight_specs = [w], [pl.BlockSpec((d, tn), lambda gi, j, t: (0, j))]
    if cast_w:
        cs = jnp.ones((n,), F32) if col_scale is None else col_scale
        weight_ins.append(cs.reshape(1, n))
        weight_specs.append(pl.BlockSpec((1, tn), lambda gi, j, t: (0, j)))
    scratch = [pltpu.VMEM((gt, tm, d), BF16)] + ([pltpu.VMEM((d, tn), BF16)] if cast_w else [])
    limit = _vmem_limit(
        [_nbytes((tm, d), x.dtype), _nbytes((d, tn), w.dtype), _nbytes((tm, tn), out_dtype)]
        + side.window_bytes,
        [_nbytes((gt, tm, d), BF16), _nbytes((d, tn), BF16) * cast_w],
    )
    outs = pl.pallas_call(
        functools.partial(_norm_matmul_kernel, n_side=len(side), cast_w=cast_w),
        grid=grid,
        in_specs=[
            pl.BlockSpec((tm, d), lambda gi, j, t: (gi * gt + jnp.where(j == 0, t, gt - 1), 0)),
            pl.BlockSpec((1, d), lambda gi, j, t: (0, 0)),
        ] + weight_specs + side.in_specs,
        out_specs=[pl.BlockSpec((tm, tn), lambda gi, j, t: (gi * gt + t, j))] + side.out_specs,
        out_shape=[jax.ShapeDtypeStruct((m, n), out_dtype)] + side.out_shapes,
        scratch_shapes=scratch,
        compiler_params=pltpu.CompilerParams(
            dimension_semantics=("arbitrary", "arbitrary", "arbitrary"), vmem_limit_bytes=limit
        ),
        name=name,
    )(x, g.reshape(1, d), *weight_ins, *side_weights)
    return tuple(outs)


def _norm_matmul_pair_kernel(x_ref, g_ref, w1_ref, w2_ref, o1_ref, o2_ref, h_ref):
    j = pl.program_id(1)

    @pl.when(j == 0)
    def _():
        for rows in _row_blocks(x_ref.shape[0]):
            h = (_rms(x_ref[rows, :]) * g_ref[...]).astype(h_ref.dtype)
            h_ref[rows, :] = h
            o1_ref[rows, :] = _dot(h, w1_ref[...]).astype(o1_ref.dtype)
            o2_ref[rows, :] = _dot(h, w2_ref[...]).astype(o2_ref.dtype)

    @pl.when(j > 0)
    def _():
        o1_ref[...] = _dot(h_ref[...], w1_ref[...]).astype(o1_ref.dtype)
        o2_ref[...] = _dot(h_ref[...], w2_ref[...]).astype(o2_ref.dtype)


def _norm_matmul_pair(x, g, w1, w2, out_dtype, *, tm, tn, name):
    m, d = x.shape
    n = w1.shape[1]
    assert w1.shape == w2.shape
    tm, tn = min(tm, m), min(tn, n)
    w_spec = pl.BlockSpec((d, tn), lambda i, j: (0, j))
    o_spec = pl.BlockSpec((tm, tn), lambda i, j: (i, j))
    limit = _vmem_limit(
        [_nbytes((tm, d), x.dtype)] + [_nbytes((d, tn), w1.dtype), _nbytes((tm, tn), out_dtype)] * 2,
        [_nbytes((tm, d), BF16), _nbytes((tm, d), F32)],
    )
    return pl.pallas_call(
        _norm_matmul_pair_kernel,
        grid=(m // tm, n // tn),
        in_specs=[pl.BlockSpec((tm, d), lambda i, j: (i, 0)), pl.BlockSpec((1, d), lambda i, j: (0, 0)),
                  w_spec, w_spec],
        out_specs=[o_spec, o_spec],
        out_shape=[jax.ShapeDtypeStruct((m, n), out_dtype)] * 2,
        scratch_shapes=[pltpu.VMEM((tm, d), BF16)],
        compiler_params=pltpu.CompilerParams(
            dimension_semantics=("arbitrary", "arbitrary"), vmem_limit_bytes=limit
        ),
        name=name,
    )(x, g.reshape(1, d), w1, w2)


def _matmul2_res_kernel(a1_ref, a2_ref, w1_ref, w2_ref, res_ref, o_ref):
    acc = _dot(a1_ref[...], w1_ref[...]) + _dot(a2_ref[...], w2_ref[...])
    o_ref[...] = res_ref[...] + acc


def _matmul2_res(a1, a2, w, res, *, tm, tn, name):
    m, k1 = a1.shape
    k2 = a2.shape[1]
    assert k1 == k2 and w.shape[0] == k1 + k2
    n = w.shape[1]
    limit = _vmem_limit(
        [
            _nbytes((tm, k1), a1.dtype),
            _nbytes((tm, k2), a2.dtype),
            _nbytes((k1, tn), w.dtype),
            _nbytes((k2, tn), w.dtype),
            _nbytes((tm, tn), F32),
            _nbytes((tm, tn), F32),
        ],
        [_nbytes((tm, tn), F32)],
    )
    return pl.pallas_call(
        _matmul2_res_kernel,
        grid=(m // tm, n // tn),
        in_specs=[
            pl.BlockSpec((tm, k1), lambda i, j: (i, 0)),
            pl.BlockSpec((tm, k2), lambda i, j: (i, 0)),
            pl.BlockSpec((k1, tn), lambda i, j: (0, j)),
            pl.BlockSpec((k2, tn), lambda i, j: (1, j)),
            pl.BlockSpec((tm, tn), lambda i, j: (i, j)),
        ],
        out_specs=pl.BlockSpec((tm, tn), lambda i, j: (i, j)),
        out_shape=jax.ShapeDtypeStruct((m, n), F32),
        compiler_params=pltpu.CompilerParams(
            dimension_semantics=("parallel", "arbitrary"), vmem_limit_bytes=limit
        ),
        name=name,
    )(a1, a2, w, w, res)


def _bf16_part(x):
    bits = lax.bitcast_convert_type(x, jnp.int32) & jnp.int32(-65536)
    return lax.bitcast_convert_type(bits, F32)


def _diff_attn_body(ins, o_ref, scratch, *, tq, lam_init):
    slope_ref, lq1_ref, lk1_ref, lq2_ref, lk2_ref, g_ref, q_ref, k_ref, v_ref = ins
    kx_ref, s_ref, p_ref = scratch
    s_len = q_ref.shape[0]
    d = DA_HEAD_DIM
    lam = (jnp.exp(jnp.sum(lq1_ref[...] * lk1_ref[...], axis=-1, keepdims=True))
           - jnp.exp(jnp.sum(lq2_ref[...] * lk2_ref[...], axis=-1, keepdims=True))
           + lam_init)

    lane = lax.broadcasted_iota(jnp.int32, (1, d), 1)
    slope2 = jnp.full((1, d), slope_ref[pl.program_id(1)] * LOG2_E, F32)
    piece_hi = _bf16_part(slope2)
    rest = slope2 - piece_hi
    piece_mid = _bf16_part(rest)
    piece_lo = _bf16_part(rest - piece_mid)
    piece = jnp.where((lane == 0) | (lane == 3), piece_hi,
                      jnp.where((lane == 1) | (lane == 4), piece_mid, piece_lo))
    q_extra = jnp.where(lane < 3, piece * 256.0, jnp.where(lane < 6, piece, 0.0))
    q_extra = jnp.broadcast_to(q_extra, (tq, d)).astype(BF16)
    kpos = lax.broadcasted_iota(jnp.int32, (s_len, d), 0)
    klane = lax.broadcasted_iota(jnp.int32, (s_len, d), 1)
    k_extra = jnp.where(klane < 3, kpos >> 8, jnp.where(klane < 6, kpos & 255, 0))
    kx_ref[...] = k_extra.astype(F32).astype(kx_ref.dtype)

    row = lax.broadcasted_iota(jnp.int32, (tq, tq), 0)
    col = lax.broadcasted_iota(jnp.int32, (tq, tq), 1)
    causal = col <= row

    for qi in reversed(range(s_len // tq)):
        lo, hi = qi * tq, (qi + 1) * tq
        key_blocks = [slice(j * tq, (j + 1) * tq) for j in range(qi + 1)]
        heads = []
        for c in range(2):
            dcols = slice(c * d, (c + 1) * d)
            q_aug = jnp.concatenate([q_ref[lo:hi, dcols], q_extra], axis=1)
            m = None
            for j, cols in enumerate(key_blocks):
                k_aug = jnp.concatenate([k_ref[cols, dcols], kx_ref[cols, :]], axis=1)
                s = _dot_nt(q_aug, k_aug)
                if j == qi:
                    s = jnp.where(causal, s, NEG_INF)
                s_ref[c, :, cols] = s
                bm = _fold_lanes(s, jnp.maximum)
                m = bm if m is None else jnp.maximum(m, bm)
            m = jnp.max(m, axis=-1, keepdims=True)
            l = None
            for cols in key_blocks:
                p = jnp.exp2(s_ref[c, :, cols] - m)
                bl = _fold_lanes(p, jnp.add)
                l = bl if l is None else l + bl
                p_ref[c, :, cols] = p.astype(p_ref.dtype)
            l = jnp.sum(l, axis=-1, keepdims=True)
            acc = _dot(p_ref[c, :, 0:hi], v_ref[0:hi, :])
            heads.append((acc, l))
        (acc1, l1), (acc2, l2) = heads
        out = acc1 * (1.0 / l1) - acc2 * (lam / l2)
        y = _rms(out) * g_ref[...] * (1.0 - lam_init)
        o_ref[lo:hi, :] = y.astype(o_ref.dtype)
        yield


def _retention_body(ins, o_ref):
    lg_ref, q_ref, k_ref, v_ref, gate_ref = ins
    s_len = q_ref.shape[0]
    c = RET_CHUNK
    dk, dv = RET_QK_DIM, RET_V_DIM
    scale = dk ** -0.5
    lg = lg_ref[pl.program_id(1)]

    row = lax.broadcasted_iota(jnp.int32, (c, c), 0)
    col = lax.broadcasted_iota(jnp.int32, (c, c), 1)
    diff = (row - col).astype(F32)
    intra = jnp.where(diff >= 0, jnp.exp(lg * jnp.maximum(diff, 0.0)), 0.0) * scale
    row_k = lax.broadcasted_iota(jnp.int32, (c, dk), 0).astype(F32)
    k_decay = jnp.exp(lg * (float(c - 1) - row_k)) * scale
    row_v = lax.broadcasted_iota(jnp.int32, (c, dv), 0).astype(F32)
    q_decay = jnp.exp(lg * (row_v + 1.0))
    chunk_decay = jnp.exp(jnp.full((1, dv), lg * float(c), F32))

    state = jnp.zeros((dk, dv), F32)
    for i in range(s_len // c):
        lo, hi = i * c, (i + 1) * c
        q = q_ref[lo:hi, :]
        k = k_ref[lo:hi, :]
        v = v_ref[lo:hi, :]
        scores = _dot_nt(q, k) * intra
        y = _dot(scores.astype(BF16), v)
        if i > 0:
            y = y + _dot(q, state.astype(BF16)) * q_decay
        if i + 1 < s_len // c:
            kd = (k.astype(F32) * k_decay).astype(BF16)
            state = state * chunk_decay + _dot_tn(kd, v)
        gate = gate_ref[lo:hi, :].astype(F32)
        o_ref[lo:hi, :] = (_rms(y) * (gate * jax.nn.sigmoid(gate))).astype(o_ref.dtype)
        yield


N_ATTN_INS, N_RET_INS = 9, 5


def _mixers_kernel(*refs, n_side, tq, lam_init):
    ins, rest = refs[:N_ATTN_INS + N_RET_INS], refs[N_ATTN_INS + N_RET_INS:]
    side_in, rest = rest[:n_side], rest[n_side:]
    da_ref, ret_ref = rest[:2]
    side_out, scratch = rest[2:2 + n_side], rest[2 + n_side:]
    _cast_blocks(zip(side_in, side_out))
    pending = [_retention_body(ins[N_ATTN_INS:], ret_ref),
               _diff_attn_body(ins[:N_ATTN_INS], da_ref, scratch, tq=tq, lam_init=lam_init)]
    while pending:
        pending = [body for body in pending if next(body, StopIteration) is not StopIteration]


def _mixers(proj, slopes, lq1, lk1, lq2, lk2, g, log_gammas, *, batch, seq, lam_init, tq,
            side_weights=()):
    assert DA_HEADS == RET_HEADS
    hb = DA_V_DIM
    q_blk0, k_blk0, v_blk0 = 0, DA_HEADS, 2 * DA_HEADS
    da_cols = 3 * DA_HEADS * DA_V_DIM
    rq_blk0 = da_cols // RET_QK_DIM
    rk_blk0 = rq_blk0 + RET_HEADS
    rv_blk0 = (da_cols + 2 * RET_HEADS * RET_QK_DIM) // RET_V_DIM
    rg_blk0 = rv_blk0 + RET_HEADS
    grid = (batch, DA_HEADS)
    side = _SideCasts(side_weights, grid)
    scratch = [
        pltpu.VMEM((seq, DA_HEAD_DIM), BF16),
        pltpu.VMEM((2, tq, seq), F32),
        pltpu.VMEM((2, tq, seq), BF16),
    ]
    limit = _vmem_limit(
        [_nbytes((seq, hb), proj.dtype)] * 4
        + [_nbytes((seq, RET_QK_DIM), proj.dtype)] * 2 + [_nbytes((seq, RET_V_DIM), proj.dtype)] * 3
        + side.window_bytes,
        [_nbytes((seq, DA_HEAD_DIM), BF16),
         3 * _nbytes((2, tq, seq), F32), 3 * _nbytes((2, tq, seq), BF16)],
    )
    vec = pl.BlockSpec((1, DA_HEAD_DIM), lambda b, h: (0, 0))
    smem = pl.BlockSpec(memory_space=pltpu.SMEM)

    def head_cols(width, blk0):
        return pl.BlockSpec((seq, width), lambda b, h: (b, blk0 + h))

    outs = pl.pallas_call(
        functools.partial(_mixers_kernel, n_side=len(side), tq=tq, lam_init=lam_init),
        grid=grid,
        in_specs=[
            smem, vec, vec, vec, vec,
            pl.BlockSpec((1, hb), lambda b, h: (0, 0)),
            head_cols(hb, q_blk0), head_cols(hb, k_blk0), head_cols(hb, v_blk0),
            smem,
            head_cols(RET_QK_DIM, rq_blk0), head_cols(RET_QK_DIM, rk_blk0),
            head_cols(RET_V_DIM, rv_blk0), head_cols(RET_V_DIM, rg_blk0),
        ] + side.in_specs,
        out_specs=[head_cols(hb, 0), head_cols(RET_V_DIM, 0)] + side.out_specs,
        out_shape=[jax.ShapeDtypeStruct((batch * seq, DA_HEADS * hb), BF16),
                   jax.ShapeDtypeStruct((batch * seq, RET_HEADS * RET_V_DIM), BF16)] + side.out_shapes,
        scratch_shapes=scratch,
        compiler_params=pltpu.CompilerParams(
            dimension_semantics=("arbitrary", "arbitrary"), vmem_limit_bytes=limit
        ),
        name="token_mixers",
    )(slopes, lq1.reshape(1, -1), lk1.reshape(1, -1), lq2.reshape(1, -1), lk2.reshape(1, -1),
      g.reshape(1, hb), proj, proj, proj, log_gammas, proj, proj, proj, proj, *side_weights)
    return tuple(outs)


def _xattn_kernel(*refs, n_side):
    ins, o_ref, (xo_ref,), side = _split_refs(refs, 5, n_side)
    xq_ref, xk_ref, xv_ref, res_ref, wo_ref = ins
    _cast_blocks(side)
    d_model = xq_ref.shape[1]
    hd = d_model // XATTN_HEADS
    scale2 = hd ** -0.5 * LOG2_E
    for h in range(XATTN_HEADS):
        cols = slice(h * hd, (h + 1) * hd)
        s = _dot_nt(xq_ref[:, cols], xk_ref[:, cols]) * scale2
        p = jnp.exp2(s - jnp.max(s, axis=-1, keepdims=True))
        p = p * (1.0 / jnp.sum(p, axis=-1, keepdims=True))
        xo_ref[:, cols] = _dot(p.astype(BF16), xv_ref[:, cols]).astype(xo_ref.dtype)
    o_ref[...] = res_ref[...] + _dot(xo_ref[...], wo_ref[...])


def _cross_attention(xq, xk, xv, res, wo, *, batch, seq, mem_len, tq, side_weights=()):
    d = xq.shape[1]
    nq = seq // tq
    grid = (batch, nq)
    side = _SideCasts(side_weights, grid)
    limit = _vmem_limit(
        [
            _nbytes((tq, d), xq.dtype),
            _nbytes((mem_len, d), xk.dtype),
            _nbytes((mem_len, d), xv.dtype),
            _nbytes((tq, d), F32),
            _nbytes((d, d), wo.dtype),
            _nbytes((tq, d), F32),
        ] + side.window_bytes,
        [_nbytes((tq, d), BF16), _nbytes((tq, d), F32)],
    )
    outs = pl.pallas_call(
        functools.partial(_xattn_kernel, n_side=len(side)),
        grid=grid,
        in_specs=[
            pl.BlockSpec((tq, d), lambda b, i: (b * nq + i, 0)),
            pl.BlockSpec((mem_len, d), lambda b, i: (b, 0)),
            pl.BlockSpec((mem_len, d), lambda b, i: (b, 0)),
            pl.BlockSpec((tq, d), lambda b, i: (b * nq + i, 0)),
            pl.BlockSpec((d, d), lambda b, i: (0, 0)),
        ] + side.in_specs,
        out_specs=[pl.BlockSpec((tq, d), lambda b, i: (b * nq + i, 0))] + side.out_specs,
        out_shape=[jax.ShapeDtypeStruct((batch * seq, d), F32)] + side.out_shapes,
        scratch_shapes=[pltpu.VMEM((tq, d), BF16)],
        compiler_params=pltpu.CompilerParams(
            dimension_semantics=("arbitrary", "arbitrary"), vmem_limit_bytes=limit
        ),
        name="cross_attention",
    )(xq, xk, xv, res, wo, *side_weights)
    return tuple(outs)


def _ffn_kernel(x_ref, g_ref, wg_ref, wu_ref, wd_ref, gf_ref, o_ref, h_ref, *, final_norm):
    f = pl.program_id(1)
    last = pl.num_programs(1) - 1
    blocks = _row_blocks(x_ref.shape[0])

    def partial_ffn(h, chunked=False):
        hidden = wg_ref.shape[1]
        out = None
        for cols in (_col_chunks(hidden) if chunked else [slice(0, hidden)]):
            gate = _dot(h, wg_ref[:, cols])
            up = _dot(h, wu_ref[:, cols])
            act = (gate * jax.nn.sigmoid(gate)) * up
            part = _dot(act.astype(BF16), wd_ref[cols, :])
            out = part if out is None else out + part
        return out

    @pl.when(f == 0)
    def _():
        for rows in blocks:
            x = x_ref[rows, :]
            h = (_rms(x) * g_ref[...]).astype(h_ref.dtype)
            h_ref[rows, :] = h
            o_ref[rows, :] = x + partial_ffn(h)

    if final_norm:
        @pl.when(jnp.logical_and(f > 0, f < last))
        def _():
            o_ref[...] += partial_ffn(h_ref[...], chunked=True)

        @pl.when(f == last)
        def _():
            for rows in blocks:
                y = o_ref[rows, :] + partial_ffn(h_ref[rows, :])
                o_ref[rows, :] = _rms(y) * gf_ref[...]
    else:
        @pl.when(f > 0)
        def _():
            o_ref[...] += partial_ffn(h_ref[...], chunked=True)


def _ffn(x, g, wg, wu, wd, gf, *, final_norm, tm, tf):
    m, d = x.shape
    d_ff = wg.shape[1]
    limit = _vmem_limit(
        [
            _nbytes((tm, d), F32),
            _nbytes((d, tf), wg.dtype),
            _nbytes((d, tf), wu.dtype),
            _nbytes((tf, d), wd.dtype),
            _nbytes((tm, d), F32),
        ],
        [_nbytes((tm, d), BF16), _nbytes((tm, d), F32), 4 * _nbytes((tm, tf), F32)],
    )
    return pl.pallas_call(
        functools.partial(_ffn_kernel, final_norm=final_norm),
        grid=(m // tm, d_ff // tf),
        in_specs=[
            pl.BlockSpec((tm, d), lambda i, f: (i, 0)),
            pl.BlockSpec((1, d), lambda i, f: (0, 0)),
            pl.BlockSpec((d, tf), lambda i, f: (0, f)),
            pl.BlockSpec((d, tf), lambda i, f: (0, f)),
            pl.BlockSpec((tf, d), lambda i, f: (f, 0)),
            pl.BlockSpec((1, d), lambda i, f: (0, 0)),
        ],
        out_specs=pl.BlockSpec((tm, d), lambda i, f: (i, 0)),
        out_shape=jax.ShapeDtypeStruct((m, d), F32),
        scratch_shapes=[pltpu.VMEM((tm, d), BF16)],
        compiler_params=pltpu.CompilerParams(
            dimension_semantics=("parallel", "arbitrary"), vmem_limit_bytes=limit
        ),
        name="swiglu_ffn",
    )(x, g.reshape(1, d), wg, wu, wd, gf.reshape(1, d))


def kernel(x, mem, norm_mix_g, w_in, lambda_q1, lambda_k1, lambda_q2, lambda_k2, da_subln_g, w_o, norm_x_g, norm_mem_g, w_xq, w_xk, w_xv, w_xo, norm_ffn_g, w_gate, w_up, w_down, norm_f_g):
    batch, seq, d_model = x.shape
    mem_len = mem.shape[1]
    depth = w_in.shape[0]
    slopes = jnp.asarray(2.0 ** (-8.0 * np.arange(1, DA_HEADS + 1) / DA_HEADS), dtype=F32)
    log_gammas = jnp.asarray(np.log(1.0 - 2.0 ** (-5.0 - np.arange(RET_HEADS))), dtype=F32)

    xf = x.reshape(batch * seq, d_model)
    memf = mem.reshape(batch * mem_len, d_model)
    for l in range(depth):
        lam_init = 0.8 - 0.6 * math.exp(-0.3 * l)
        n_dq = DA_HEADS * 2 * DA_HEAD_DIM
        col_scale = jnp.where(jnp.arange(w_in.shape[2]) < n_dq, DA_HEAD_DIM ** -0.5 * LOG2_E, 1.0)
        proj, wb_o, wb_xq = _norm_matmul(
            xf, norm_mix_g[l], w_in[l], BF16, tm=ROW_TILE, tn=COL_TILE,
            group_tiles=IN_PROJ_GROUP_TILES, name="in_proj",
            col_scale=col_scale.astype(F32), side_weights=(w_o[l], w_xq[l]))
        da, ret, wb_up, wb_xk, wb_xv, wb_xo = _mixers(
            proj, slopes, lambda_q1[l], lambda_k1[l], lambda_q2[l], lambda_k2[l], da_subln_g[l],
            log_gammas, batch=batch, seq=seq, lam_init=lam_init, tq=ATTN_Q_TILE,
            side_weights=(w_up[l], w_xk[l], w_xv[l], w_xo[l]))
        xf = _matmul2_res(da, ret, wb_o, xf, tm=OUT_PROJ_ROW_TILE, tn=d_model, name="out_proj")

        xq, wb_gate = _norm_matmul(xf, norm_x_g[l], wb_xq, BF16, tm=ROW_TILE, tn=d_model,
                                   name="xattn_q", side_weights=(w_gate[l],))
        xk, xv = _norm_matmul_pair(memf, norm_mem_g[l], wb_xk, wb_xv, BF16, tm=ROW_TILE, tn=COL_TILE,
                                   name="xattn_kv")
        xf, wb_down = _cross_attention(xq, xk, xv, xf, wb_xo, batch=batch, seq=seq, mem_len=mem_len,
                                       tq=XATTN_Q_TILE, side_weights=(w_down[l],))

        xf = _ffn(xf, norm_ffn_g[l], wb_gate, wb_up, wb_down, norm_f_g,
                  final_norm=(l == depth - 1), tm=FFN_ROW_TILE, tf=FFN_COL_TILE)
    return xf.reshape(batch, seq, d_model)
```

```python
import functools
import math

import jax
import jax.numpy as jnp
import numpy as np
from jax import lax
from jax.experimental import pallas as pl
from jax.experimental.pallas import tpu as pltpu

F32 = jnp.float32
BF16 = jnp.bfloat16

DA_HEADS = 4
DA_HEAD_DIM = 128
DA_V_DIM = 2 * DA_HEAD_DIM
RET_HEADS = 4
RET_QK_DIM = 128
RET_V_DIM = 256
XATTN_HEADS = 4
RET_CHUNK = 256
NORM_EPS = 1e-6
NEG_INF = -1e30
LOG2_E = math.log2(math.e)
NORM_BLOCK_ROWS = 256

ROW_TILE = 1024
COL_TILE = 1024
KV_COL_TILE = 512
IN_PROJ_GROUP_TILES = 2
ATTN_Q_TILE = 512
XATTN_Q_TILE = 512
FFN_ROW_TILE, FFN_COL_TILE = 1024, 512

V7X_LANES = 128
MXU_COLS = 256
BF16_TILE_ROWS = 16
V7X_VMEM_BYTES = 64 * 1024 * 1024
V7X_VMEM_USABLE_BYTES = V7X_VMEM_BYTES - 6 * 1024 * 1024
COMPILER_SCRATCH_BYTES = 4 * 1024 * 1024


def _nbytes(shape, dtype):
    return int(np.prod(shape)) * jnp.dtype(dtype).itemsize


def _vmem_limit(pipelined, resident):
    need = 2 * sum(pipelined) + sum(resident) + COMPILER_SCRATCH_BYTES
    return int(min(V7X_VMEM_USABLE_BYTES, need))


def _rms(x):
    return x * lax.rsqrt(jnp.mean(x * x, axis=-1, keepdims=True) + NORM_EPS)


def _fold_lanes(x, op):
    tiles = [x[:, i:i + V7X_LANES] for i in range(0, x.shape[1], V7X_LANES)]
    return functools.reduce(op, tiles)


def _dot(a, b):
    return jnp.dot(a, b, preferred_element_type=F32)


def _dot_nt(a, b):
    return lax.dot_general(a, b, (((1,), (1,)), ((), ())), preferred_element_type=F32)


def _dot_tn(a, b):
    return lax.dot_general(a, b, (((0,), (0,)), ((), ())), preferred_element_type=F32)


class _SideCasts:
    def __init__(self, weights, grid):
        self.weights = list(weights)
        self.grid = tuple(grid)
        n_steps = int(np.prod(self.grid))
        self.plans = []
        for w in self.weights:
            rows, n_blocks = w.shape[0], n_steps
            while rows % n_blocks or (rows // n_blocks) % BF16_TILE_ROWS:
                n_blocks -= 1
            self.plans.append((n_blocks, rows // n_blocks))

    def __len__(self):
        return len(self.weights)

    def _specs(self):
        specs = []
        for w, (n_blocks, block_rows) in zip(self.weights, self.plans):
            def index(*ids, n_blocks=n_blocks):
                step = ids[0]
                for extent, idx in zip(self.grid[1:], ids[1:]):
                    step = step * extent + idx
                return (jnp.minimum(step, n_blocks - 1), 0)
            specs.append(pl.BlockSpec((block_rows, w.shape[1]), index))
        return specs

    in_specs = property(_specs)
    out_specs = property(_specs)

    @property
    def out_shapes(self):
        return [jax.ShapeDtypeStruct(w.shape, BF16) for w in self.weights]

    @property
    def window_bytes(self):
        return [_nbytes((rows, w.shape[1]), dt)
                for w, (_, rows) in zip(self.weights, self.plans) for dt in (w.dtype, BF16)]


def _split_refs(refs, n_in, n_side):
    ins, rest = refs[:n_in], refs[n_in:]
    side_in, rest = rest[:n_side], rest[n_side:]
    out, side_out, scratch = rest[0], rest[1:1 + n_side], rest[1 + n_side:]
    return ins, out, scratch, list(zip(side_in, side_out))


def _cast_blocks(pairs):
    for src, dst in pairs:
        dst[...] = src[...].astype(dst.dtype)


def _row_blocks(n_rows):
    step = min(NORM_BLOCK_ROWS, n_rows)
    return [slice(r, r + step) for r in range(0, n_rows, step)]


def _col_chunks(n_cols):
    step = min(MXU_COLS, n_cols)
    return [slice(c, c + step) for c in range(0, n_cols, step)]


def _norm_matmul_kernel(*refs, n_side, cast_w, group_tiles):
    ins, o_ref, scratch, side = _split_refs(refs, 4 if cast_w else 3, n_side)
    x_ref, g_ref, w_ref = ins[:3]
    h_ref = scratch[0]
    wb_ref = scratch[1] if cast_w else w_ref
    step = pl.program_id(1)
    tm = x_ref.shape[0]

    def cast_weight(cols):
        wb_ref[:, cols] = (w_ref[:, cols] * ins[3][:, cols]).astype(wb_ref.dtype)

    for t in range(group_tiles):
        @pl.when(step == t)
        def _(t=t):
            _cast_blocks(side)
            if cast_w and t == 0:
                cast_weight(slice(None))
            for rows in _row_blocks(tm):
                group_rows = slice(t * tm + rows.start, t * tm + rows.stop)
                h = (_rms(x_ref[rows, :]) * g_ref[...]).astype(h_ref.dtype)
                h_ref[group_rows, :] = h
                o_ref[group_rows, :] = _dot(h, wb_ref[...]).astype(o_ref.dtype)

    @pl.when(step >= group_tiles)
    def _():
        _cast_blocks(side)
        if cast_w:
            for cols in _col_chunks(w_ref.shape[1]):
                cast_weight(cols)
                o_ref[:, cols] = _dot(h_ref[...], wb_ref[:, cols]).astype(o_ref.dtype)
        else:
            o_ref[...] = _dot(h_ref[...], w_ref[...]).astype(o_ref.dtype)


def _norm_matmul(x, g, w, out_dtype, *, tm, tn, name, group_tiles=1, col_scale=None,
                 side_weights=()):
    m, d = x.shape
    n = w.shape[1]
    tm, tn = min(tm, m), min(tn, n)
    gt = group_tiles
    grid = (m // (tm * gt), gt + n // tn - 1)
    side = _SideCasts(side_weights, grid)
    cast_w = w.dtype != BF16
    assert cast_w or col_scale is None

    def col_tile(s):
        return jnp.maximum(s - (gt - 1), 0)

    weight_ins, weight_specs = [w], [pl.BlockSpec((d, tn), lambda gi, s: (0, col_tile(s)))]
    if cast_w:
        cs = jnp.ones((n,), F32) if col_scale is None else col_scale
        weight_ins.append(cs.reshape(1, n))
        weight_specs.append(pl.BlockSpec((1, tn), lambda gi, s: (0, col_tile(s))))
    scratch = [pltpu.VMEM((gt * tm, d), BF16)] + ([pltpu.VMEM((d, tn), BF16)] if cast_w else [])
    limit = _vmem_limit(
        [_nbytes((tm, d), x.dtype), _nbytes((d, tn), w.dtype), _nbytes((gt * tm, tn), out_dtype),
         _nbytes((8, d), F32), _nbytes((8, tn), F32)]
        + side.window_bytes,
        [_nbytes((gt * tm, d), BF16), _nbytes((d, tn), BF16) * cast_w],
    )
    outs = pl.pallas_call(
        functools.partial(_norm_matmul_kernel, n_side=len(side), cast_w=cast_w, group_tiles=gt),
        grid=grid,
        in_specs=[
            pl.BlockSpec((tm, d), lambda gi, s: (gi * gt + jnp.minimum(s, gt - 1), 0)),
            pl.BlockSpec((1, d), lambda gi, s: (0, 0)),
        ] + weight_specs + side.in_specs,
        out_specs=[pl.BlockSpec((gt * tm, tn), lambda gi, s: (gi, col_tile(s)))] + side.out_specs,
        out_shape=[jax.ShapeDtypeStruct((m, n), out_dtype)] + side.out_shapes,
        scratch_shapes=scratch,
        compiler_params=pltpu.CompilerParams(
            dimension_semantics=("arbitrary", "arbitrary"), vmem_limit_bytes=limit
        ),
        name=name,
    )(x, g.reshape(1, d), *weight_ins, *side_weights)
    return tuple(outs)


def _norm_matmul_pair_kernel(x_ref, g_ref, w1_ref, w2_ref, o1_ref, o2_ref, h_ref):
    @pl.when(pl.program_id(1) == 0)
    def _():
        h_ref[...] = (_rms(x_ref[...]) * g_ref[...]).astype(h_ref.dtype)

    for w_ref, o_ref in ((w1_ref, o1_ref), (w2_ref, o2_ref)):
        for cols in _col_chunks(w_ref.shape[1]):
            o_ref[:, cols] = _dot(h_ref[...], w_ref[:, cols].astype(BF16)).astype(o_ref.dtype)


def _norm_matmul_pair(x, g, w1, w2, out_dtype, *, tm, tn, name):
    m, d = x.shape
    n = w1.shape[1]
    assert w1.shape == w2.shape
    tm, tn = min(tm, m), min(tn, n)
    w_spec = pl.BlockSpec((d, tn), lambda i, j: (0, j))
    o_spec = pl.BlockSpec((tm, tn), lambda i, j: (i, j))
    limit = _vmem_limit(
        [_nbytes((tm, d), x.dtype)] + [_nbytes((d, tn), w1.dtype), _nbytes((tm, tn), out_dtype)] * 2,
        [_nbytes((tm, d), BF16), _nbytes((tm, d), F32)],
    )
    return pl.pallas_call(
        _norm_matmul_pair_kernel,
        grid=(m // tm, n // tn),
        in_specs=[pl.BlockSpec((tm, d), lambda i, j: (i, 0)), pl.BlockSpec((1, d), lambda i, j: (0, 0)),
                  w_spec, w_spec],
        out_specs=[o_spec, o_spec],
        out_shape=[jax.ShapeDtypeStruct((m, n), out_dtype)] * 2,
        scratch_shapes=[pltpu.VMEM((tm, d), BF16)],
        compiler_params=pltpu.CompilerParams(
            dimension_semantics=("arbitrary", "arbitrary"), vmem_limit_bytes=limit
        ),
        name=name,
    )(x, g.reshape(1, d), w1, w2)


def _matmul2_res_kernel(a1_ref, a2_ref, w1_ref, w2_ref, res_ref, o_ref):
    acc = _dot(a1_ref[...], w1_ref[...]) + _dot(a2_ref[...], w2_ref[...])
    o_ref[...] = res_ref[...] + acc


def _matmul2_res(a1, a2, w, res, *, tm, name):
    m, k1 = a1.shape
    k2 = a2.shape[1]
    assert k1 == k2 and w.shape[0] == k1 + k2
    n = w.shape[1]
    limit = _vmem_limit(
        [_nbytes((tm, k1), a1.dtype), _nbytes((tm, k2), a2.dtype),
         _nbytes((tm, n), F32), _nbytes((tm, n), F32)],
        [_nbytes(w.shape, w.dtype), _nbytes((tm, n), F32)],
    )
    resident = pl.Buffered(1)
    return pl.pallas_call(
        _matmul2_res_kernel,
        grid=(m // tm,),
        in_specs=[
            pl.BlockSpec((tm, k1), lambda i: (i, 0)),
            pl.BlockSpec((tm, k2), lambda i: (i, 0)),
            pl.BlockSpec((k1, n), lambda i: (0, 0), pipeline_mode=resident),
            pl.BlockSpec((k2, n), lambda i: (1, 0), pipeline_mode=resident),
            pl.BlockSpec((tm, n), lambda i: (i, 0)),
        ],
        out_specs=pl.BlockSpec((tm, n), lambda i: (i, 0)),
        out_shape=jax.ShapeDtypeStruct((m, n), F32),
        compiler_params=pltpu.CompilerParams(
            dimension_semantics=("arbitrary",), vmem_limit_bytes=limit
        ),
        name=name,
    )(a1, a2, w, w, res)


def _bf16_part(x):
    bits = lax.bitcast_convert_type(x, jnp.int32) & jnp.int32(-65536)
    return lax.bitcast_convert_type(bits, F32)


def _diff_attn_body(ins, o_ref, scratch, *, tq, lam_init):
    slope_ref, lq1_ref, lk1_ref, lq2_ref, lk2_ref, g_ref, q_ref, k_ref, v_ref = ins
    kx_ref, s_ref, p_ref = scratch
    s_len = q_ref.shape[0]
    d = DA_HEAD_DIM
    lam = (jnp.exp(jnp.sum(lq1_ref[...] * lk1_ref[...], axis=-1, keepdims=True))
           - jnp.exp(jnp.sum(lq2_ref[...] * lk2_ref[...], axis=-1, keepdims=True))
           + lam_init)

    lane = lax.broadcasted_iota(jnp.int32, (1, d), 1)
    slope2 = jnp.full((1, d), slope_ref[pl.program_id(1)] * LOG2_E, F32)
    piece_hi = _bf16_part(slope2)
    rest = slope2 - piece_hi
    piece_mid = _bf16_part(rest)
    piece_lo = _bf16_part(rest - piece_mid)
    piece = jnp.where((lane == 0) | (lane == 3), piece_hi,
                      jnp.where((lane == 1) | (lane == 4), piece_mid, piece_lo))
    q_extra = jnp.where(lane < 3, piece * 256.0, jnp.where(lane < 6, piece, 0.0))
    q_extra = jnp.broadcast_to(q_extra, (tq, d)).astype(BF16)
    kpos = lax.broadcasted_iota(jnp.int32, (s_len, d), 0)
    klane = lax.broadcasted_iota(jnp.int32, (s_len, d), 1)
    k_extra = jnp.where(klane < 3, kpos >> 8, jnp.where(klane < 6, kpos & 255, 0))
    kx_ref[...] = k_extra.astype(F32).astype(kx_ref.dtype)

    row = lax.broadcasted_iota(jnp.int32, (tq, tq), 0)
    col = lax.broadcasted_iota(jnp.int32, (tq, tq), 1)
    causal = col <= row

    for qi in reversed(range(s_len // tq)):
        lo, hi = qi * tq, (qi + 1) * tq
        key_blocks = [slice(j * tq, (j + 1) * tq) for j in range(qi + 1)]
        heads = []
        for c in range(2):
            dcols = slice(c * d, (c + 1) * d)
            q_aug = jnp.concatenate([q_ref[lo:hi, dcols], q_extra], axis=1)
            m = None
            for j, cols in enumerate(key_blocks):
                k_aug = jnp.concatenate([k_ref[cols, dcols], kx_ref[cols, :]], axis=1)
                s = _dot_nt(q_aug, k_aug)
                if j == qi:
                    s = jnp.where(causal, s, NEG_INF)
                s_ref[c, :, cols] = s
                bm = _fold_lanes(s, jnp.maximum)
                m = bm if m is None else jnp.maximum(m, bm)
            m = jnp.max(m, axis=-1, keepdims=True)
            l = None
            for cols in key_blocks:
                p = jnp.exp2(s_ref[c, :, cols] - m)
                bl = _fold_lanes(p, jnp.add)
                l = bl if l is None else l + bl
                p_ref[c, :, cols] = p.astype(p_ref.dtype)
            l = jnp.sum(l, axis=-1, keepdims=True)
            acc = _dot(p_ref[c, :, 0:hi], v_ref[0:hi, :])
            heads.append((acc, l))
        (acc1, l1), (acc2, l2) = heads
        out = acc1 * (1.0 / l1) - acc2 * (lam / l2)
        y = _rms(out) * g_ref[...] * (1.0 - lam_init)
        o_ref[lo:hi, :] = y.astype(o_ref.dtype)
        yield


def _retention_body(ins, o_ref):
    lg_ref, q_ref, k_ref, v_ref, gate_ref = ins
    s_len = q_ref.shape[0]
    c = RET_CHUNK
    dk, dv = RET_QK_DIM, RET_V_DIM
    scale = dk ** -0.5
    lg = lg_ref[pl.program_id(1)]

    row = lax.broadcasted_iota(jnp.int32, (c, c), 0)
    col = lax.broadcasted_iota(jnp.int32, (c, c), 1)
    diff = (row - col).astype(F32)
    intra = jnp.where(diff >= 0, jnp.exp(lg * jnp.maximum(diff, 0.0)), 0.0) * scale
    row_k = lax.broadcasted_iota(jnp.int32, (c, dk), 0).astype(F32)
    k_decay = jnp.exp(lg * (float(c - 1) - row_k)) * scale
    row_v = lax.broadcasted_iota(jnp.int32, (c, dv), 0).astype(F32)
    q_decay = jnp.exp(lg * (row_v + 1.0))
    chunk_decay = jnp.exp(jnp.full((1, dv), lg * float(c), F32))

    state = jnp.zeros((dk, dv), F32)
    for i in range(s_len // c):
        lo, hi = i * c, (i + 1) * c
        q = q_ref[lo:hi, :]
        k = k_ref[lo:hi, :]
        v = v_ref[lo:hi, :]
        scores = _dot_nt(q, k) * intra
        y = _dot(scores.astype(BF16), v)
        if i > 0:
            y = y + _dot(q, state.astype(BF16)) * q_decay
        if i + 1 < s_len // c:
            kd = (k.astype(F32) * k_decay).astype(BF16)
            state = state * chunk_decay + _dot_tn(kd, v)
        gate = gate_ref[lo:hi, :].astype(F32)
        o_ref[lo:hi, :] = (_rms(y) * (gate * jax.nn.sigmoid(gate))).astype(o_ref.dtype)
        yield


N_ATTN_INS, N_RET_INS = 9, 5


def _mixers_kernel(*refs, n_side, tq, lam_init):
    ins, rest = refs[:N_ATTN_INS + N_RET_INS], refs[N_ATTN_INS + N_RET_INS:]
    side_in, rest = rest[:n_side], rest[n_side:]
    da_ref, ret_ref = rest[:2]
    side_out, scratch = rest[2:2 + n_side], rest[2 + n_side:]
    _cast_blocks(zip(side_in, side_out))
    pending = [_retention_body(ins[N_ATTN_INS:], ret_ref),
               _diff_attn_body(ins[:N_ATTN_INS], da_ref, scratch, tq=tq, lam_init=lam_init)]
    while pending:
        pending = [body for body in pending if next(body, StopIteration) is not StopIteration]


def _mixers(proj, slopes, lq1, lk1, lq2, lk2, g, log_gammas, *, batch, seq, lam_init, tq,
            side_weights=()):
    assert DA_HEADS == RET_HEADS
    hb = DA_V_DIM
    q_blk0, k_blk0, v_blk0 = 0, DA_HEADS, 2 * DA_HEADS
    da_cols = 3 * DA_HEADS * DA_V_DIM
    rq_blk0 = da_cols // RET_QK_DIM
    rk_blk0 = rq_blk0 + RET_HEADS
    rv_blk0 = (da_cols + 2 * RET_HEADS * RET_QK_DIM) // RET_V_DIM
    rg_blk0 = rv_blk0 + RET_HEADS
    grid = (batch, DA_HEADS)
    side = _SideCasts(side_weights, grid)
    scratch = [
        pltpu.VMEM((seq, DA_HEAD_DIM), BF16),
        pltpu.VMEM((2, tq, seq), F32),
        pltpu.VMEM((2, tq, seq), BF16),
    ]
    limit = _vmem_limit(
        [_nbytes((seq, hb), proj.dtype)] * 4
        + [_nbytes((seq, RET_QK_DIM), proj.dtype)] * 2 + [_nbytes((seq, RET_V_DIM), proj.dtype)] * 3
        + side.window_bytes,
        [_nbytes((seq, DA_HEAD_DIM), BF16),
         3 * _nbytes((2, tq, seq), F32), 3 * _nbytes((2, tq, seq), BF16)],
    )
    vec = pl.BlockSpec((1, DA_HEAD_DIM), lambda b, h: (0, 0))
    smem = pl.BlockSpec(memory_space=pltpu.SMEM)

    def head_cols(width, blk0):
        return pl.BlockSpec((seq, width), lambda b, h: (b, blk0 + h))

    outs = pl.pallas_call(
        functools.partial(_mixers_kernel, n_side=len(side), tq=tq, lam_init=lam_init),
        grid=grid,
        in_specs=[
            smem, vec, vec, vec, vec,
            pl.BlockSpec((1, hb), lambda b, h: (0, 0)),
            head_cols(hb, q_blk0), head_cols(hb, k_blk0), head_cols(hb, v_blk0),
            smem,
            head_cols(RET_QK_DIM, rq_blk0), head_cols(RET_QK_DIM, rk_blk0),
            head_cols(RET_V_DIM, rv_blk0), head_cols(RET_V_DIM, rg_blk0),
        ] + side.in_specs,
        out_specs=[head_cols(hb, 0), head_cols(RET_V_DIM, 0)] + side.out_specs,
        out_shape=[jax.ShapeDtypeStruct((batch * seq, DA_HEADS * hb), BF16),
                   jax.ShapeDtypeStruct((batch * seq, RET_HEADS * RET_V_DIM), BF16)] + side.out_shapes,
        scratch_shapes=scratch,
        compiler_params=pltpu.CompilerParams(
            dimension_semantics=("arbitrary", "arbitrary"), vmem_limit_bytes=limit
        ),
        name="token_mixers",
    )(slopes, lq1.reshape(1, -1), lk1.reshape(1, -1), lq2.reshape(1, -1), lk2.reshape(1, -1),
      g.reshape(1, hb), proj, proj, proj, log_gammas, proj, proj, proj, proj, *side_weights)
    return tuple(outs)


def _xattn_kernel(*refs, n_side):
    ins, o_ref, (xo_ref,), side = _split_refs(refs, 5, n_side)
    xq_ref, xk_ref, xv_ref, res_ref, wo_ref = ins
    _cast_blocks(side)
    d_model = xq_ref.shape[1]
    hd = d_model // XATTN_HEADS
    scale2 = hd ** -0.5 * LOG2_E
    for h in range(XATTN_HEADS):
        cols = slice(h * hd, (h + 1) * hd)
        s = _dot_nt(xq_ref[:, cols], xk_ref[:, cols]) * scale2
        p = jnp.exp2(s - jnp.max(s, axis=-1, keepdims=True))
        p = p * (1.0 / jnp.sum(p, axis=-1, keepdims=True))
        xo_ref[:, cols] = _dot(p.astype(BF16), xv_ref[:, cols]).astype(xo_ref.dtype)
    o_ref[...] = res_ref[...] + _dot(xo_ref[...], wo_ref[...])


def _cross_attention(xq, xk, xv, res, wo, *, batch, seq, mem_len, tq, side_weights=()):
    d = xq.shape[1]
    nq = seq // tq
    grid = (batch, nq)
    side = _SideCasts(side_weights, grid)
    limit = _vmem_limit(
        [
            _nbytes((tq, d), xq.dtype),
            _nbytes((mem_len, d), xk.dtype),
            _nbytes((mem_len, d), xv.dtype),
            _nbytes((tq, d), F32),
            _nbytes((d, d), wo.dtype),
            _nbytes((tq, d), F32),
        ] + side.window_bytes,
        [_nbytes((tq, d), BF16), _nbytes((tq, d), F32)],
    )
    outs = pl.pallas_call(
        functools.partial(_xattn_kernel, n_side=len(side)),
        grid=grid,
        in_specs=[
            pl.BlockSpec((tq, d), lambda b, i: (b * nq + i, 0)),
            pl.BlockSpec((mem_len, d), lambda b, i: (b, 0)),
            pl.BlockSpec((mem_len, d), lambda b, i: (b, 0)),
            pl.BlockSpec((tq, d), lambda b, i: (b * nq + i, 0)),
            pl.BlockSpec((d, d), lambda b, i: (0, 0)),
        ] + side.in_specs,
        out_specs=[pl.BlockSpec((tq, d), lambda b, i: (b * nq + i, 0))] + side.out_specs,
        out_shape=[jax.ShapeDtypeStruct((batch * seq, d), F32)] + side.out_shapes,
        scratch_shapes=[pltpu.VMEM((tq, d), BF16)],
        compiler_params=pltpu.CompilerParams(
            dimension_semantics=("arbitrary", "arbitrary"), vmem_limit_bytes=limit
        ),
        name="cross_attention",
    )(xq, xk, xv, res, wo, *side_weights)
    return tuple(outs)


def _ffn_kernel(x_ref, g_ref, wg_ref, wu_ref, wd_ref, gf_ref, o_ref, h_ref, *, final_norm):
    f = pl.program_id(1)
    last = pl.num_programs(1) - 1
    blocks = _row_blocks(x_ref.shape[0])

    def partial_ffn(h, chunked=False):
        hidden = wg_ref.shape[1]
        out = None
        for cols in (_col_chunks(hidden) if chunked else [slice(0, hidden)]):
            gate = _dot(h, wg_ref[:, cols])
            up = _dot(h, wu_ref[:, cols])
            act = (gate * jax.nn.sigmoid(gate)) * up
            part = _dot(act.astype(BF16), wd_ref[cols, :])
            out = part if out is None else out + part
        return out

    @pl.when(f == 0)
    def _():
        for rows in blocks:
            x = x_ref[rows, :]
            h = (_rms(x) * g_ref[...]).astype(h_ref.dtype)
            h_ref[rows, :] = h
            o_ref[rows, :] = x + partial_ffn(h)

    if final_norm:
        @pl.when(jnp.logical_and(f > 0, f < last))
        def _():
            o_ref[...] += partial_ffn(h_ref[...], chunked=True)

        @pl.when(f == last)
        def _():
            for rows in blocks:
                y = o_ref[rows, :] + partial_ffn(h_ref[rows, :])
                o_ref[rows, :] = _rms(y) * gf_ref[...]
    else:
        @pl.when(f > 0)
        def _():
            o_ref[...] += partial_ffn(h_ref[...], chunked=True)


def _ffn(x, g, wg, wu, wd, gf, *, final_norm, tm, tf):
    m, d = x.shape
    d_ff = wg.shape[1]
    limit = _vmem_limit(
        [
            _nbytes((tm, d), F32),
            _nbytes((d, tf), wg.dtype),
            _nbytes((d, tf), wu.dtype),
            _nbytes((tf, d), wd.dtype),
            _nbytes((tm, d), F32),
        ],
        [_nbytes((tm, d), BF16), _nbytes((tm, d), F32), 4 * _nbytes((tm, tf), F32)],
    )
    return pl.pallas_call(
        functools.partial(_ffn_kernel, final_norm=final_norm),
        grid=(m // tm, d_ff // tf),
        in_specs=[
            pl.BlockSpec((tm, d), lambda i, f: (i, 0)),
            pl.BlockSpec((1, d), lambda i, f: (0, 0)),
            pl.BlockSpec((d, tf), lambda i, f: (0, f)),
            pl.BlockSpec((d, tf), lambda i, f: (0, f)),
            pl.BlockSpec((tf, d), lambda i, f: (f, 0)),
            pl.BlockSpec((1, d), lambda i, f: (0, 0)),
        ],
        out_specs=pl.BlockSpec((tm, d), lambda i, f: (i, 0)),
        out_shape=jax.ShapeDtypeStruct((m, d), F32),
        scratch_shapes=[pltpu.VMEM((tm, d), BF16)],
        compiler_params=pltpu.CompilerParams(
            dimension_semantics=("parallel", "arbitrary"), vmem_limit_bytes=limit
        ),
        name="swiglu_ffn",
    )(x, g.reshape(1, d), wg, wu, wd, gf.reshape(1, d))


def kernel(x, mem, norm_mix_g, w_in, lambda_q1, lambda_k1, lambda_q2, lambda_k2, da_subln_g, w_o, norm_x_g, norm_mem_g, w_xq, w_xk, w_xv, w_xo, norm_ffn_g, w_gate, w_up, w_down, norm_f_g):
    batch, seq, d_model = x.shape
    mem_len = mem.shape[1]
    depth = w_in.shape[0]
    slopes = jnp.asarray(2.0 ** (-8.0 * np.arange(1, DA_HEADS + 1) / DA_HEADS), dtype=F32)
    log_gammas = jnp.asarray(np.log(1.0 - 2.0 ** (-5.0 - np.arange(RET_HEADS))), dtype=F32)

    xf = x.reshape(batch * seq, d_model)
    memf = mem.reshape(batch * mem_len, d_model)
    for l in range(depth):
        lam_init = 0.8 - 0.6 * math.exp(-0.3 * l)
        n_dq = DA_HEADS * 2 * DA_HEAD_DIM
        col_scale = jnp.where(jnp.arange(w_in.shape[2]) < n_dq, DA_HEAD_DIM ** -0.5 * LOG2_E, 1.0)
        proj, = _norm_matmul(
            xf, norm_mix_g[l], w_in[l], BF16, tm=ROW_TILE, tn=COL_TILE,
            group_tiles=IN_PROJ_GROUP_TILES, name="in_proj", col_scale=col_scale.astype(F32))
        da, ret, wb_up, wb_o, wb_xq, wb_xo = _mixers(
            proj, slopes, lambda_q1[l], lambda_k1[l], lambda_q2[l], lambda_k2[l], da_subln_g[l],
            log_gammas, batch=batch, seq=seq, lam_init=lam_init, tq=ATTN_Q_TILE,
            side_weights=(w_up[l], w_o[l], w_xq[l], w_xo[l]))
        xf = _matmul2_res(da, ret, wb_o, xf, tm=ROW_TILE, name="out_proj")

        xq, wb_gate = _norm_matmul(xf, norm_x_g[l], wb_xq, BF16, tm=ROW_TILE, tn=d_model,
                                   name="xattn_q", side_weights=(w_gate[l],))
        xk, xv = _norm_matmul_pair(memf, norm_mem_g[l], w_xk[l], w_xv[l], BF16, tm=ROW_TILE,
                                   tn=KV_COL_TILE, name="xattn_kv")
        xf, wb_down = _cross_attention(xq, xk, xv, xf, wb_xo, batch=batch, seq=seq, mem_len=mem_len,
                                       tq=XATTN_Q_TILE, side_weights=(w_down[l],))

        xf = _ffn(xf, norm_ffn_g[l], wb_gate, wb_up, wb_down, norm_f_g,
                  final_norm=(l == depth - 1), tm=FFN_ROW_TILE, tf=FFN_COL_TILE)
    return xf.reshape(batch, seq, d_model)
```

```python
import functools
import math

import jax
import jax.numpy as jnp
import numpy as np
from jax import lax
from jax.experimental import pallas as pl
from jax.experimental.pallas import tpu as pltpu

F32 = jnp.float32
BF16 = jnp.bfloat16

DA_HEADS = 4
DA_HEAD_DIM = 128
DA_V_DIM = 2 * DA_HEAD_DIM
RET_HEADS = 4
RET_QK_DIM = 128
RET_V_DIM = 256
XATTN_HEADS = 4
RET_CHUNK = 256
NORM_EPS = 1e-6
NEG_INF = -1e30
LOG2_E = math.log2(math.e)
NORM_BLOCK_ROWS = 256

ROW_TILE = 1024
COL_TILE = 1024
KV_COL_TILE = 512
IN_PROJ_GROUP_TILES = 2
OUT_PROJ_ROW_TILE = 512
ATTN_Q_TILE = 512
XATTN_Q_TILE = 512
FFN_ROW_TILE, FFN_COL_TILE = 1024, 512

V7X_LANES = 128
MXU_COLS = 256
BF16_TILE_ROWS = 16
V7X_VMEM_BYTES = 64 * 1024 * 1024
V7X_VMEM_USABLE_BYTES = V7X_VMEM_BYTES - 6 * 1024 * 1024
COMPILER_SCRATCH_BYTES = 4 * 1024 * 1024


def _nbytes(shape, dtype):
    return int(np.prod(shape)) * jnp.dtype(dtype).itemsize


def _vmem_limit(pipelined, resident):
    need = 2 * sum(pipelined) + sum(resident) + COMPILER_SCRATCH_BYTES
    return int(min(V7X_VMEM_USABLE_BYTES, need))


def _rms(x):
    return x * lax.rsqrt(jnp.mean(x * x, axis=-1, keepdims=True) + NORM_EPS)


def _fold_lanes(x, op):
    tiles = [x[:, i:i + V7X_LANES] for i in range(0, x.shape[1], V7X_LANES)]
    return functools.reduce(op, tiles)


def _dot(a, b):
    return jnp.dot(a, b, preferred_element_type=F32)


def _dot_nt(a, b):
    return lax.dot_general(a, b, (((1,), (1,)), ((), ())), preferred_element_type=F32)


def _dot_tn(a, b):
    return lax.dot_general(a, b, (((0,), (0,)), ((), ())), preferred_element_type=F32)


class _SideCasts:
    def __init__(self, weights, grid):
        self.weights = list(weights)
        self.grid = tuple(grid)
        n_steps = int(np.prod(self.grid))
        self.plans = []
        for w in self.weights:
            rows, n_blocks = w.shape[0], n_steps
            while rows % n_blocks or (rows // n_blocks) % BF16_TILE_ROWS:
                n_blocks -= 1
            self.plans.append((n_blocks, rows // n_blocks))

    def __len__(self):
        return len(self.weights)

    def _specs(self):
        specs = []
        for w, (n_blocks, block_rows) in zip(self.weights, self.plans):
            def index(*ids, n_blocks=n_blocks):
                step = ids[0]
                for extent, idx in zip(self.grid[1:], ids[1:]):
                    step = step * extent + idx
                return (jnp.minimum(step, n_blocks - 1), 0)
            specs.append(pl.BlockSpec((block_rows, w.shape[1]), index))
        return specs

    in_specs = property(_specs)
    out_specs = property(_specs)

    @property
    def out_shapes(self):
        return [jax.ShapeDtypeStruct(w.shape, BF16) for w in self.weights]

    @property
    def window_bytes(self):
        return [_nbytes((rows, w.shape[1]), dt)
                for w, (_, rows) in zip(self.weights, self.plans) for dt in (w.dtype, BF16)]


def _split_refs(refs, n_in, n_side):
    ins, rest = refs[:n_in], refs[n_in:]
    side_in, rest = rest[:n_side], rest[n_side:]
    out, side_out, scratch = rest[0], rest[1:1 + n_side], rest[1 + n_side:]
    return ins, out, scratch, list(zip(side_in, side_out))


def _cast_blocks(pairs):
    for src, dst in pairs:
        dst[...] = src[...].astype(dst.dtype)


def _row_blocks(n_rows):
    step = min(NORM_BLOCK_ROWS, n_rows)
    return [slice(r, r + step) for r in range(0, n_rows, step)]


def _col_chunks(n_cols):
    step = min(MXU_COLS, n_cols)
    return [slice(c, c + step) for c in range(0, n_cols, step)]


def _norm_matmul_kernel(*refs, n_side, cast_w, group_tiles):
    ins, o_ref, scratch, side = _split_refs(refs, 4 if cast_w else 3, n_side)
    x_ref, g_ref, w_ref = ins[:3]
    h_ref = scratch[0]
    wb_ref = scratch[1] if cast_w else w_ref
    step = pl.program_id(1)
    tm = x_ref.shape[0]

    def cast_weight(cols):
        wb_ref[:, cols] = (w_ref[:, cols] * ins[3][:, cols]).astype(wb_ref.dtype)

    for t in range(group_tiles):
        @pl.when(step == t)
        def _(t=t):
            _cast_blocks(side)
            if cast_w and t == 0:
                cast_weight(slice(None))
            for rows in _row_blocks(tm):
                group_rows = slice(t * tm + rows.start, t * tm + rows.stop)
                h = (_rms(x_ref[rows, :]) * g_ref[...]).astype(h_ref.dtype)
                h_ref[group_rows, :] = h
                o_ref[group_rows, :] = _dot(h, wb_ref[...]).astype(o_ref.dtype)

    @pl.when(step >= group_tiles)
    def _():
        _cast_blocks(side)
        if cast_w:
            for cols in _col_chunks(w_ref.shape[1]):
                cast_weight(cols)
                o_ref[:, cols] = _dot(h_ref[...], wb_ref[:, cols]).astype(o_ref.dtype)
        else:
            o_ref[...] = _dot(h_ref[...], w_ref[...]).astype(o_ref.dtype)


def _norm_matmul(x, g, w, out_dtype, *, tm, tn, name, group_tiles=1, col_scale=None,
                 side_weights=()):
    m, d = x.shape
    n = w.shape[1]
    tm, tn = min(tm, m), min(tn, n)
    gt = group_tiles
    grid = (m // (tm * gt), gt + n // tn - 1)
    side = _SideCasts(side_weights, grid)
    cast_w = w.dtype != BF16
    assert cast_w or col_scale is None

    def col_tile(s):
        return jnp.maximum(s - (gt - 1), 0)

    weight_ins, weight_specs = [w], [pl.BlockSpec((d, tn), lambda gi, s: (0, col_tile(s)))]
    if cast_w:
        cs = jnp.ones((n,), F32) if col_scale is None else col_scale
        weight_ins.append(cs.reshape(1, n))
        weight_specs.append(pl.BlockSpec((1, tn), lambda gi, s: (0, col_tile(s))))
    scratch = [pltpu.VMEM((gt * tm, d), BF16)] + ([pltpu.VMEM((d, tn), BF16)] if cast_w else [])
    limit = _vmem_limit(
        [_nbytes((tm, d), x.dtype), _nbytes((d, tn), w.dtype), _nbytes((gt * tm, tn), out_dtype),
         _nbytes((8, d), F32), _nbytes((8, tn), F32)]
        + side.window_bytes,
        [_nbytes((gt * tm, d), BF16), _nbytes((d, tn), BF16) * cast_w],
    )
    outs = pl.pallas_call(
        functools.partial(_norm_matmul_kernel, n_side=len(side), cast_w=cast_w, group_tiles=gt),
        grid=grid,
        in_specs=[
            pl.BlockSpec((tm, d), lambda gi, s: (gi * gt + jnp.minimum(s, gt - 1), 0)),
            pl.BlockSpec((1, d), lambda gi, s: (0, 0)),
        ] + weight_specs + side.in_specs,
        out_specs=[pl.BlockSpec((gt * tm, tn), lambda gi, s: (gi, col_tile(s)))] + side.out_specs,
        out_shape=[jax.ShapeDtypeStruct((m, n), out_dtype)] + side.out_shapes,
        scratch_shapes=scratch,
        compiler_params=pltpu.CompilerParams(
            dimension_semantics=("arbitrary", "arbitrary"), vmem_limit_bytes=limit
        ),
        name=name,
    )(x, g.reshape(1, d), *weight_ins, *side_weights)
    return tuple(outs)


def _norm_matmul_pair_kernel(x_ref, g_ref, w1_ref, w2_ref, o1_ref, o2_ref, h_ref):
    @pl.when(pl.program_id(1) == 0)
    def _():
        h_ref[...] = (_rms(x_ref[...]) * g_ref[...]).astype(h_ref.dtype)

    for w_ref, o_ref in ((w1_ref, o1_ref), (w2_ref, o2_ref)):
        for cols in _col_chunks(w_ref.shape[1]):
            o_ref[:, cols] = _dot(h_ref[...], w_ref[:, cols].astype(BF16)).astype(o_ref.dtype)


def _norm_matmul_pair(x, g, w1, w2, out_dtype, *, tm, tn, name):
    m, d = x.shape
    n = w1.shape[1]
    assert w1.shape == w2.shape
    tm, tn = min(tm, m), min(tn, n)
    w_spec = pl.BlockSpec((d, tn), lambda i, j: (0, j))
    o_spec = pl.BlockSpec((tm, tn), lambda i, j: (i, j))
    limit = _vmem_limit(
        [_nbytes((tm, d), x.dtype)] + [_nbytes((d, tn), w1.dtype), _nbytes((tm, tn), out_dtype)] * 2,
        [_nbytes((tm, d), BF16), _nbytes((tm, d), F32)],
    )
    return pl.pallas_call(
        _norm_matmul_pair_kernel,
        grid=(m // tm, n // tn),
        in_specs=[pl.BlockSpec((tm, d), lambda i, j: (i, 0)), pl.BlockSpec((1, d), lambda i, j: (0, 0)),
                  w_spec, w_spec],
        out_specs=[o_spec, o_spec],
        out_shape=[jax.ShapeDtypeStruct((m, n), out_dtype)] * 2,
        scratch_shapes=[pltpu.VMEM((tm, d), BF16)],
        compiler_params=pltpu.CompilerParams(
            dimension_semantics=("arbitrary", "arbitrary"), vmem_limit_bytes=limit
        ),
        name=name,
    )(x, g.reshape(1, d), w1, w2)


RES_RING_SLOTS = 3


def _matmul2_res_kernel(a1_ref, a2_ref, w1_ref, w2_ref, res_hbm, o_ref, ring_ref, sem_ref):
    step, n_steps = pl.program_id(0), pl.num_programs(0)
    tm = o_ref.shape[0]

    def res_copy(tile):
        slot = lax.rem(tile, RES_RING_SLOTS)
        rows = pl.ds(pl.multiple_of(tile * tm, tm), tm)
        return pltpu.make_async_copy(res_hbm.at[rows, :], ring_ref.at[slot], sem_ref.at[slot])

    @pl.when(step == 0)
    def _():
        for tile in range(RES_RING_SLOTS - 1):
            res_copy(tile).start()

    @pl.when(step + (RES_RING_SLOTS - 1) < n_steps)
    def _():
        res_copy(step + (RES_RING_SLOTS - 1)).start()

    acc = _dot(a1_ref[...], w1_ref[...]) + _dot(a2_ref[...], w2_ref[...])
    res_copy(step).wait()
    o_ref[...] = ring_ref[lax.rem(step, RES_RING_SLOTS)] + acc


def _matmul2_res(a1, a2, w, res, *, tm, name):
    m, k1 = a1.shape
    k2 = a2.shape[1]
    assert k1 == k2 and w.shape[0] == k1 + k2
    n = w.shape[1]
    assert m // tm >= RES_RING_SLOTS - 1
    limit = _vmem_limit(
        [_nbytes((tm, k1), a1.dtype), _nbytes((tm, k2), a2.dtype), _nbytes((tm, n), F32)],
        [_nbytes(w.shape, w.dtype), _nbytes((RES_RING_SLOTS, tm, n), F32), _nbytes((tm, n), F32)],
    )
    resident = pl.Buffered(1)
    return pl.pallas_call(
        _matmul2_res_kernel,
        grid=(m // tm,),
        in_specs=[
            pl.BlockSpec((tm, k1), lambda i: (i, 0)),
            pl.BlockSpec((tm, k2), lambda i: (i, 0)),
            pl.BlockSpec((k1, n), lambda i: (0, 0), pipeline_mode=resident),
            pl.BlockSpec((k2, n), lambda i: (1, 0), pipeline_mode=resident),
            pl.BlockSpec(memory_space=pl.ANY),
        ],
        out_specs=pl.BlockSpec((tm, n), lambda i: (i, 0)),
        out_shape=jax.ShapeDtypeStruct((m, n), F32),
        scratch_shapes=[pltpu.VMEM((RES_RING_SLOTS, tm, n), F32),
                        pltpu.SemaphoreType.DMA((RES_RING_SLOTS,))],
        compiler_params=pltpu.CompilerParams(
            dimension_semantics=("arbitrary",), vmem_limit_bytes=limit
        ),
        name=name,
    )(a1, a2, w, w, res)


def _bf16_part(x):
    bits = lax.bitcast_convert_type(x, jnp.int32) & jnp.int32(-65536)
    return lax.bitcast_convert_type(bits, F32)


def _diff_attn_body(ins, o_ref, scratch, *, tq, lam_init):
    slope_ref, lq1_ref, lk1_ref, lq2_ref, lk2_ref, g_ref, q_ref, k_ref, v_ref = ins
    kx_ref, s_ref, p_ref = scratch
    s_len = q_ref.shape[0]
    d = DA_HEAD_DIM
    lam = (jnp.exp(jnp.sum(lq1_ref[...] * lk1_ref[...], axis=-1, keepdims=True))
           - jnp.exp(jnp.sum(lq2_ref[...] * lk2_ref[...], axis=-1, keepdims=True))
           + lam_init)

    lane = lax.broadcasted_iota(jnp.int32, (1, d), 1)
    slope2 = jnp.full((1, d), slope_ref[pl.program_id(1)] * LOG2_E, F32)
    piece_hi = _bf16_part(slope2)
    rest = slope2 - piece_hi
    piece_mid = _bf16_part(rest)
    piece_lo = _bf16_part(rest - piece_mid)
    piece = jnp.where((lane == 0) | (lane == 3), piece_hi,
                      jnp.where((lane == 1) | (lane == 4), piece_mid, piece_lo))
    q_extra = jnp.where(lane < 3, piece * 256.0, jnp.where(lane < 6, piece, 0.0))
    q_extra = jnp.broadcast_to(q_extra, (tq, d)).astype(BF16)
    kpos = lax.broadcasted_iota(jnp.int32, (s_len, d), 0)
    klane = lax.broadcasted_iota(jnp.int32, (s_len, d), 1)
    k_extra = jnp.where(klane < 3, kpos >> 8, jnp.where(klane < 6, kpos & 255, 0))
    kx_ref[...] = k_extra.astype(F32).astype(kx_ref.dtype)

    row = lax.broadcasted_iota(jnp.int32, (tq, tq), 0)
    col = lax.broadcasted_iota(jnp.int32, (tq, tq), 1)
    causal = col <= row

    for qi in reversed(range(s_len // tq)):
        lo, hi = qi * tq, (qi + 1) * tq
        key_blocks = [slice(j * tq, (j + 1) * tq) for j in range(qi + 1)]
        heads = []
        for c in range(2):
            dcols = slice(c * d, (c + 1) * d)
            q_aug = jnp.concatenate([q_ref[lo:hi, dcols], q_extra], axis=1)
            m = None
            for j, cols in enumerate(key_blocks):
                k_aug = jnp.concatenate([k_ref[cols, dcols], kx_ref[cols, :]], axis=1)
                s = _dot_nt(q_aug, k_aug)
                if j == qi:
                    s = jnp.where(causal, s, NEG_INF)
                s_ref[c, :, cols] = s
                bm = _fold_lanes(s, jnp.maximum)
                m = bm if m is None else jnp.maximum(m, bm)
            m = jnp.max(m, axis=-1, keepdims=True)
            l = None
            for cols in key_blocks:
                p = jnp.exp2(s_ref[c, :, cols] - m)
                bl = _fold_lanes(p, jnp.add)
                l = bl if l is None else l + bl
                p_ref[c, :, cols] = p.astype(p_ref.dtype)
            l = jnp.sum(l, axis=-1, keepdims=True)
            acc = _dot(p_ref[c, :, 0:hi], v_ref[0:hi, :])
            heads.append((acc, l))
        (acc1, l1), (acc2, l2) = heads
        out = acc1 * (1.0 / l1) - acc2 * (lam / l2)
        y = _rms(out) * g_ref[...] * (1.0 - lam_init)
        o_ref[lo:hi, :] = y.astype(o_ref.dtype)
        yield


def _retention_body(ins, o_ref):
    lg_ref, q_ref, k_ref, v_ref, gate_ref = ins
    s_len = q_ref.shape[0]
    c = RET_CHUNK
    dk, dv = RET_QK_DIM, RET_V_DIM
    scale = dk ** -0.5
    lg = lg_ref[pl.program_id(1)]

    row = lax.broadcasted_iota(jnp.int32, (c, c), 0)
    col = lax.broadcasted_iota(jnp.int32, (c, c), 1)
    diff = (row - col).astype(F32)
    intra = jnp.where(diff >= 0, jnp.exp(lg * jnp.maximum(diff, 0.0)), 0.0) * scale
    row_k = lax.broadcasted_iota(jnp.int32, (c, dk), 0).astype(F32)
    k_decay = jnp.exp(lg * (float(c - 1) - row_k)) * scale
    row_v = lax.broadcasted_iota(jnp.int32, (c, dv), 0).astype(F32)
    q_decay = jnp.exp(lg * (row_v + 1.0))
    chunk_decay = jnp.exp(jnp.full((1, dv), lg * float(c), F32))

    state = jnp.zeros((dk, dv), F32)
    for i in range(s_len // c):
        lo, hi = i * c, (i + 1) * c
        q = q_ref[lo:hi, :]
        k = k_ref[lo:hi, :]
        v = v_ref[lo:hi, :]
        scores = _dot_nt(q, k) * intra
        y = _dot(scores.astype(BF16), v)
        if i > 0:
            y = y + _dot(q, state.astype(BF16)) * q_decay
        if i + 1 < s_len // c:
            kd = (k.astype(F32) * k_decay).astype(BF16)
            state = state * chunk_decay + _dot_tn(kd, v)
        gate = gate_ref[lo:hi, :].astype(F32)
        o_ref[lo:hi, :] = (_rms(y) * (gate * jax.nn.sigmoid(gate))).astype(o_ref.dtype)
        yield


N_ATTN_INS, N_RET_INS = 9, 5


def _mixers_kernel(*refs, n_side, tq, lam_init):
    ins, rest = refs[:N_ATTN_INS + N_RET_INS], refs[N_ATTN_INS + N_RET_INS:]
    side_in, rest = rest[:n_side], rest[n_side:]
    da_ref, ret_ref = rest[:2]
    side_out, scratch = rest[2:2 + n_side], rest[2 + n_side:]
    _cast_blocks(zip(side_in, side_out))
    pending = [_retention_body(ins[N_ATTN_INS:], ret_ref),
               _diff_attn_body(ins[:N_ATTN_INS], da_ref, scratch, tq=tq, lam_init=lam_init)]
    while pending:
        pending = [body for body in pending if next(body, StopIteration) is not StopIteration]


def _mixers(proj, slopes, lq1, lk1, lq2, lk2, g, log_gammas, *, batch, seq, lam_init, tq,
            side_weights=()):
    assert DA_HEADS == RET_HEADS
    hb = DA_V_DIM
    q_blk0, k_blk0, v_blk0 = 0, DA_HEADS, 2 * DA_HEADS
    da_cols = 3 * DA_HEADS * DA_V_DIM
    rq_blk0 = da_cols // RET_QK_DIM
    rk_blk0 = rq_blk0 + RET_HEADS
    rv_blk0 = (da_cols + 2 * RET_HEADS * RET_QK_DIM) // RET_V_DIM
    rg_blk0 = rv_blk0 + RET_HEADS
    grid = (batch, DA_HEADS)
    side = _SideCasts(side_weights, grid)
    scratch = [
        pltpu.VMEM((seq, DA_HEAD_DIM), BF16),
        pltpu.VMEM((2, tq, seq), F32),
        pltpu.VMEM((2, tq, seq), BF16),
    ]
    limit = _vmem_limit(
        [_nbytes((seq, hb), proj.dtype)] * 4
        + [_nbytes((seq, RET_QK_DIM), proj.dtype)] * 2 + [_nbytes((seq, RET_V_DIM), proj.dtype)] * 3
        + side.window_bytes,
        [_nbytes((seq, DA_HEAD_DIM), BF16),
         3 * _nbytes((2, tq, seq), F32), 3 * _nbytes((2, tq, seq), BF16)],
    )
    vec = pl.BlockSpec((1, DA_HEAD_DIM), lambda b, h: (0, 0))
    smem = pl.BlockSpec(memory_space=pltpu.SMEM)

    def head_cols(width, blk0):
        return pl.BlockSpec((seq, width), lambda b, h: (b, blk0 + h))

    outs = pl.pallas_call(
        functools.partial(_mixers_kernel, n_side=len(side), tq=tq, lam_init=lam_init),
        grid=grid,
        in_specs=[
            smem, vec, vec, vec, vec,
            pl.BlockSpec((1, hb), lambda b, h: (0, 0)),
            head_cols(hb, q_blk0), head_cols(hb, k_blk0), head_cols(hb, v_blk0),
            smem,
            head_cols(RET_QK_DIM, rq_blk0), head_cols(RET_QK_DIM, rk_blk0),
            head_cols(RET_V_DIM, rv_blk0), head_cols(RET_V_DIM, rg_blk0),
        ] + side.in_specs,
        out_specs=[head_cols(hb, 0), head_cols(RET_V_DIM, 0)] + side.out_specs,
        out_shape=[jax.ShapeDtypeStruct((batch * seq, DA_HEADS * hb), BF16),
                   jax.ShapeDtypeStruct((batch * seq, RET_HEADS * RET_V_DIM), BF16)] + side.out_shapes,
        scratch_shapes=scratch,
        compiler_params=pltpu.CompilerParams(
            dimension_semantics=("arbitrary", "arbitrary"), vmem_limit_bytes=limit
        ),
        name="token_mixers",
    )(slopes, lq1.reshape(1, -1), lk1.reshape(1, -1), lq2.reshape(1, -1), lk2.reshape(1, -1),
      g.reshape(1, hb), proj, proj, proj, log_gammas, proj, proj, proj, proj, *side_weights)
    return tuple(outs)


def _xattn_kernel(*refs, n_side):
    ins, o_ref, (xo_ref,), side = _split_refs(refs, 5, n_side)
    xq_ref, xk_ref, xv_ref, res_ref, wo_ref = ins
    _cast_blocks(side)
    d_model = xq_ref.shape[1]
    hd = d_model // XATTN_HEADS
    scale2 = hd ** -0.5 * LOG2_E
    for h in range(XATTN_HEADS):
        cols = slice(h * hd, (h + 1) * hd)
        s = _dot_nt(xq_ref[:, cols], xk_ref[:, cols]) * scale2
        p = jnp.exp2(s - jnp.max(s, axis=-1, keepdims=True))
        p = p * (1.0 / jnp.sum(p, axis=-1, keepdims=True))
        xo_ref[:, cols] = _dot(p.astype(BF16), xv_ref[:, cols]).astype(xo_ref.dtype)
    o_ref[...] = res_ref[...] + _dot(xo_ref[...], wo_ref[...])


def _cross_attention(xq, xk, xv, res, wo, *, batch, seq, mem_len, tq, side_weights=()):
    d = xq.shape[1]
    nq = seq // tq
    grid = (batch, nq)
    side = _SideCasts(side_weights, grid)
    limit = _vmem_limit(
        [
            _nbytes((tq, d), xq.dtype),
            _nbytes((mem_len, d), xk.dtype),
            _nbytes((mem_len, d), xv.dtype),
            _nbytes((tq, d), F32),
            _nbytes((d, d), wo.dtype),
            _nbytes((tq, d), F32),
        ] + side.window_bytes,
        [_nbytes((tq, d), BF16), _nbytes((tq, d), F32)],
    )
    outs = pl.pallas_call(
        functools.partial(_xattn_kernel, n_side=len(side)),
        grid=grid,
        in_specs=[
            pl.BlockSpec((tq, d), lambda b, i: (b * nq + i, 0)),
            pl.BlockSpec((mem_len, d), lambda b, i: (b, 0)),
            pl.BlockSpec((mem_len, d), lambda b, i: (b, 0)),
            pl.BlockSpec((tq, d), lambda b, i: (b * nq + i, 0)),
            pl.BlockSpec((d, d), lambda b, i: (0, 0)),
        ] + side.in_specs,
        out_specs=[pl.BlockSpec((tq, d), lambda b, i: (b * nq + i, 0))] + side.out_specs,
        out_shape=[jax.ShapeDtypeStruct((batch * seq, d), F32)] + side.out_shapes,
        scratch_shapes=[pltpu.VMEM((tq, d), BF16)],
        compiler_params=pltpu.CompilerParams(
            dimension_semantics=("arbitrary", "arbitrary"), vmem_limit_bytes=limit
        ),
        name="cross_attention",
    )(xq, xk, xv, res, wo, *side_weights)
    return tuple(outs)


def _ffn_kernel(x_ref, g_ref, wg_ref, wu_ref, wd_ref, gf_ref, o_ref, h_ref, *, final_norm):
    f = pl.program_id(1)
    last = pl.num_programs(1) - 1
    blocks = _row_blocks(x_ref.shape[0])

    def partial_ffn(h, chunked=False):
        hidden = wg_ref.shape[1]
        out = None
        for cols in (_col_chunks(hidden) if chunked else [slice(0, hidden)]):
            gate = _dot(h, wg_ref[:, cols])
            up = _dot(h, wu_ref[:, cols])
            act = (gate * jax.nn.sigmoid(gate)) * up
            part = _dot(act.astype(BF16), wd_ref[cols, :])
            out = part if out is None else out + part
        return out

    @pl.when(f == 0)
    def _():
        for rows in blocks:
            x = x_ref[rows, :]
            h = (_rms(x) * g_ref[...]).astype(h_ref.dtype)
            h_ref[rows, :] = h
            o_ref[rows, :] = x + partial_ffn(h)

    if final_norm:
        @pl.when(jnp.logical_and(f > 0, f < last))
        def _():
            o_ref[...] += partial_ffn(h_ref[...], chunked=True)

        @pl.when(f == last)
        def _():
            for rows in blocks:
                y = o_ref[rows, :] + partial_ffn(h_ref[rows, :])
                o_ref[rows, :] = _rms(y) * gf_ref[...]
    else:
        @pl.when(f > 0)
        def _():
            o_ref[...] += partial_ffn(h_ref[...], chunked=True)


def _ffn(x, g, wg, wu, wd, gf, *, final_norm, tm, tf):
    m, d = x.shape
    d_ff = wg.shape[1]
    limit = _vmem_limit(
        [
            _nbytes((tm, d), F32),
            _nbytes((d, tf), wg.dtype),
            _nbytes((d, tf), wu.dtype),
            _nbytes((tf, d), wd.dtype),
            _nbytes((tm, d), F32),
        ],
        [_nbytes((tm, d), BF16), _nbytes((tm, d), F32), 4 * _nbytes((tm, tf), F32)],
    )
    return pl.pallas_call(
        functools.partial(_ffn_kernel, final_norm=final_norm),
        grid=(m // tm, d_ff // tf),
        in_specs=[
            pl.BlockSpec((tm, d), lambda i, f: (i, 0)),
            pl.BlockSpec((1, d), lambda i, f: (0, 0)),
            pl.BlockSpec((d, tf), lambda i, f: (0, f)),
            pl.BlockSpec((d, tf), lambda i, f: (0, f)),
            pl.BlockSpec((tf, d), lambda i, f: (f, 0)),
            pl.BlockSpec((1, d), lambda i, f: (0, 0)),
        ],
        out_specs=pl.BlockSpec((tm, d), lambda i, f: (i, 0)),
        out_shape=jax.ShapeDtypeStruct((m, d), F32),
        scratch_shapes=[pltpu.VMEM((tm, d), BF16)],
        compiler_params=pltpu.CompilerParams(
            dimension_semantics=("parallel", "arbitrary"), vmem_limit_bytes=limit
        ),
        name="swiglu_ffn",
    )(x, g.reshape(1, d), wg, wu, wd, gf.reshape(1, d))


def kernel(x, mem, norm_mix_g, w_in, lambda_q1, lambda_k1, lambda_q2, lambda_k2, da_subln_g, w_o, norm_x_g, norm_mem_g, w_xq, w_xk, w_xv, w_xo, norm_ffn_g, w_gate, w_up, w_down, norm_f_g):
    batch, seq, d_model = x.shape
    mem_len = mem.shape[1]
    depth = w_in.shape[0]
    slopes = jnp.asarray(2.0 ** (-8.0 * np.arange(1, DA_HEADS + 1) / DA_HEADS), dtype=F32)
    log_gammas = jnp.asarray(np.log(1.0 - 2.0 ** (-5.0 - np.arange(RET_HEADS))), dtype=F32)

    xf = x.reshape(batch * seq, d_model)
    memf = mem.reshape(batch * mem_len, d_model)
    for l in range(depth):
        lam_init = 0.8 - 0.6 * math.exp(-0.3 * l)
        n_dq = DA_HEADS * 2 * DA_HEAD_DIM
        col_scale = jnp.where(jnp.arange(w_in.shape[2]) < n_dq, DA_HEAD_DIM ** -0.5 * LOG2_E, 1.0)
        proj, = _norm_matmul(
            xf, norm_mix_g[l], w_in[l], BF16, tm=ROW_TILE, tn=COL_TILE,
            group_tiles=IN_PROJ_GROUP_TILES, name="in_proj", col_scale=col_scale.astype(F32))
        da, ret, wb_up, wb_o, wb_xq, wb_xo = _mixers(
            proj, slopes, lambda_q1[l], lambda_k1[l], lambda_q2[l], lambda_k2[l], da_subln_g[l],
            log_gammas, batch=batch, seq=seq, lam_init=lam_init, tq=ATTN_Q_TILE,
            side_weights=(w_up[l], w_o[l], w_xq[l], w_xo[l]))
        xf = _matmul2_res(da, ret, wb_o, xf, tm=OUT_PROJ_ROW_TILE, name="out_proj")

        xq, wb_gate = _norm_matmul(xf, norm_x_g[l], wb_xq, BF16, tm=ROW_TILE, tn=d_model,
                                   name="xattn_q", side_weights=(w_gate[l],))
        xk, xv = _norm_matmul_pair(memf, norm_mem_g[l], w_xk[l], w_xv[l], BF16, tm=ROW_TILE,
                                   tn=KV_COL_TILE, name="xattn_kv")
        xf, wb_down = _cross_attention(xq, xk, xv, xf, wb_xo, batch=batch, seq=seq, mem_len=mem_len,
                                       tq=XATTN_Q_TILE, side_weights=(w_down[l],))

        xf = _ffn(xf, norm_ffn_g[l], wb_gate, wb_up, wb_down, norm_f_g,
                  final_norm=(l == depth - 1), tm=FFN_ROW_TILE, tf=FFN_COL_TILE)
    return xf.reshape(batch, seq, d_model)
```

```python
import functools
import math

import jax
import jax.numpy as jnp
import numpy as np
from jax import lax
from jax.experimental import pallas as pl
from jax.experimental.pallas import tpu as pltpu

F32 = jnp.float32
BF16 = jnp.bfloat16

DA_HEADS = 4
DA_HEAD_DIM = 128
DA_V_DIM = 2 * DA_HEAD_DIM
RET_HEADS = 4
RET_QK_DIM = 128
RET_V_DIM = 256
XATTN_HEADS = 4
RET_CHUNK = 256
NORM_EPS = 1e-6
NEG_INF = -1e30
LOG2_E = math.log2(math.e)
NORM_BLOCK_ROWS = 256

ROW_TILE = 1024
COL_TILE = 1024
KV_COL_TILE = 512
IN_PROJ_GROUP_TILES = 2
OUT_PROJ_ROW_TILE = 512
ATTN_Q_TILE = 512
XATTN_Q_TILE = 512
FFN_ROW_TILE, FFN_COL_TILE = 1024, 512

V7X_LANES = 128
MXU_COLS = 256
BF16_TILE_ROWS = 16
V7X_VMEM_BYTES = 64 * 1024 * 1024
V7X_VMEM_USABLE_BYTES = V7X_VMEM_BYTES - 6 * 1024 * 1024
COMPILER_SCRATCH_BYTES = 4 * 1024 * 1024


def _nbytes(shape, dtype):
    return int(np.prod(shape)) * jnp.dtype(dtype).itemsize


def _vmem_limit(pipelined, resident):
    need = 2 * sum(pipelined) + sum(resident) + COMPILER_SCRATCH_BYTES
    return int(min(V7X_VMEM_USABLE_BYTES, need))


def _rms(x):
    return x * lax.rsqrt(jnp.mean(x * x, axis=-1, keepdims=True) + NORM_EPS)


def _fold_lanes(x, op):
    tiles = [x[:, i:i + V7X_LANES] for i in range(0, x.shape[1], V7X_LANES)]
    return functools.reduce(op, tiles)


def _dot(a, b):
    return jnp.dot(a, b, preferred_element_type=F32)


def _dot_nt(a, b):
    return lax.dot_general(a, b, (((1,), (1,)), ((), ())), preferred_element_type=F32)


def _dot_tn(a, b):
    return lax.dot_general(a, b, (((0,), (0,)), ((), ())), preferred_element_type=F32)


class _SideCasts:
    def __init__(self, weights, grid):
        self.weights = list(weights)
        self.grid = tuple(grid)
        n_steps = int(np.prod(self.grid))
        self.plans = []
        for w in self.weights:
            rows, n_blocks = w.shape[0], n_steps
            while rows % n_blocks or (rows // n_blocks) % BF16_TILE_ROWS:
                n_blocks -= 1
            self.plans.append((n_blocks, rows // n_blocks))

    def __len__(self):
        return len(self.weights)

    def _specs(self):
        specs = []
        for w, (n_blocks, block_rows) in zip(self.weights, self.plans):
            def index(*ids, n_blocks=n_blocks):
                step = ids[0]
                for extent, idx in zip(self.grid[1:], ids[1:]):
                    step = step * extent + idx
                return (jnp.minimum(step, n_blocks - 1), 0)
            specs.append(pl.BlockSpec((block_rows, w.shape[1]), index))
        return specs

    in_specs = property(_specs)
    out_specs = property(_specs)

    @property
    def out_shapes(self):
        return [jax.ShapeDtypeStruct(w.shape, BF16) for w in self.weights]

    @property
    def window_bytes(self):
        return [_nbytes((rows, w.shape[1]), dt)
                for w, (_, rows) in zip(self.weights, self.plans) for dt in (w.dtype, BF16)]


def _split_refs(refs, n_in, n_side):
    ins, rest = refs[:n_in], refs[n_in:]
    side_in, rest = rest[:n_side], rest[n_side:]
    out, side_out, scratch = rest[0], rest[1:1 + n_side], rest[1 + n_side:]
    return ins, out, scratch, list(zip(side_in, side_out))


def _cast_blocks(pairs):
    for src, dst in pairs:
        dst[...] = src[...].astype(dst.dtype)


def _row_blocks(n_rows):
    step = min(NORM_BLOCK_ROWS, n_rows)
    return [slice(r, r + step) for r in range(0, n_rows, step)]


def _col_chunks(n_cols):
    step = min(MXU_COLS, n_cols)
    return [slice(c, c + step) for c in range(0, n_cols, step)]


def _norm_matmul_kernel(*refs, n_side, cast_w, group_tiles):
    ins, o_ref, scratch, side = _split_refs(refs, 4 if cast_w else 3, n_side)
    x_ref, g_ref, w_ref = ins[:3]
    h_ref = scratch[0]
    wb_ref = scratch[1] if cast_w else w_ref
    step = pl.program_id(1)
    tm = x_ref.shape[0]

    def cast_weight(cols):
        wb_ref[:, cols] = (w_ref[:, cols] * ins[3][:, cols]).astype(wb_ref.dtype)

    for t in range(group_tiles):
        @pl.when(step == t)
        def _(t=t):
            _cast_blocks(side)
            if cast_w and t == 0:
                cast_weight(slice(None))
            for rows in _row_blocks(tm):
                group_rows = slice(t * tm + rows.start, t * tm + rows.stop)
                h = (_rms(x_ref[rows, :]) * g_ref[...]).astype(h_ref.dtype)
                h_ref[group_rows, :] = h
                o_ref[group_rows, :] = _dot(h, wb_ref[...]).astype(o_ref.dtype)

    @pl.when(step >= group_tiles)
    def _():
        _cast_blocks(side)
        if cast_w:
            for cols in _col_chunks(w_ref.shape[1]):
                cast_weight(cols)
                o_ref[:, cols] = _dot(h_ref[...], wb_ref[:, cols]).astype(o_ref.dtype)
        else:
            o_ref[...] = _dot(h_ref[...], w_ref[...]).astype(o_ref.dtype)


def _norm_matmul(x, g, w, out_dtype, *, tm, tn, name, group_tiles=1, col_scale=None,
                 side_weights=()):
    m, d = x.shape
    n = w.shape[1]
    tm, tn = min(tm, m), min(tn, n)
    gt = group_tiles
    grid = (m // (tm * gt), gt + n // tn - 1)
    side = _SideCasts(side_weights, grid)
    cast_w = w.dtype != BF16
    assert cast_w or col_scale is None

    def col_tile(s):
        return jnp.maximum(s - (gt - 1), 0)

    weight_ins, weight_specs = [w], [pl.BlockSpec((d, tn), lambda gi, s: (0, col_tile(s)))]
    if cast_w:
        cs = jnp.ones((n,), F32) if col_scale is None else col_scale
        weight_ins.append(cs.reshape(1, n))
        weight_specs.append(pl.BlockSpec((1, tn), lambda gi, s: (0, col_tile(s))))
    scratch = [pltpu.VMEM((gt * tm, d), BF16)] + ([pltpu.VMEM((d, tn), BF16)] if cast_w else [])
    limit = _vmem_limit(
        [_nbytes((tm, d), x.dtype), _nbytes((d, tn), w.dtype), _nbytes((gt * tm, tn), out_dtype),
         _nbytes((8, d), F32), _nbytes((8, tn), F32)]
        + side.window_bytes,
        [_nbytes((gt * tm, d), BF16), _nbytes((d, tn), BF16) * cast_w],
    )
    outs = pl.pallas_call(
        functools.partial(_norm_matmul_kernel, n_side=len(side), cast_w=cast_w, group_tiles=gt),
        grid=grid,
        in_specs=[
            pl.BlockSpec((tm, d), lambda gi, s: (gi * gt + jnp.minimum(s, gt - 1), 0)),
            pl.BlockSpec((1, d), lambda gi, s: (0, 0)),
        ] + weight_specs + side.in_specs,
        out_specs=[pl.BlockSpec((gt * tm, tn), lambda gi, s: (gi, col_tile(s)))] + side.out_specs,
        out_shape=[jax.ShapeDtypeStruct((m, n), out_dtype)] + side.out_shapes,
        scratch_shapes=scratch,
        compiler_params=pltpu.CompilerParams(
            dimension_semantics=("arbitrary", "arbitrary"), vmem_limit_bytes=limit
        ),
        name=name,
    )(x, g.reshape(1, d), *weight_ins, *side_weights)
    return tuple(outs)


def _norm_matmul_pair_kernel(x_ref, g_ref, w1_ref, w2_ref, o1_ref, o2_ref, h_ref):
    @pl.when(pl.program_id(1) == 0)
    def _():
        h_ref[...] = (_rms(x_ref[...]) * g_ref[...]).astype(h_ref.dtype)

    for w_ref, o_ref in ((w1_ref, o1_ref), (w2_ref, o2_ref)):
        for cols in _col_chunks(w_ref.shape[1]):
            o_ref[:, cols] = _dot(h_ref[...], w_ref[:, cols].astype(BF16)).astype(o_ref.dtype)


def _norm_matmul_pair(x, g, w1, w2, out_dtype, *, tm, tn, name):
    m, d = x.shape
    n = w1.shape[1]
    assert w1.shape == w2.shape
    tm, tn = min(tm, m), min(tn, n)
    w_spec = pl.BlockSpec((d, tn), lambda i, j: (0, j))
    o_spec = pl.BlockSpec((tm, tn), lambda i, j: (i, j))
    limit = _vmem_limit(
        [_nbytes((tm, d), x.dtype)] + [_nbytes((d, tn), w1.dtype), _nbytes((tm, tn), out_dtype)] * 2,
        [_nbytes((tm, d), BF16), _nbytes((tm, d), F32)],
    )
    return pl.pallas_call(
        _norm_matmul_pair_kernel,
        grid=(m // tm, n // tn),
        in_specs=[pl.BlockSpec((tm, d), lambda i, j: (i, 0)), pl.BlockSpec((1, d), lambda i, j: (0, 0)),
                  w_spec, w_spec],
        out_specs=[o_spec, o_spec],
        out_shape=[jax.ShapeDtypeStruct((m, n), out_dtype)] * 2,
        scratch_shapes=[pltpu.VMEM((tm, d), BF16)],
        compiler_params=pltpu.CompilerParams(
            dimension_semantics=("arbitrary", "arbitrary"), vmem_limit_bytes=limit
        ),
        name=name,
    )(x, g.reshape(1, d), w1, w2)


def _matmul2_res_kernel(a1_ref, a2_ref, w1_ref, w2_ref, res_ref, o_ref):
    acc = _dot(a1_ref[...], w1_ref[...]) + _dot(a2_ref[...], w2_ref[...])
    o_ref[...] = res_ref[...] + acc


def _matmul2_res(a1, a2, w, res, *, tm, name):
    m, k1 = a1.shape
    k2 = a2.shape[1]
    assert k1 == k2 and w.shape[0] == k1 + k2
    n = w.shape[1]
    limit = _vmem_limit(
        [_nbytes((tm, k1), a1.dtype), _nbytes((tm, k2), a2.dtype),
         _nbytes((tm, n), F32), _nbytes((tm, n), F32)],
        [_nbytes(w.shape, w.dtype), _nbytes((tm, n), F32)],
    )
    resident = pl.Buffered(1)
    return pl.pallas_call(
        _matmul2_res_kernel,
        grid=(m // tm,),
        in_specs=[
            pl.BlockSpec((tm, k1), lambda i: (i, 0)),
            pl.BlockSpec((tm, k2), lambda i: (i, 0)),
            pl.BlockSpec((k1, n), lambda i: (0, 0), pipeline_mode=resident),
            pl.BlockSpec((k2, n), lambda i: (1, 0), pipeline_mode=resident),
            pl.BlockSpec((tm, n), lambda i: (i, 0)),
        ],
        out_specs=pl.BlockSpec((tm, n), lambda i: (i, 0)),
        out_shape=jax.ShapeDtypeStruct((m, n), F32),
        compiler_params=pltpu.CompilerParams(
            dimension_semantics=("arbitrary",), vmem_limit_bytes=limit
        ),
        name=name,
    )(a1, a2, w, w, res)


def _out_proj_xq_kernel(*refs, n_side):
    ins, rest = refs[:7], refs[7:]
    a1_ref, a2_ref, w1_ref, w2_ref, res_ref, g_ref, wq_ref = ins
    side_in, rest = rest[:n_side], rest[n_side:]
    x1_ref, xq_ref = rest[:2]
    side_out = rest[2:2 + n_side]
    _cast_blocks(zip(side_in, side_out))
    for rows in _row_blocks(x1_ref.shape[0]):
        x1 = res_ref[rows, :] + _dot(a1_ref[rows, :], w1_ref[...]) + _dot(a2_ref[rows, :], w2_ref[...])
        x1_ref[rows, :] = x1
        h = (_rms(x1) * g_ref[...]).astype(BF16)
        xq_ref[rows, :] = _dot(h, wq_ref[...]).astype(xq_ref.dtype)


def _out_proj_xq(a1, a2, wo, res, g, wq, *, tm, side_weights=()):
    m, k1 = a1.shape
    k2 = a2.shape[1]
    n = wo.shape[1]
    assert k1 == k2 and wo.shape[0] == k1 + k2 and wq.shape == (n, n)
    grid = (m // tm,)
    side = _SideCasts(side_weights, grid)
    limit = _vmem_limit(
        [_nbytes((tm, k1), a1.dtype), _nbytes((tm, k2), a2.dtype), _nbytes((tm, n), F32),
         _nbytes((tm, n), F32), _nbytes((tm, n), BF16)] + side.window_bytes,
        [_nbytes(wo.shape, wo.dtype), _nbytes(wq.shape, wq.dtype), 2 * _nbytes((tm, n), F32)],
    )
    resident = pl.Buffered(1)
    row_tile = lambda width: pl.BlockSpec((tm, width), lambda i: (i, 0))
    outs = pl.pallas_call(
        functools.partial(_out_proj_xq_kernel, n_side=len(side)),
        grid=grid,
        in_specs=[
            row_tile(k1), row_tile(k2),
            pl.BlockSpec((k1, n), lambda i: (0, 0), pipeline_mode=resident),
            pl.BlockSpec((k2, n), lambda i: (1, 0), pipeline_mode=resident),
            row_tile(n),
            pl.BlockSpec((1, n), lambda i: (0, 0)),
            pl.BlockSpec((n, n), lambda i: (0, 0), pipeline_mode=resident),
        ] + side.in_specs,
        out_specs=[row_tile(n), row_tile(n)] + side.out_specs,
        out_shape=[jax.ShapeDtypeStruct((m, n), F32), jax.ShapeDtypeStruct((m, n), BF16)]
        + side.out_shapes,
        compiler_params=pltpu.CompilerParams(
            dimension_semantics=("arbitrary",), vmem_limit_bytes=limit
        ),
        name="out_proj_xq",
    )(a1, a2, wo, wo, res, g.reshape(1, n), wq, *side_weights)
    return tuple(outs)


def _bf16_part(x):
    bits = lax.bitcast_convert_type(x, jnp.int32) & jnp.int32(-65536)
    return lax.bitcast_convert_type(bits, F32)


def _diff_attn_body(ins, o_ref, scratch, *, tq, lam_init):
    slope_ref, lq1_ref, lk1_ref, lq2_ref, lk2_ref, g_ref, q_ref, k_ref, v_ref = ins
    kx_ref, s_ref, p_ref = scratch
    s_len = q_ref.shape[0]
    d = DA_HEAD_DIM
    lam = (jnp.exp(jnp.sum(lq1_ref[...] * lk1_ref[...], axis=-1, keepdims=True))
           - jnp.exp(jnp.sum(lq2_ref[...] * lk2_ref[...], axis=-1, keepdims=True))
           + lam_init)

    lane = lax.broadcasted_iota(jnp.int32, (1, d), 1)
    slope2 = jnp.full((1, d), slope_ref[pl.program_id(1)] * LOG2_E, F32)
    piece_hi = _bf16_part(slope2)
    rest = slope2 - piece_hi
    piece_mid = _bf16_part(rest)
    piece_lo = _bf16_part(rest - piece_mid)
    piece = jnp.where((lane == 0) | (lane == 3), piece_hi,
                      jnp.where((lane == 1) | (lane == 4), piece_mid, piece_lo))
    q_extra = jnp.where(lane < 3, piece * 256.0, jnp.where(lane < 6, piece, 0.0))
    q_extra = jnp.broadcast_to(q_extra, (tq, d)).astype(BF16)
    kpos = lax.broadcasted_iota(jnp.int32, (s_len, d), 0)
    klane = lax.broadcasted_iota(jnp.int32, (s_len, d), 1)
    k_extra = jnp.where(klane < 3, kpos >> 8, jnp.where(klane < 6, kpos & 255, 0))
    kx_ref[...] = k_extra.astype(F32).astype(kx_ref.dtype)

    row = lax.broadcasted_iota(jnp.int32, (tq, tq), 0)
    col = lax.broadcasted_iota(jnp.int32, (tq, tq), 1)
    causal = col <= row

    for qi in reversed(range(s_len // tq)):
        lo, hi = qi * tq, (qi + 1) * tq
        key_blocks = [slice(j * tq, (j + 1) * tq) for j in range(qi + 1)]
        heads = []
        for c in range(2):
            dcols = slice(c * d, (c + 1) * d)
            q_aug = jnp.concatenate([q_ref[lo:hi, dcols], q_extra], axis=1)
            m = None
            for j, cols in enumerate(key_blocks):
                k_aug = jnp.concatenate([k_ref[cols, dcols], kx_ref[cols, :]], axis=1)
                s = _dot_nt(q_aug, k_aug)
                if j == qi:
                    s = jnp.where(causal, s, NEG_INF)
                s_ref[c, :, cols] = s
                bm = _fold_lanes(s, jnp.maximum)
                m = bm if m is None else jnp.maximum(m, bm)
            m = jnp.max(m, axis=-1, keepdims=True)
            l = None
            for cols in key_blocks:
                p = jnp.exp2(s_ref[c, :, cols] - m)
                bl = _fold_lanes(p, jnp.add)
                l = bl if l is None else l + bl
                p_ref[c, :, cols] = p.astype(p_ref.dtype)
            l = jnp.sum(l, axis=-1, keepdims=True)
            acc = _dot(p_ref[c, :, 0:hi], v_ref[0:hi, :])
            heads.append((acc, l))
        (acc1, l1), (acc2, l2) = heads
        out = acc1 * (1.0 / l1) - acc2 * (lam / l2)
        y = _rms(out) * g_ref[...] * (1.0 - lam_init)
        o_ref[lo:hi, :] = y.astype(o_ref.dtype)
        yield


def _retention_body(ins, o_ref):
    lg_ref, q_ref, k_ref, v_ref, gate_ref = ins
    s_len = q_ref.shape[0]
    c = RET_CHUNK
    dk, dv = RET_QK_DIM, RET_V_DIM
    scale = dk ** -0.5
    lg = lg_ref[pl.program_id(1)]

    row = lax.broadcasted_iota(jnp.int32, (c, c), 0)
    col = lax.broadcasted_iota(jnp.int32, (c, c), 1)
    diff = (row - col).astype(F32)
    intra = jnp.where(diff >= 0, jnp.exp(lg * jnp.maximum(diff, 0.0)), 0.0) * scale
    row_k = lax.broadcasted_iota(jnp.int32, (c, dk), 0).astype(F32)
    k_decay = jnp.exp(lg * (float(c - 1) - row_k)) * scale
    row_v = lax.broadcasted_iota(jnp.int32, (c, dv), 0).astype(F32)
    q_decay = jnp.exp(lg * (row_v + 1.0))
    chunk_decay = jnp.exp(jnp.full((1, dv), lg * float(c), F32))

    state = jnp.zeros((dk, dv), F32)
    for i in range(s_len // c):
        lo, hi = i * c, (i + 1) * c
        q = q_ref[lo:hi, :]
        k = k_ref[lo:hi, :]
        v = v_ref[lo:hi, :]
        scores = _dot_nt(q, k) * intra
        y = _dot(scores.astype(BF16), v)
        if i > 0:
            y = y + _dot(q, state.astype(BF16)) * q_decay
        if i + 1 < s_len // c:
            kd = (k.astype(F32) * k_decay).astype(BF16)
            state = state * chunk_decay + _dot_tn(kd, v)
        gate = gate_ref[lo:hi, :].astype(F32)
        o_ref[lo:hi, :] = (_rms(y) * (gate * jax.nn.sigmoid(gate))).astype(o_ref.dtype)
        yield


N_ATTN_INS, N_RET_INS = 9, 5


def _mixers_kernel(*refs, n_side, tq, lam_init):
    ins, rest = refs[:N_ATTN_INS + N_RET_INS], refs[N_ATTN_INS + N_RET_INS:]
    side_in, rest = rest[:n_side], rest[n_side:]
    da_ref, ret_ref = rest[:2]
    side_out, scratch = rest[2:2 + n_side], rest[2 + n_side:]
    _cast_blocks(zip(side_in, side_out))
    pending = [_retention_body(ins[N_ATTN_INS:], ret_ref),
               _diff_attn_body(ins[:N_ATTN_INS], da_ref, scratch, tq=tq, lam_init=lam_init)]
    while pending:
        pending = [body for body in pending if next(body, StopIteration) is not StopIteration]


def _mixers(proj, slopes, lq1, lk1, lq2, lk2, g, log_gammas, *, batch, seq, lam_init, tq,
            side_weights=()):
    assert DA_HEADS == RET_HEADS
    hb = DA_V_DIM
    q_blk0, k_blk0, v_blk0 = 0, DA_HEADS, 2 * DA_HEADS
    da_cols = 3 * DA_HEADS * DA_V_DIM
    rq_blk0 = da_cols // RET_QK_DIM
    rk_blk0 = rq_blk0 + RET_HEADS
    rv_blk0 = (da_cols + 2 * RET_HEADS * RET_QK_DIM) // RET_V_DIM
    rg_blk0 = rv_blk0 + RET_HEADS
    grid = (batch, DA_HEADS)
    side = _SideCasts(side_weights, grid)
    scratch = [
        pltpu.VMEM((seq, DA_HEAD_DIM), BF16),
        pltpu.VMEM((2, tq, seq), F32),
        pltpu.VMEM((2, tq, seq), BF16),
    ]
    limit = _vmem_limit(
        [_nbytes((seq, hb), proj.dtype)] * 4
        + [_nbytes((seq, RET_QK_DIM), proj.dtype)] * 2 + [_nbytes((seq, RET_V_DIM), proj.dtype)] * 3
        + side.window_bytes,
        [_nbytes((seq, DA_HEAD_DIM), BF16),
         3 * _nbytes((2, tq, seq), F32), 3 * _nbytes((2, tq, seq), BF16)],
    )
    vec = pl.BlockSpec((1, DA_HEAD_DIM), lambda b, h: (0, 0))
    smem = pl.BlockSpec(memory_space=pltpu.SMEM)

    def head_cols(width, blk0):
        return pl.BlockSpec((seq, width), lambda b, h: (b, blk0 + h))

    outs = pl.pallas_call(
        functools.partial(_mixers_kernel, n_side=len(side), tq=tq, lam_init=lam_init),
        grid=grid,
        in_specs=[
            smem, vec, vec, vec, vec,
            pl.BlockSpec((1, hb), lambda b, h: (0, 0)),
            head_cols(hb, q_blk0), head_cols(hb, k_blk0), head_cols(hb, v_blk0),
            smem,
            head_cols(RET_QK_DIM, rq_blk0), head_cols(RET_QK_DIM, rk_blk0),
            head_cols(RET_V_DIM, rv_blk0), head_cols(RET_V_DIM, rg_blk0),
        ] + side.in_specs,
        out_specs=[head_cols(hb, 0), head_cols(RET_V_DIM, 0)] + side.out_specs,
        out_shape=[jax.ShapeDtypeStruct((batch * seq, DA_HEADS * hb), BF16),
                   jax.ShapeDtypeStruct((batch * seq, RET_HEADS * RET_V_DIM), BF16)] + side.out_shapes,
        scratch_shapes=scratch,
        compiler_params=pltpu.CompilerParams(
            dimension_semantics=("arbitrary", "arbitrary"), vmem_limit_bytes=limit
        ),
        name="token_mixers",
    )(slopes, lq1.reshape(1, -1), lk1.reshape(1, -1), lq2.reshape(1, -1), lk2.reshape(1, -1),
      g.reshape(1, hb), proj, proj, proj, log_gammas, proj, proj, proj, proj, *side_weights)
    return tuple(outs)


def _xattn_kernel(*refs, n_side):
    ins, o_ref, (xo_ref,), side = _split_refs(refs, 5, n_side)
    xq_ref, xk_ref, xv_ref, res_ref, wo_ref = ins
    _cast_blocks(side)
    d_model = xq_ref.shape[1]
    hd = d_model // XATTN_HEADS
    scale2 = hd ** -0.5 * LOG2_E
    for h in range(XATTN_HEADS):
        cols = slice(h * hd, (h + 1) * hd)
        s = _dot_nt(xq_ref[:, cols], xk_ref[:, cols]) * scale2
        p = jnp.exp2(s - jnp.max(s, axis=-1, keepdims=True))
        p = p * (1.0 / jnp.sum(p, axis=-1, keepdims=True))
        xo_ref[:, cols] = _dot(p.astype(BF16), xv_ref[:, cols]).astype(xo_ref.dtype)
    o_ref[...] = res_ref[...] + _dot(xo_ref[...], wo_ref[...])


def _cross_attention(xq, xk, xv, res, wo, *, batch, seq, mem_len, tq, side_weights=()):
    d = xq.shape[1]
    nq = seq // tq
    grid = (batch, nq)
    side = _SideCasts(side_weights, grid)
    limit = _vmem_limit(
        [
            _nbytes((tq, d), xq.dtype),
            _nbytes((mem_len, d), xk.dtype),
            _nbytes((mem_len, d), xv.dtype),
            _nbytes((tq, d), F32),
            _nbytes((d, d), wo.dtype),
            _nbytes((tq, d), F32),
        ] + side.window_bytes,
        [_nbytes((tq, d), BF16), _nbytes((tq, d), F32)],
    )
    outs = pl.pallas_call(
        functools.partial(_xattn_kernel, n_side=len(side)),
        grid=grid,
        in_specs=[
            pl.BlockSpec((tq, d), lambda b, i: (b * nq + i, 0)),
            pl.BlockSpec((mem_len, d), lambda b, i: (b, 0)),
            pl.BlockSpec((mem_len, d), lambda b, i: (b, 0)),
            pl.BlockSpec((tq, d), lambda b, i: (b * nq + i, 0)),
            pl.BlockSpec((d, d), lambda b, i: (0, 0)),
        ] + side.in_specs,
        out_specs=[pl.BlockSpec((tq, d), lambda b, i: (b * nq + i, 0))] + side.out_specs,
        out_shape=[jax.ShapeDtypeStruct((batch * seq, d), F32)] + side.out_shapes,
        scratch_shapes=[pltpu.VMEM((tq, d), BF16)],
        compiler_params=pltpu.CompilerParams(
            dimension_semantics=("arbitrary", "arbitrary"), vmem_limit_bytes=limit
        ),
        name="cross_attention",
    )(xq, xk, xv, res, wo, *side_weights)
    return tuple(outs)


def _ffn_kernel(x_ref, g_ref, wg_ref, wu_ref, wd_ref, gf_ref, o_ref, h_ref, *, final_norm):
    f = pl.program_id(1)
    last = pl.num_programs(1) - 1
    blocks = _row_blocks(x_ref.shape[0])

    def partial_ffn(h, chunked=False):
        hidden = wg_ref.shape[1]
        out = None
        for cols in (_col_chunks(hidden) if chunked else [slice(0, hidden)]):
            gate = _dot(h, wg_ref[:, cols])
            up = _dot(h, wu_ref[:, cols])
            act = (gate * jax.nn.sigmoid(gate)) * up
            part = _dot(act.astype(BF16), wd_ref[cols, :])
            out = part if out is None else out + part
        return out

    @pl.when(f == 0)
    def _():
        for rows in blocks:
            x = x_ref[rows, :]
            h = (_rms(x) * g_ref[...]).astype(h_ref.dtype)
            h_ref[rows, :] = h
            o_ref[rows, :] = x + partial_ffn(h)

    if final_norm:
        @pl.when(jnp.logical_and(f > 0, f < last))
        def _():
            o_ref[...] += partial_ffn(h_ref[...], chunked=True)

        @pl.when(f == last)
        def _():
            for rows in blocks:
                y = o_ref[rows, :] + partial_ffn(h_ref[rows, :])
                o_ref[rows, :] = _rms(y) * gf_ref[...]
    else:
        @pl.when(f > 0)
        def _():
            o_ref[...] += partial_ffn(h_ref[...], chunked=True)


def _ffn(x, g, wg, wu, wd, gf, *, final_norm, tm, tf):
    m, d = x.shape
    d_ff = wg.shape[1]
    limit = _vmem_limit(
        [
            _nbytes((tm, d), F32),
            _nbytes((d, tf), wg.dtype),
            _nbytes((d, tf), wu.dtype),
            _nbytes((tf, d), wd.dtype),
            _nbytes((tm, d), F32),
        ],
        [_nbytes((tm, d), BF16), _nbytes((tm, d), F32), 4 * _nbytes((tm, tf), F32)],
    )
    return pl.pallas_call(
        functools.partial(_ffn_kernel, final_norm=final_norm),
        grid=(m // tm, d_ff // tf),
        in_specs=[
            pl.BlockSpec((tm, d), lambda i, f: (i, 0)),
            pl.BlockSpec((1, d), lambda i, f: (0, 0)),
            pl.BlockSpec((d, tf), lambda i, f: (0, f)),
            pl.BlockSpec((d, tf), lambda i, f: (0, f)),
            pl.BlockSpec((tf, d), lambda i, f: (f, 0)),
            pl.BlockSpec((1, d), lambda i, f: (0, 0)),
        ],
        out_specs=pl.BlockSpec((tm, d), lambda i, f: (i, 0)),
        out_shape=jax.ShapeDtypeStruct((m, d), F32),
        scratch_shapes=[pltpu.VMEM((tm, d), BF16)],
        compiler_params=pltpu.CompilerParams(
            dimension_semantics=("parallel", "arbitrary"), vmem_limit_bytes=limit
        ),
        name="swiglu_ffn",
    )(x, g.reshape(1, d), wg, wu, wd, gf.reshape(1, d))


def kernel(x, mem, norm_mix_g, w_in, lambda_q1, lambda_k1, lambda_q2, lambda_k2, da_subln_g, w_o, norm_x_g, norm_mem_g, w_xq, w_xk, w_xv, w_xo, norm_ffn_g, w_gate, w_up, w_down, norm_f_g):
    batch, seq, d_model = x.shape
    mem_len = mem.shape[1]
    depth = w_in.shape[0]
    slopes = jnp.asarray(2.0 ** (-8.0 * np.arange(1, DA_HEADS + 1) / DA_HEADS), dtype=F32)
    log_gammas = jnp.asarray(np.log(1.0 - 2.0 ** (-5.0 - np.arange(RET_HEADS))), dtype=F32)

    xf = x.reshape(batch * seq, d_model)
    memf = mem.reshape(batch * mem_len, d_model)
    for l in range(depth):
        lam_init = 0.8 - 0.6 * math.exp(-0.3 * l)
        n_dq = DA_HEADS * 2 * DA_HEAD_DIM
        col_scale = jnp.where(jnp.arange(w_in.shape[2]) < n_dq, DA_HEAD_DIM ** -0.5 * LOG2_E, 1.0)
        proj, = _norm_matmul(
            xf, norm_mix_g[l], w_in[l], BF16, tm=ROW_TILE, tn=COL_TILE,
            group_tiles=IN_PROJ_GROUP_TILES, name="in_proj", col_scale=col_scale.astype(F32))
        da, ret, wb_up, wb_o, wb_xq, wb_xo = _mixers(
            proj, slopes, lambda_q1[l], lambda_k1[l], lambda_q2[l], lambda_k2[l], da_subln_g[l],
            log_gammas, batch=batch, seq=seq, lam_init=lam_init, tq=ATTN_Q_TILE,
            side_weights=(w_up[l], w_o[l], w_xq[l], w_xo[l]))
        xf, xq, wb_gate = _out_proj_xq(da, ret, wb_o, xf, norm_x_g[l], wb_xq, tm=OUT_PROJ_ROW_TILE,
                                       side_weights=(w_gate[l],))
        xk, xv = _norm_matmul_pair(memf, norm_mem_g[l], w_xk[l], w_xv[l], BF16, tm=ROW_TILE,
                                   tn=KV_COL_TILE, name="xattn_kv")
        xf, wb_down = _cross_attention(xq, xk, xv, xf, wb_xo, batch=batch, seq=seq, mem_len=mem_len,
                                       tq=XATTN_Q_TILE, side_weights=(w_down[l],))

        xf = _ffn(xf, norm_ffn_g[l], wb_gate, wb_up, wb_down, norm_f_g,
                  final_norm=(l == depth - 1), tm=FFN_ROW_TILE, tf=FFN_COL_TILE)
    return xf.reshape(batch, seq, d_model)
```

```python
import functools
import math

import jax
import jax.numpy as jnp
import numpy as np
from jax import lax
from jax.experimental import pallas as pl
from jax.experimental.pallas import tpu as pltpu

F32 = jnp.float32
BF16 = jnp.bfloat16

DA_HEADS = 4
DA_HEAD_DIM = 128
DA_V_DIM = 2 * DA_HEAD_DIM
RET_HEADS = 4
RET_QK_DIM = 128
RET_V_DIM = 256
XATTN_HEADS = 4
RET_CHUNK = 256
NORM_EPS = 1e-6
NEG_INF = -1e30
LOG2_E = math.log2(math.e)
NORM_BLOCK_ROWS = 256

ROW_TILE = 1024
COL_TILE = 1024
KV_COL_TILE = 512
IN_PROJ_GROUP_TILES = 2
OUT_PROJ_ROW_TILE = 512
ATTN_Q_TILE = 512
XATTN_Q_TILE = 512
FFN_ROW_TILE, FFN_COL_TILE = 1024, 512

V7X_LANES = 128
MXU_COLS = 256
BF16_TILE_ROWS = 16
V7X_VMEM_BYTES = 64 * 1024 * 1024
V7X_VMEM_USABLE_BYTES = V7X_VMEM_BYTES - 6 * 1024 * 1024
COMPILER_SCRATCH_BYTES = 4 * 1024 * 1024


def _nbytes(shape, dtype):
    return int(np.prod(shape)) * jnp.dtype(dtype).itemsize


def _vmem_limit(pipelined, resident):
    need = 2 * sum(pipelined) + sum(resident) + COMPILER_SCRATCH_BYTES
    return int(min(V7X_VMEM_USABLE_BYTES, need))


def _rms(x):
    return x * lax.rsqrt(jnp.mean(x * x, axis=-1, keepdims=True) + NORM_EPS)


def _fold_lanes(x, op):
    tiles = [x[:, i:i + V7X_LANES] for i in range(0, x.shape[1], V7X_LANES)]
    return functools.reduce(op, tiles)


def _dot(a, b):
    return jnp.dot(a, b, preferred_element_type=F32)


def _dot_nt(a, b):
    return lax.dot_general(a, b, (((1,), (1,)), ((), ())), preferred_element_type=F32)


def _dot_tn(a, b):
    return lax.dot_general(a, b, (((0,), (0,)), ((), ())), preferred_element_type=F32)


class _SideCasts:
    def __init__(self, weights, grid):
        self.weights = list(weights)
        self.grid = tuple(grid)
        n_steps = int(np.prod(self.grid))
        self.plans = []
        for w in self.weights:
            rows, n_blocks = w.shape[0], n_steps
            while rows % n_blocks or (rows // n_blocks) % BF16_TILE_ROWS:
                n_blocks -= 1
            self.plans.append((n_blocks, rows // n_blocks))

    def __len__(self):
        return len(self.weights)

    def _specs(self):
        specs = []
        for w, (n_blocks, block_rows) in zip(self.weights, self.plans):
            def index(*ids, n_blocks=n_blocks):
                step = ids[0]
                for extent, idx in zip(self.grid[1:], ids[1:]):
                    step = step * extent + idx
                return (jnp.minimum(step, n_blocks - 1), 0)
            specs.append(pl.BlockSpec((block_rows, w.shape[1]), index))
        return specs

    in_specs = property(_specs)
    out_specs = property(_specs)

    @property
    def out_shapes(self):
        return [jax.ShapeDtypeStruct(w.shape, BF16) for w in self.weights]

    @property
    def window_bytes(self):
        return [_nbytes((rows, w.shape[1]), dt)
                for w, (_, rows) in zip(self.weights, self.plans) for dt in (w.dtype, BF16)]


def _split_refs(refs, n_in, n_side):
    ins, rest = refs[:n_in], refs[n_in:]
    side_in, rest = rest[:n_side], rest[n_side:]
    out, side_out, scratch = rest[0], rest[1:1 + n_side], rest[1 + n_side:]
    return ins, out, scratch, list(zip(side_in, side_out))


def _cast_blocks(pairs):
    for src, dst in pairs:
        dst[...] = src[...].astype(dst.dtype)


def _row_blocks(n_rows):
    step = min(NORM_BLOCK_ROWS, n_rows)
    return [slice(r, r + step) for r in range(0, n_rows, step)]


def _col_chunks(n_cols):
    step = min(MXU_COLS, n_cols)
    return [slice(c, c + step) for c in range(0, n_cols, step)]


def _norm_matmul_kernel(*refs, n_side, cast_w, group_tiles):
    ins, o_ref, scratch, side = _split_refs(refs, 4 if cast_w else 3, n_side)
    x_ref, g_ref, w_ref = ins[:3]
    h_ref = scratch[0]
    wb_ref = scratch[1] if cast_w else w_ref
    step = pl.program_id(1)
    tm = x_ref.shape[0]

    def cast_weight(cols):
        wb_ref[:, cols] = (w_ref[:, cols] * ins[3][:, cols]).astype(wb_ref.dtype)

    for t in range(group_tiles):
        @pl.when(step == t)
        def _(t=t):
            _cast_blocks(side)
            if cast_w and t == 0:
                cast_weight(slice(None))
            for rows in _row_blocks(tm):
                group_rows = slice(t * tm + rows.start, t * tm + rows.stop)
                h = (_rms(x_ref[rows, :]) * g_ref[...]).astype(h_ref.dtype)
                h_ref[group_rows, :] = h
                o_ref[group_rows, :] = _dot(h, wb_ref[...]).astype(o_ref.dtype)

    @pl.when(step >= group_tiles)
    def _():
        _cast_blocks(side)
        if cast_w:
            for cols in _col_chunks(w_ref.shape[1]):
                cast_weight(cols)
                o_ref[:, cols] = _dot(h_ref[...], wb_ref[:, cols]).astype(o_ref.dtype)
        else:
            o_ref[...] = _dot(h_ref[...], w_ref[...]).astype(o_ref.dtype)


def _norm_matmul(x, g, w, out_dtype, *, tm, tn, name, group_tiles=1, col_scale=None,
                 side_weights=()):
    m, d = x.shape
    n = w.shape[1]
    tm, tn = min(tm, m), min(tn, n)
    gt = group_tiles
    grid = (m // (tm * gt), gt + n // tn - 1)
    side = _SideCasts(side_weights, grid)
    cast_w = w.dtype != BF16
    assert cast_w or col_scale is None

    def col_tile(s):
        return jnp.maximum(s - (gt - 1), 0)

    weight_ins, weight_specs = [w], [pl.BlockSpec((d, tn), lambda gi, s: (0, col_tile(s)))]
    if cast_w:
        cs = jnp.ones((n,), F32) if col_scale is None else col_scale
        weight_ins.append(cs.reshape(1, n))
        weight_specs.append(pl.BlockSpec((1, tn), lambda gi, s: (0, col_tile(s))))
    scratch = [pltpu.VMEM((gt * tm, d), BF16)] + ([pltpu.VMEM((d, tn), BF16)] if cast_w else [])
    limit = _vmem_limit(
        [_nbytes((tm, d), x.dtype), _nbytes((d, tn), w.dtype), _nbytes((gt * tm, tn), out_dtype),
         _nbytes((8, d), F32), _nbytes((8, tn), F32)]
        + side.window_bytes,
        [_nbytes((gt * tm, d), BF16), _nbytes((d, tn), BF16) * cast_w],
    )
    outs = pl.pallas_call(
        functools.partial(_norm_matmul_kernel, n_side=len(side), cast_w=cast_w, group_tiles=gt),
        grid=grid,
        in_specs=[
            pl.BlockSpec((tm, d), lambda gi, s: (gi * gt + jnp.minimum(s, gt - 1), 0)),
            pl.BlockSpec((1, d), lambda gi, s: (0, 0)),
        ] + weight_specs + side.in_specs,
        out_specs=[pl.BlockSpec((gt * tm, tn), lambda gi, s: (gi, col_tile(s)))] + side.out_specs,
        out_shape=[jax.ShapeDtypeStruct((m, n), out_dtype)] + side.out_shapes,
        scratch_shapes=scratch,
        compiler_params=pltpu.CompilerParams(
            dimension_semantics=("arbitrary", "arbitrary"), vmem_limit_bytes=limit
        ),
        name=name,
    )(x, g.reshape(1, d), *weight_ins, *side_weights)
    return tuple(outs)


def _norm_matmul_pair_kernel(x_ref, g_ref, w1_ref, w2_ref, o1_ref, o2_ref, h_ref):
    @pl.when(pl.program_id(1) == 0)
    def _():
        h_ref[...] = (_rms(x_ref[...]) * g_ref[...]).astype(h_ref.dtype)

    for w_ref, o_ref in ((w1_ref, o1_ref), (w2_ref, o2_ref)):
        for cols in _col_chunks(w_ref.shape[1]):
            o_ref[:, cols] = _dot(h_ref[...], w_ref[:, cols].astype(BF16)).astype(o_ref.dtype)


def _norm_matmul_pair(x, g, w1, w2, out_dtype, *, tm, tn, name):
    m, d = x.shape
    n = w1.shape[1]
    assert w1.shape == w2.shape
    tm, tn = min(tm, m), min(tn, n)
    w_spec = pl.BlockSpec((d, tn), lambda i, j: (0, j))
    o_spec = pl.BlockSpec((tm, tn), lambda i, j: (i, j))
    limit = _vmem_limit(
        [_nbytes((tm, d), x.dtype)] + [_nbytes((d, tn), w1.dtype), _nbytes((tm, tn), out_dtype)] * 2,
        [_nbytes((tm, d), BF16), _nbytes((tm, d), F32)],
    )
    return pl.pallas_call(
        _norm_matmul_pair_kernel,
        grid=(m // tm, n // tn),
        in_specs=[pl.BlockSpec((tm, d), lambda i, j: (i, 0)), pl.BlockSpec((1, d), lambda i, j: (0, 0)),
                  w_spec, w_spec],
        out_specs=[o_spec, o_spec],
        out_shape=[jax.ShapeDtypeStruct((m, n), out_dtype)] * 2,
        scratch_shapes=[pltpu.VMEM((tm, d), BF16)],
        compiler_params=pltpu.CompilerParams(
            dimension_semantics=("arbitrary", "arbitrary"), vmem_limit_bytes=limit
        ),
        name=name,
    )(x, g.reshape(1, d), w1, w2)


def _matmul_res_kernel(*refs, n_side):
    (a_ref, w_ref, res_ref), o_ref, _, side = _split_refs(refs, 3, n_side)
    _cast_blocks(side)
    o_ref[...] = res_ref[...] + _dot(a_ref[...], w_ref[...])


def _matmul_res(a, w, res, *, tm, name, side_weights=()):
    m, k = a.shape
    n = w.shape[1]
    grid = (m // tm,)
    side = _SideCasts(side_weights, grid)
    limit = _vmem_limit(
        [_nbytes((tm, k), a.dtype), _nbytes((tm, n), F32), _nbytes((tm, n), F32)]
        + side.window_bytes,
        [_nbytes(w.shape, w.dtype), _nbytes((tm, n), F32)],
    )
    outs = pl.pallas_call(
        functools.partial(_matmul_res_kernel, n_side=len(side)),
        grid=grid,
        in_specs=[
            pl.BlockSpec((tm, k), lambda i: (i, 0)),
            pl.BlockSpec((k, n), lambda i: (0, 0), pipeline_mode=pl.Buffered(1)),
            pl.BlockSpec((tm, n), lambda i: (i, 0)),
        ] + side.in_specs,
        out_specs=[pl.BlockSpec((tm, n), lambda i: (i, 0))] + side.out_specs,
        out_shape=[jax.ShapeDtypeStruct((m, n), F32)] + side.out_shapes,
        compiler_params=pltpu.CompilerParams(
            dimension_semantics=("arbitrary",), vmem_limit_bytes=limit
        ),
        name=name,
    )(a, w, res, *side_weights)
    return tuple(outs)


def _out_proj_xattn_kernel(*refs, n_side):
    ins, rest = refs[:9], refs[9:]
    a1_ref, a2_ref, w1_ref, w2_ref, res_ref, g_ref, wq_ref, xk_ref, xv_ref = ins
    side_in, rest = rest[:n_side], rest[n_side:]
    x1_ref, xo_ref = rest[:2]
    side_out = rest[2:2 + n_side]
    _cast_blocks(zip(side_in, side_out))
    d_model = x1_ref.shape[1]
    hd = d_model // XATTN_HEADS
    scale2 = hd ** -0.5 * LOG2_E
    for rows in _row_blocks(x1_ref.shape[0]):
        x1 = res_ref[rows, :] + _dot(a1_ref[rows, :], w1_ref[...]) + _dot(a2_ref[rows, :], w2_ref[...])
        x1_ref[rows, :] = x1
        h = (_rms(x1) * g_ref[...]).astype(BF16)
        xq = _dot(h, wq_ref[...]).astype(BF16)
        for head in range(XATTN_HEADS):
            cols = slice(head * hd, (head + 1) * hd)
            s = _dot_nt(xq[:, cols], xk_ref[:, cols]) * scale2
            p = jnp.exp2(s - jnp.max(s, axis=-1, keepdims=True))
            p = p * (1.0 / jnp.sum(p, axis=-1, keepdims=True))
            xo_ref[rows, cols] = _dot(p.astype(BF16), xv_ref[:, cols]).astype(xo_ref.dtype)


def _out_proj_xattn(a1, a2, wo, res, g, wq, xk, xv, *, tm, seq, mem_len, side_weights=()):
    m, k1 = a1.shape
    k2 = a2.shape[1]
    n = wo.shape[1]
    assert k1 == k2 and wo.shape[0] == k1 + k2 and wq.shape == (n, n) and seq % tm == 0
    tiles_per_batch = seq // tm
    grid = (m // tm,)
    side = _SideCasts(side_weights, grid)
    limit = _vmem_limit(
        [_nbytes((tm, k1), a1.dtype), _nbytes((tm, k2), a2.dtype), _nbytes((tm, n), F32),
         _nbytes((tm, n), F32), _nbytes((tm, n), BF16),
         _nbytes((mem_len, n), xk.dtype), _nbytes((mem_len, n), xv.dtype)] + side.window_bytes,
        [_nbytes(wo.shape, wo.dtype), _nbytes(wq.shape, wq.dtype), 2 * _nbytes((tm, n), F32)],
    )
    resident = pl.Buffered(1)
    row_tile = lambda width: pl.BlockSpec((tm, width), lambda i: (i, 0))
    memory = pl.BlockSpec((mem_len, n), lambda i: (i // tiles_per_batch, 0))
    outs = pl.pallas_call(
        functools.partial(_out_proj_xattn_kernel, n_side=len(side)),
        grid=grid,
        in_specs=[
            row_tile(k1), row_tile(k2),
            pl.BlockSpec((k1, n), lambda i: (0, 0), pipeline_mode=resident),
            pl.BlockSpec((k2, n), lambda i: (1, 0), pipeline_mode=resident),
            row_tile(n),
            pl.BlockSpec((1, n), lambda i: (0, 0)),
            pl.BlockSpec((n, n), lambda i: (0, 0), pipeline_mode=resident),
            memory, memory,
        ] + side.in_specs,
        out_specs=[row_tile(n), row_tile(n)] + side.out_specs,
        out_shape=[jax.ShapeDtypeStruct((m, n), F32), jax.ShapeDtypeStruct((m, n), BF16)]
        + side.out_shapes,
        compiler_params=pltpu.CompilerParams(
            dimension_semantics=("arbitrary",), vmem_limit_bytes=limit
        ),
        name="out_proj_xattn",
    )(a1, a2, wo, wo, res, g.reshape(1, n), wq, xk, xv, *side_weights)
    return tuple(outs)


def _bf16_part(x):
    bits = lax.bitcast_convert_type(x, jnp.int32) & jnp.int32(-65536)
    return lax.bitcast_convert_type(bits, F32)


def _diff_attn_body(ins, o_ref, scratch, *, tq, lam_init):
    slope_ref, lq1_ref, lk1_ref, lq2_ref, lk2_ref, g_ref, q_ref, k_ref, v_ref = ins
    kx_ref, s_ref, p_ref = scratch
    s_len = q_ref.shape[0]
    d = DA_HEAD_DIM
    lam = (jnp.exp(jnp.sum(lq1_ref[...] * lk1_ref[...], axis=-1, keepdims=True))
           - jnp.exp(jnp.sum(lq2_ref[...] * lk2_ref[...], axis=-1, keepdims=True))
           + lam_init)

    lane = lax.broadcasted_iota(jnp.int32, (1, d), 1)
    slope2 = jnp.full((1, d), slope_ref[pl.program_id(1)] * LOG2_E, F32)
    piece_hi = _bf16_part(slope2)
    rest = slope2 - piece_hi
    piece_mid = _bf16_part(rest)
    piece_lo = _bf16_part(rest - piece_mid)
    piece = jnp.where((lane == 0) | (lane == 3), piece_hi,
                      jnp.where((lane == 1) | (lane == 4), piece_mid, piece_lo))
    q_extra = jnp.where(lane < 3, piece * 256.0, jnp.where(lane < 6, piece, 0.0))
    q_extra = jnp.broadcast_to(q_extra, (tq, d)).astype(BF16)
    kpos = lax.broadcasted_iota(jnp.int32, (s_len, d), 0)
    klane = lax.broadcasted_iota(jnp.int32, (s_len, d), 1)
    k_extra = jnp.where(klane < 3, kpos >> 8, jnp.where(klane < 6, kpos & 255, 0))
    kx_ref[...] = k_extra.astype(F32).astype(kx_ref.dtype)

    row = lax.broadcasted_iota(jnp.int32, (tq, tq), 0)
    col = lax.broadcasted_iota(jnp.int32, (tq, tq), 1)
    causal = col <= row

    for qi in reversed(range(s_len // tq)):
        lo, hi = qi * tq, (qi + 1) * tq
        key_blocks = [slice(j * tq, (j + 1) * tq) for j in range(qi + 1)]
        heads = []
        for c in range(2):
            dcols = slice(c * d, (c + 1) * d)
            q_aug = jnp.concatenate([q_ref[lo:hi, dcols], q_extra], axis=1)
            m = None
            for j, cols in enumerate(key_blocks):
                k_aug = jnp.concatenate([k_ref[cols, dcols], kx_ref[cols, :]], axis=1)
                s = _dot_nt(q_aug, k_aug)
                if j == qi:
                    s = jnp.where(causal, s, NEG_INF)
                s_ref[c, :, cols] = s
                bm = _fold_lanes(s, jnp.maximum)
                m = bm if m is None else jnp.maximum(m, bm)
            m = jnp.max(m, axis=-1, keepdims=True)
            l = None
            for cols in key_blocks:
                p = jnp.exp2(s_ref[c, :, cols] - m)
                bl = _fold_lanes(p, jnp.add)
                l = bl if l is None else l + bl
                p_ref[c, :, cols] = p.astype(p_ref.dtype)
            l = jnp.sum(l, axis=-1, keepdims=True)
            acc = _dot(p_ref[c, :, 0:hi], v_ref[0:hi, :])
            heads.append((acc, l))
        (acc1, l1), (acc2, l2) = heads
        out = acc1 * (1.0 / l1) - acc2 * (lam / l2)
        y = _rms(out) * g_ref[...] * (1.0 - lam_init)
        o_ref[lo:hi, :] = y.astype(o_ref.dtype)
        yield


def _retention_body(ins, o_ref):
    lg_ref, q_ref, k_ref, v_ref, gate_ref = ins
    s_len = q_ref.shape[0]
    c = RET_CHUNK
    dk, dv = RET_QK_DIM, RET_V_DIM
    scale = dk ** -0.5
    lg = lg_ref[pl.program_id(1)]

    row = lax.broadcasted_iota(jnp.int32, (c, c), 0)
    col = lax.broadcasted_iota(jnp.int32, (c, c), 1)
    diff = (row - col).astype(F32)
    intra = jnp.where(diff >= 0, jnp.exp(lg * jnp.maximum(diff, 0.0)), 0.0) * scale
    row_k = lax.broadcasted_iota(jnp.int32, (c, dk), 0).astype(F32)
    k_decay = jnp.exp(lg * (float(c - 1) - row_k)) * scale
    row_v = lax.broadcasted_iota(jnp.int32, (c, dv), 0).astype(F32)
    q_decay = jnp.exp(lg * (row_v + 1.0))
    chunk_decay = jnp.exp(jnp.full((1, dv), lg * float(c), F32))

    state = jnp.zeros((dk, dv), F32)
    for i in range(s_len // c):
        lo, hi = i * c, (i + 1) * c
        q = q_ref[lo:hi, :]
        k = k_ref[lo:hi, :]
        v = v_ref[lo:hi, :]
        scores = _dot_nt(q, k) * intra
        y = _dot(scores.astype(BF16), v)
        if i > 0:
            y = y + _dot(q, state.astype(BF16)) * q_decay
        if i + 1 < s_len // c:
            kd = (k.astype(F32) * k_decay).astype(BF16)
            state = state * chunk_decay + _dot_tn(kd, v)
        gate = gate_ref[lo:hi, :].astype(F32)
        o_ref[lo:hi, :] = (_rms(y) * (gate * jax.nn.sigmoid(gate))).astype(o_ref.dtype)
        yield


N_ATTN_INS, N_RET_INS = 9, 5


def _mixers_kernel(*refs, n_side, tq, lam_init):
    ins, rest = refs[:N_ATTN_INS + N_RET_INS], refs[N_ATTN_INS + N_RET_INS:]
    side_in, rest = rest[:n_side], rest[n_side:]
    da_ref, ret_ref = rest[:2]
    side_out, scratch = rest[2:2 + n_side], rest[2 + n_side:]
    _cast_blocks(zip(side_in, side_out))
    pending = [_retention_body(ins[N_ATTN_INS:], ret_ref),
               _diff_attn_body(ins[:N_ATTN_INS], da_ref, scratch, tq=tq, lam_init=lam_init)]
    while pending:
        pending = [body for body in pending if next(body, StopIteration) is not StopIteration]


def _mixers(proj, slopes, lq1, lk1, lq2, lk2, g, log_gammas, *, batch, seq, lam_init, tq,
            side_weights=()):
    assert DA_HEADS == RET_HEADS
    hb = DA_V_DIM
    q_blk0, k_blk0, v_blk0 = 0, DA_HEADS, 2 * DA_HEADS
    da_cols = 3 * DA_HEADS * DA_V_DIM
    rq_blk0 = da_cols // RET_QK_DIM
    rk_blk0 = rq_blk0 + RET_HEADS
    rv_blk0 = (da_cols + 2 * RET_HEADS * RET_QK_DIM) // RET_V_DIM
    rg_blk0 = rv_blk0 + RET_HEADS
    grid = (batch, DA_HEADS)
    side = _SideCasts(side_weights, grid)
    scratch = [
        pltpu.VMEM((seq, DA_HEAD_DIM), BF16),
        pltpu.VMEM((2, tq, seq), F32),
        pltpu.VMEM((2, tq, seq), BF16),
    ]
    limit = _vmem_limit(
        [_nbytes((seq, hb), proj.dtype)] * 4
        + [_nbytes((seq, RET_QK_DIM), proj.dtype)] * 2 + [_nbytes((seq, RET_V_DIM), proj.dtype)] * 3
        + side.window_bytes,
        [_nbytes((seq, DA_HEAD_DIM), BF16),
         3 * _nbytes((2, tq, seq), F32), 3 * _nbytes((2, tq, seq), BF16)],
    )
    vec = pl.BlockSpec((1, DA_HEAD_DIM), lambda b, h: (0, 0))
    smem = pl.BlockSpec(memory_space=pltpu.SMEM)

    def head_cols(width, blk0):
        return pl.BlockSpec((seq, width), lambda b, h: (b, blk0 + h))

    outs = pl.pallas_call(
        functools.partial(_mixers_kernel, n_side=len(side), tq=tq, lam_init=lam_init),
        grid=grid,
        in_specs=[
            smem, vec, vec, vec, vec,
            pl.BlockSpec((1, hb), lambda b, h: (0, 0)),
            head_cols(hb, q_blk0), head_cols(hb, k_blk0), head_cols(hb, v_blk0),
            smem,
            head_cols(RET_QK_DIM, rq_blk0), head_cols(RET_QK_DIM, rk_blk0),
            head_cols(RET_V_DIM, rv_blk0), head_cols(RET_V_DIM, rg_blk0),
        ] + side.in_specs,
        out_specs=[head_cols(hb, 0), head_cols(RET_V_DIM, 0)] + side.out_specs,
        out_shape=[jax.ShapeDtypeStruct((batch * seq, DA_HEADS * hb), BF16),
                   jax.ShapeDtypeStruct((batch * seq, RET_HEADS * RET_V_DIM), BF16)] + side.out_shapes,
        scratch_shapes=scratch,
        compiler_params=pltpu.CompilerParams(
            dimension_semantics=("arbitrary", "arbitrary"), vmem_limit_bytes=limit
        ),
        name="token_mixers",
    )(slopes, lq1.reshape(1, -1), lk1.reshape(1, -1), lq2.reshape(1, -1), lk2.reshape(1, -1),
      g.reshape(1, hb), proj, proj, proj, log_gammas, proj, proj, proj, proj, *side_weights)
    return tuple(outs)


def _xattn_kernel(*refs, n_side):
    ins, o_ref, (xo_ref,), side = _split_refs(refs, 5, n_side)
    xq_ref, xk_ref, xv_ref, res_ref, wo_ref = ins
    _cast_blocks(side)
    d_model = xq_ref.shape[1]
    hd = d_model // XATTN_HEADS
    scale2 = hd ** -0.5 * LOG2_E
    for h in range(XATTN_HEADS):
        cols = slice(h * hd, (h + 1) * hd)
        s = _dot_nt(xq_ref[:, cols], xk_ref[:, cols]) * scale2
        p = jnp.exp2(s - jnp.max(s, axis=-1, keepdims=True))
        p = p * (1.0 / jnp.sum(p, axis=-1, keepdims=True))
        xo_ref[:, cols] = _dot(p.astype(BF16), xv_ref[:, cols]).astype(xo_ref.dtype)
    o_ref[...] = res_ref[...] + _dot(xo_ref[...], wo_ref[...])


def _cross_attention(xq, xk, xv, res, wo, *, batch, seq, mem_len, tq, side_weights=()):
    d = xq.shape[1]
    nq = seq // tq
    grid = (batch, nq)
    side = _SideCasts(side_weights, grid)
    limit = _vmem_limit(
        [
            _nbytes((tq, d), xq.dtype),
            _nbytes((mem_len, d), xk.dtype),
            _nbytes((mem_len, d), xv.dtype),
            _nbytes((tq, d), F32),
            _nbytes((d, d), wo.dtype),
            _nbytes((tq, d), F32),
        ] + side.window_bytes,
        [_nbytes((tq, d), BF16), _nbytes((tq, d), F32)],
    )
    outs = pl.pallas_call(
        functools.partial(_xattn_kernel, n_side=len(side)),
        grid=grid,
        in_specs=[
            pl.BlockSpec((tq, d), lambda b, i: (b * nq + i, 0)),
            pl.BlockSpec((mem_len, d), lambda b, i: (b, 0)),
            pl.BlockSpec((mem_len, d), lambda b, i: (b, 0)),
            pl.BlockSpec((tq, d), lambda b, i: (b * nq + i, 0)),
            pl.BlockSpec((d, d), lambda b, i: (0, 0)),
        ] + side.in_specs,
        out_specs=[pl.BlockSpec((tq, d), lambda b, i: (b * nq + i, 0))] + side.out_specs,
        out_shape=[jax.ShapeDtypeStruct((batch * seq, d), F32)] + side.out_shapes,
        scratch_shapes=[pltpu.VMEM((tq, d), BF16)],
        compiler_params=pltpu.CompilerParams(
            dimension_semantics=("arbitrary", "arbitrary"), vmem_limit_bytes=limit
        ),
        name="cross_attention",
    )(xq, xk, xv, res, wo, *side_weights)
    return tuple(outs)


def _ffn_kernel(x_ref, g_ref, wg_ref, wu_ref, wd_ref, gf_ref, o_ref, h_ref, *, final_norm):
    f = pl.program_id(1)
    last = pl.num_programs(1) - 1
    blocks = _row_blocks(x_ref.shape[0])

    def partial_ffn(h, chunked=False):
        hidden = wg_ref.shape[1]
        out = None
        for cols in (_col_chunks(hidden) if chunked else [slice(0, hidden)]):
            gate = _dot(h, wg_ref[:, cols])
            up = _dot(h, wu_ref[:, cols])
            act = (gate * jax.nn.sigmoid(gate)) * up
            part = _dot(act.astype(BF16), wd_ref[cols, :])
            out = part if out is None else out + part
        return out

    @pl.when(f == 0)
    def _():
        for rows in blocks:
            x = x_ref[rows, :]
            h = (_rms(x) * g_ref[...]).astype(h_ref.dtype)
            h_ref[rows, :] = h
            o_ref[rows, :] = x + partial_ffn(h)

    if final_norm:
        @pl.when(jnp.logical_and(f > 0, f < last))
        def _():
            o_ref[...] += partial_ffn(h_ref[...], chunked=True)

        @pl.when(f == last)
        def _():
            for rows in blocks:
                y = o_ref[rows, :] + partial_ffn(h_ref[rows, :])
                o_ref[rows, :] = _rms(y) * gf_ref[...]
    else:
        @pl.when(f > 0)
        def _():
            o_ref[...] += partial_ffn(h_ref[...], chunked=True)


def _ffn(x, g, wg, wu, wd, gf, *, final_norm, tm, tf):
    m, d = x.shape
    d_ff = wg.shape[1]
    limit = _vmem_limit(
        [
            _nbytes((tm, d), F32),
            _nbytes((d, tf), wg.dtype),
            _nbytes((d, tf), wu.dtype),
            _nbytes((tf, d), wd.dtype),
            _nbytes((tm, d), F32),
        ],
        [_nbytes((tm, d), BF16), _nbytes((tm, d), F32), 4 * _nbytes((tm, tf), F32)],
    )
    return pl.pallas_call(
        functools.partial(_ffn_kernel, final_norm=final_norm),
        grid=(m // tm, d_ff // tf),
        in_specs=[
            pl.BlockSpec((tm, d), lambda i, f: (i, 0)),
            pl.BlockSpec((1, d), lambda i, f: (0, 0)),
            pl.BlockSpec((d, tf), lambda i, f: (0, f)),
            pl.BlockSpec((d, tf), lambda i, f: (0, f)),
            pl.BlockSpec((tf, d), lambda i, f: (f, 0)),
            pl.BlockSpec((1, d), lambda i, f: (0, 0)),
        ],
        out_specs=pl.BlockSpec((tm, d), lambda i, f: (i, 0)),
        out_shape=jax.ShapeDtypeStruct((m, d), F32),
        scratch_shapes=[pltpu.VMEM((tm, d), BF16)],
        compiler_params=pltpu.CompilerParams(
            dimension_semantics=("parallel", "arbitrary"), vmem_limit_bytes=limit
        ),
        name="swiglu_ffn",
    )(x, g.reshape(1, d), wg, wu, wd, gf.reshape(1, d))


def kernel(x, mem, norm_mix_g, w_in, lambda_q1, lambda_k1, lambda_q2, lambda_k2, da_subln_g, w_o, norm_x_g, norm_mem_g, w_xq, w_xk, w_xv, w_xo, norm_ffn_g, w_gate, w_up, w_down, norm_f_g):
    batch, seq, d_model = x.shape
    mem_len = mem.shape[1]
    depth = w_in.shape[0]
    slopes = jnp.asarray(2.0 ** (-8.0 * np.arange(1, DA_HEADS + 1) / DA_HEADS), dtype=F32)
    log_gammas = jnp.asarray(np.log(1.0 - 2.0 ** (-5.0 - np.arange(RET_HEADS))), dtype=F32)

    xf = x.reshape(batch * seq, d_model)
    memf = mem.reshape(batch * mem_len, d_model)
    for l in range(depth):
        lam_init = 0.8 - 0.6 * math.exp(-0.3 * l)
        n_dq = DA_HEADS * 2 * DA_HEAD_DIM
        col_scale = jnp.where(jnp.arange(w_in.shape[2]) < n_dq, DA_HEAD_DIM ** -0.5 * LOG2_E, 1.0)
        proj, = _norm_matmul(
            xf, norm_mix_g[l], w_in[l], BF16, tm=ROW_TILE, tn=COL_TILE,
            group_tiles=IN_PROJ_GROUP_TILES, name="in_proj", col_scale=col_scale.astype(F32))
        da, ret, wb_up, wb_o, wb_xq, wb_xo = _mixers(
            proj, slopes, lambda_q1[l], lambda_k1[l], lambda_q2[l], lambda_k2[l], da_subln_g[l],
            log_gammas, batch=batch, seq=seq, lam_init=lam_init, tq=ATTN_Q_TILE,
            side_weights=(w_up[l], w_o[l], w_xq[l], w_xo[l]))
        xk, xv = _norm_matmul_pair(memf, norm_mem_g[l], w_xk[l], w_xv[l], BF16, tm=ROW_TILE,
                                   tn=KV_COL_TILE, name="xattn_kv")
        xf, xo, wb_gate = _out_proj_xattn(
            da, ret, wb_o, xf, norm_x_g[l], wb_xq, xk, xv, tm=OUT_PROJ_ROW_TILE, seq=seq,
            mem_len=mem_len, side_weights=(w_gate[l],))
        xf, wb_down = _matmul_res(xo, wb_xo, xf, tm=OUT_PROJ_ROW_TILE, name="xattn_out_proj",
                                  side_weights=(w_down[l],))

        xf = _ffn(xf, norm_ffn_g[l], wb_gate, wb_up, wb_down, norm_f_g,
                  final_norm=(l == depth - 1), tm=FFN_ROW_TILE, tf=FFN_COL_TILE)
    return xf.reshape(batch, seq, d_model)
```

```python
import functools
import math

import jax
import jax.numpy as jnp
import numpy as np
from jax import lax
from jax.experimental import pallas as pl
from jax.experimental.pallas import tpu as pltpu

F32 = jnp.float32
BF16 = jnp.bfloat16

DA_HEADS = 4
DA_HEAD_DIM = 128
DA_V_DIM = 2 * DA_HEAD_DIM
RET_HEADS = 4
RET_QK_DIM = 128
RET_V_DIM = 256
XATTN_HEADS = 4
RET_CHUNK = 256
NORM_EPS = 1e-6
NEG_INF = -1e30
LOG2_E = math.log2(math.e)
NORM_BLOCK_ROWS = 256

ROW_TILE = 1024
COL_TILE = 1024
KV_COL_TILE = 512
IN_PROJ_GROUP_TILES = 2
OUT_PROJ_ROW_TILE = 512
ATTN_Q_TILE = 512
XATTN_Q_TILE = 512
FFN_ROW_TILE, FFN_COL_TILE = 1024, 512

V7X_LANES = 128
MXU_COLS = 256
BF16_TILE_ROWS = 16
V7X_VMEM_BYTES = 64 * 1024 * 1024
V7X_VMEM_USABLE_BYTES = V7X_VMEM_BYTES - 4 * 1024 * 1024
COMPILER_SCRATCH_BYTES = 4 * 1024 * 1024


def _nbytes(shape, dtype):
    return int(np.prod(shape)) * jnp.dtype(dtype).itemsize


def _vmem_limit(pipelined, resident):
    need = 2 * sum(pipelined) + sum(resident) + COMPILER_SCRATCH_BYTES
    return int(min(V7X_VMEM_USABLE_BYTES, need))


def _rms(x):
    return x * lax.rsqrt(jnp.mean(x * x, axis=-1, keepdims=True) + NORM_EPS)


def _fold_lanes(x, op):
    tiles = [x[:, i:i + V7X_LANES] for i in range(0, x.shape[1], V7X_LANES)]
    return functools.reduce(op, tiles)


def _dot(a, b):
    return jnp.dot(a, b, preferred_element_type=F32)


def _dot_nt(a, b):
    return lax.dot_general(a, b, (((1,), (1,)), ((), ())), preferred_element_type=F32)


def _dot_tn(a, b):
    return lax.dot_general(a, b, (((0,), (0,)), ((), ())), preferred_element_type=F32)


class _SideCasts:
    def __init__(self, weights, grid):
        self.weights = list(weights)
        self.grid = tuple(grid)
        n_steps = int(np.prod(self.grid))
        self.plans = []
        for w in self.weights:
            rows, n_blocks = w.shape[0], n_steps
            while rows % n_blocks or (rows // n_blocks) % BF16_TILE_ROWS:
                n_blocks -= 1
            self.plans.append((n_blocks, rows // n_blocks))

    def __len__(self):
        return len(self.weights)

    def _specs(self):
        specs = []
        for w, (n_blocks, block_rows) in zip(self.weights, self.plans):
            def index(*ids, n_blocks=n_blocks):
                step = ids[0]
                for extent, idx in zip(self.grid[1:], ids[1:]):
                    step = step * extent + idx
                return (jnp.minimum(step, n_blocks - 1), 0)
            specs.append(pl.BlockSpec((block_rows, w.shape[1]), index))
        return specs

    in_specs = property(_specs)
    out_specs = property(_specs)

    @property
    def out_shapes(self):
        return [jax.ShapeDtypeStruct(w.shape, BF16) for w in self.weights]

    @property
    def window_bytes(self):
        return [_nbytes((rows, w.shape[1]), dt)
                for w, (_, rows) in zip(self.weights, self.plans) for dt in (w.dtype, BF16)]


def _split_refs(refs, n_in, n_side):
    ins, rest = refs[:n_in], refs[n_in:]
    side_in, rest = rest[:n_side], rest[n_side:]
    out, side_out, scratch = rest[0], rest[1:1 + n_side], rest[1 + n_side:]
    return ins, out, scratch, list(zip(side_in, side_out))


def _cast_blocks(pairs):
    for src, dst in pairs:
        dst[...] = src[...].astype(dst.dtype)


def _row_blocks(n_rows):
    step = min(NORM_BLOCK_ROWS, n_rows)
    return [slice(r, r + step) for r in range(0, n_rows, step)]


def _col_chunks(n_cols):
    step = min(MXU_COLS, n_cols)
    return [slice(c, c + step) for c in range(0, n_cols, step)]


def _norm_matmul_kernel(*refs, n_side, cast_w, group_tiles):
    ins, o_ref, scratch, side = _split_refs(refs, 4 if cast_w else 3, n_side)
    x_ref, g_ref, w_ref = ins[:3]
    h_ref = scratch[0]
    wb_ref = scratch[1] if cast_w else w_ref
    step = pl.program_id(1)
    tm = x_ref.shape[0]

    def cast_weight(cols):
        wb_ref[:, cols] = (w_ref[:, cols] * ins[3][:, cols]).astype(wb_ref.dtype)

    for t in range(group_tiles):
        @pl.when(step == t)
        def _(t=t):
            _cast_blocks(side)
            if cast_w and t == 0:
                cast_weight(slice(None))
            for rows in _row_blocks(tm):
                group_rows = slice(t * tm + rows.start, t * tm + rows.stop)
                h = (_rms(x_ref[rows, :]) * g_ref[...]).astype(h_ref.dtype)
                h_ref[group_rows, :] = h
                o_ref[group_rows, :] = _dot(h, wb_ref[...]).astype(o_ref.dtype)

    @pl.when(step >= group_tiles)
    def _():
        _cast_blocks(side)
        if cast_w:
            for cols in _col_chunks(w_ref.shape[1]):
                cast_weight(cols)
                o_ref[:, cols] = _dot(h_ref[...], wb_ref[:, cols]).astype(o_ref.dtype)
        else:
            o_ref[...] = _dot(h_ref[...], w_ref[...]).astype(o_ref.dtype)


def _norm_matmul(x, g, w, out_dtype, *, tm, tn, name, group_tiles=1, col_scale=None,
                 side_weights=()):
    m, d = x.shape
    n = w.shape[1]
    tm, tn = min(tm, m), min(tn, n)
    gt = group_tiles
    grid = (m // (tm * gt), gt + n // tn - 1)
    side = _SideCasts(side_weights, grid)
    cast_w = w.dtype != BF16
    assert cast_w or col_scale is None

    def col_tile(s):
        return jnp.maximum(s - (gt - 1), 0)

    weight_ins, weight_specs = [w], [pl.BlockSpec((d, tn), lambda gi, s: (0, col_tile(s)))]
    if cast_w:
        cs = jnp.ones((n,), F32) if col_scale is None else col_scale
        weight_ins.append(cs.reshape(1, n))
        weight_specs.append(pl.BlockSpec((1, tn), lambda gi, s: (0, col_tile(s))))
    scratch = [pltpu.VMEM((gt * tm, d), BF16)] + ([pltpu.VMEM((d, tn), BF16)] if cast_w else [])
    limit = _vmem_limit(
        [_nbytes((tm, d), x.dtype), _nbytes((d, tn), w.dtype), _nbytes((gt * tm, tn), out_dtype),
         _nbytes((8, d), F32), _nbytes((8, tn), F32)]
        + side.window_bytes,
        [_nbytes((gt * tm, d), BF16), _nbytes((d, tn), BF16) * cast_w],
    )
    outs = pl.pallas_call(
        functools.partial(_norm_matmul_kernel, n_side=len(side), cast_w=cast_w, group_tiles=gt),
        grid=grid,
        in_specs=[
            pl.BlockSpec((tm, d), lambda gi, s: (gi * gt + jnp.minimum(s, gt - 1), 0)),
            pl.BlockSpec((1, d), lambda gi, s: (0, 0)),
        ] + weight_specs + side.in_specs,
        out_specs=[pl.BlockSpec((gt * tm, tn), lambda gi, s: (gi, col_tile(s)))] + side.out_specs,
        out_shape=[jax.ShapeDtypeStruct((m, n), out_dtype)] + side.out_shapes,
        scratch_shapes=scratch,
        compiler_params=pltpu.CompilerParams(
            dimension_semantics=("arbitrary", "arbitrary"), vmem_limit_bytes=limit
        ),
        name=name,
    )(x, g.reshape(1, d), *weight_ins, *side_weights)
    return tuple(outs)


def _norm_matmul_pair_kernel(x_ref, g_ref, w1_ref, w2_ref, o1_ref, o2_ref, h_ref):
    @pl.when(pl.program_id(1) == 0)
    def _():
        h_ref[...] = (_rms(x_ref[...]) * g_ref[...]).astype(h_ref.dtype)

    for w_ref, o_ref in ((w1_ref, o1_ref), (w2_ref, o2_ref)):
        for cols in _col_chunks(w_ref.shape[1]):
            o_ref[:, cols] = _dot(h_ref[...], w_ref[:, cols].astype(BF16)).astype(o_ref.dtype)


def _norm_matmul_pair(x, g, w1, w2, out_dtype, *, tm, tn, name):
    m, d = x.shape
    n = w1.shape[1]
    assert w1.shape == w2.shape
    tm, tn = min(tm, m), min(tn, n)
    w_spec = pl.BlockSpec((d, tn), lambda i, j: (0, j))
    o_spec = pl.BlockSpec((tm, tn), lambda i, j: (i, j))
    limit = _vmem_limit(
        [_nbytes((tm, d), x.dtype)] + [_nbytes((d, tn), w1.dtype), _nbytes((tm, tn), out_dtype)] * 2,
        [_nbytes((tm, d), BF16), _nbytes((tm, d), F32)],
    )
    return pl.pallas_call(
        _norm_matmul_pair_kernel,
        grid=(m // tm, n // tn),
        in_specs=[pl.BlockSpec((tm, d), lambda i, j: (i, 0)), pl.BlockSpec((1, d), lambda i, j: (0, 0)),
                  w_spec, w_spec],
        out_specs=[o_spec, o_spec],
        out_shape=[jax.ShapeDtypeStruct((m, n), out_dtype)] * 2,
        scratch_shapes=[pltpu.VMEM((tm, d), BF16)],
        compiler_params=pltpu.CompilerParams(
            dimension_semantics=("arbitrary", "arbitrary"), vmem_limit_bytes=limit
        ),
        name=name,
    )(x, g.reshape(1, d), w1, w2)


def _matmul2_res_kernel(a1_ref, a2_ref, w1_ref, w2_ref, res_ref, o_ref):
    acc = _dot(a1_ref[...], w1_ref[...]) + _dot(a2_ref[...], w2_ref[...])
    o_ref[...] = res_ref[...] + acc


def _matmul2_res(a1, a2, w, res, *, tm, name):
    m, k1 = a1.shape
    k2 = a2.shape[1]
    assert k1 == k2 and w.shape[0] == k1 + k2
    n = w.shape[1]
    limit = _vmem_limit(
        [_nbytes((tm, k1), a1.dtype), _nbytes((tm, k2), a2.dtype),
         _nbytes((tm, n), F32), _nbytes((tm, n), F32)],
        [_nbytes(w.shape, w.dtype), _nbytes((tm, n), F32)],
    )
    resident = pl.Buffered(1)
    return pl.pallas_call(
        _matmul2_res_kernel,
        grid=(m // tm,),
        in_specs=[
            pl.BlockSpec((tm, k1), lambda i: (i, 0)),
            pl.BlockSpec((tm, k2), lambda i: (i, 0)),
            pl.BlockSpec((k1, n), lambda i: (0, 0), pipeline_mode=resident),
            pl.BlockSpec((k2, n), lambda i: (1, 0), pipeline_mode=resident),
            pl.BlockSpec((tm, n), lambda i: (i, 0)),
        ],
        out_specs=pl.BlockSpec((tm, n), lambda i: (i, 0)),
        out_shape=jax.ShapeDtypeStruct((m, n), F32),
        compiler_params=pltpu.CompilerParams(
            dimension_semantics=("arbitrary",), vmem_limit_bytes=limit
        ),
        name=name,
    )(a1, a2, w, w, res)


def _bf16_part(x):
    bits = lax.bitcast_convert_type(x, jnp.int32) & jnp.int32(-65536)
    return lax.bitcast_convert_type(bits, F32)


def _diff_attn_body(ins, o_ref, scratch, *, tq, lam_init):
    slope_ref, lq1_ref, lk1_ref, lq2_ref, lk2_ref, g_ref, q_ref, k_ref, v_ref = ins
    kx_ref, s_ref, p_ref = scratch
    s_len = q_ref.shape[0]
    d = DA_HEAD_DIM
    lam = (jnp.exp(jnp.sum(lq1_ref[...] * lk1_ref[...], axis=-1, keepdims=True))
           - jnp.exp(jnp.sum(lq2_ref[...] * lk2_ref[...], axis=-1, keepdims=True))
           + lam_init)

    lane = lax.broadcasted_iota(jnp.int32, (1, d), 1)
    slope2 = jnp.full((1, d), slope_ref[pl.program_id(1)] * LOG2_E, F32)
    piece_hi = _bf16_part(slope2)
    rest = slope2 - piece_hi
    piece_mid = _bf16_part(rest)
    piece_lo = _bf16_part(rest - piece_mid)
    piece = jnp.where((lane == 0) | (lane == 3), piece_hi,
                      jnp.where((lane == 1) | (lane == 4), piece_mid, piece_lo))
    q_extra = jnp.where(lane < 3, piece * 256.0, jnp.where(lane < 6, piece, 0.0))
    q_extra = jnp.broadcast_to(q_extra, (tq, d)).astype(BF16)
    kpos = lax.broadcasted_iota(jnp.int32, (s_len, d), 0)
    klane = lax.broadcasted_iota(jnp.int32, (s_len, d), 1)
    k_extra = jnp.where(klane < 3, kpos >> 8, jnp.where(klane < 6, kpos & 255, 0))
    kx_ref[...] = k_extra.astype(F32).astype(kx_ref.dtype)

    row = lax.broadcasted_iota(jnp.int32, (tq, tq), 0)
    col = lax.broadcasted_iota(jnp.int32, (tq, tq), 1)
    causal = col <= row

    for qi in reversed(range(s_len // tq)):
        lo, hi = qi * tq, (qi + 1) * tq
        key_blocks = [slice(j * tq, (j + 1) * tq) for j in range(qi + 1)]
        heads = []
        for c in range(2):
            dcols = slice(c * d, (c + 1) * d)
            q_aug = jnp.concatenate([q_ref[lo:hi, dcols], q_extra], axis=1)
            m = None
            for j, cols in enumerate(key_blocks):
                k_aug = jnp.concatenate([k_ref[cols, dcols], kx_ref[cols, :]], axis=1)
                s = _dot_nt(q_aug, k_aug)
                if j == qi:
                    s = jnp.where(causal, s, NEG_INF)
                s_ref[c, :, cols] = s
                bm = _fold_lanes(s, jnp.maximum)
                m = bm if m is None else jnp.maximum(m, bm)
            m = jnp.max(m, axis=-1, keepdims=True)
            l = None
            for cols in key_blocks:
                p = jnp.exp2(s_ref[c, :, cols] - m)
                bl = _fold_lanes(p, jnp.add)
                l = bl if l is None else l + bl
                p_ref[c, :, cols] = p.astype(p_ref.dtype)
            l = jnp.sum(l, axis=-1, keepdims=True)
            acc = _dot(p_ref[c, :, 0:hi], v_ref[0:hi, :])
            heads.append((acc, l))
        (acc1, l1), (acc2, l2) = heads
        out = acc1 * (1.0 / l1) - acc2 * (lam / l2)
        y = _rms(out) * g_ref[...] * (1.0 - lam_init)
        o_ref[lo:hi, :] = y.astype(o_ref.dtype)
        yield


def _retention_body(ins, o_ref):
    lg_ref, q_ref, k_ref, v_ref, gate_ref = ins
    s_len = q_ref.shape[0]
    c = RET_CHUNK
    dk, dv = RET_QK_DIM, RET_V_DIM
    scale = dk ** -0.5
    lg = lg_ref[pl.program_id(1)]

    row = lax.broadcasted_iota(jnp.int32, (c, c), 0)
    col = lax.broadcasted_iota(jnp.int32, (c, c), 1)
    diff = (row - col).astype(F32)
    intra = jnp.where(diff >= 0, jnp.exp(lg * jnp.maximum(diff, 0.0)), 0.0) * scale
    row_k = lax.broadcasted_iota(jnp.int32, (c, dk), 0).astype(F32)
    k_decay = jnp.exp(lg * (float(c - 1) - row_k)) * scale
    row_v = lax.broadcasted_iota(jnp.int32, (c, dv), 0).astype(F32)
    q_decay = jnp.exp(lg * (row_v + 1.0))
    chunk_decay = jnp.exp(jnp.full((1, dv), lg * float(c), F32))

    state = jnp.zeros((dk, dv), F32)
    for i in range(s_len // c):
        lo, hi = i * c, (i + 1) * c
        q = q_ref[lo:hi, :]
        k = k_ref[lo:hi, :]
        v = v_ref[lo:hi, :]
        scores = _dot_nt(q, k) * intra
        y = _dot(scores.astype(BF16), v)
        if i > 0:
            y = y + _dot(q, state.astype(BF16)) * q_decay
        if i + 1 < s_len // c:
            kd = (k.astype(F32) * k_decay).astype(BF16)
            state = state * chunk_decay + _dot_tn(kd, v)
        gate = gate_ref[lo:hi, :].astype(F32)
        o_ref[lo:hi, :] = (_rms(y) * (gate * jax.nn.sigmoid(gate))).astype(o_ref.dtype)
        yield


N_ATTN_INS, N_RET_INS = 9, 5


def _mixers_kernel(*refs, n_side, tq, lam_init):
    ins, rest = refs[:N_ATTN_INS + N_RET_INS], refs[N_ATTN_INS + N_RET_INS:]
    side_in, rest = rest[:n_side], rest[n_side:]
    da_ref, ret_ref = rest[:2]
    side_out, scratch = rest[2:2 + n_side], rest[2 + n_side:]
    _cast_blocks(zip(side_in, side_out))
    pending = [_retention_body(ins[N_ATTN_INS:], ret_ref),
               _diff_attn_body(ins[:N_ATTN_INS], da_ref, scratch, tq=tq, lam_init=lam_init)]
    while pending:
        pending = [body for body in pending if next(body, StopIteration) is not StopIteration]


def _mixers(proj, slopes, lq1, lk1, lq2, lk2, g, log_gammas, *, batch, seq, lam_init, tq,
            side_weights=()):
    assert DA_HEADS == RET_HEADS
    hb = DA_V_DIM
    q_blk0, k_blk0, v_blk0 = 0, DA_HEADS, 2 * DA_HEADS
    da_cols = 3 * DA_HEADS * DA_V_DIM
    rq_blk0 = da_cols // RET_QK_DIM
    rk_blk0 = rq_blk0 + RET_HEADS
    rv_blk0 = (da_cols + 2 * RET_HEADS * RET_QK_DIM) // RET_V_DIM
    rg_blk0 = rv_blk0 + RET_HEADS
    grid = (batch, DA_HEADS)
    side = _SideCasts(side_weights, grid)
    scratch = [
        pltpu.VMEM((seq, DA_HEAD_DIM), BF16),
        pltpu.VMEM((2, tq, seq), F32),
        pltpu.VMEM((2, tq, seq), BF16),
    ]
    limit = _vmem_limit(
        [_nbytes((seq, hb), proj.dtype)] * 4
        + [_nbytes((seq, RET_QK_DIM), proj.dtype)] * 2 + [_nbytes((seq, RET_V_DIM), proj.dtype)] * 3
        + side.window_bytes,
        [_nbytes((seq, DA_HEAD_DIM), BF16),
         3 * _nbytes((2, tq, seq), F32), 3 * _nbytes((2, tq, seq), BF16)],
    )
    vec = pl.BlockSpec((1, DA_HEAD_DIM), lambda b, h: (0, 0))
    smem = pl.BlockSpec(memory_space=pltpu.SMEM)

    def head_cols(width, blk0):
        return pl.BlockSpec((seq, width), lambda b, h: (b, blk0 + h))

    outs = pl.pallas_call(
        functools.partial(_mixers_kernel, n_side=len(side), tq=tq, lam_init=lam_init),
        grid=grid,
        in_specs=[
            smem, vec, vec, vec, vec,
            pl.BlockSpec((1, hb), lambda b, h: (0, 0)),
            head_cols(hb, q_blk0), head_cols(hb, k_blk0), head_cols(hb, v_blk0),
            smem,
            head_cols(RET_QK_DIM, rq_blk0), head_cols(RET_QK_DIM, rk_blk0),
            head_cols(RET_V_DIM, rv_blk0), head_cols(RET_V_DIM, rg_blk0),
        ] + side.in_specs,
        out_specs=[head_cols(hb, 0), head_cols(RET_V_DIM, 0)] + side.out_specs,
        out_shape=[jax.ShapeDtypeStruct((batch * seq, DA_HEADS * hb), BF16),
                   jax.ShapeDtypeStruct((batch * seq, RET_HEADS * RET_V_DIM), BF16)] + side.out_shapes,
        scratch_shapes=scratch,
        compiler_params=pltpu.CompilerParams(
            dimension_semantics=("arbitrary", "arbitrary"), vmem_limit_bytes=limit
        ),
        name="token_mixers",
    )(slopes, lq1.reshape(1, -1), lk1.reshape(1, -1), lq2.reshape(1, -1), lk2.reshape(1, -1),
      g.reshape(1, hb), proj, proj, proj, log_gammas, proj, proj, proj, proj, *side_weights)
    return tuple(outs)


def _xattn_kernel(*refs, n_side):
    ins, o_ref, (xo_ref,), side = _split_refs(refs, 5, n_side)
    xq_ref, xk_ref, xv_ref, res_ref, wo_ref = ins
    _cast_blocks(side)
    d_model = xq_ref.shape[1]
    hd = d_model // XATTN_HEADS
    scale2 = hd ** -0.5 * LOG2_E
    for h in range(XATTN_HEADS):
        cols = slice(h * hd, (h + 1) * hd)
        s = _dot_nt(xq_ref[:, cols], xk_ref[:, cols]) * scale2
        p = jnp.exp2(s - jnp.max(s, axis=-1, keepdims=True))
        p = p * (1.0 / jnp.sum(p, axis=-1, keepdims=True))
        xo_ref[:, cols] = _dot(p.astype(BF16), xv_ref[:, cols]).astype(xo_ref.dtype)
    o_ref[...] = res_ref[...] + _dot(xo_ref[...], wo_ref[...])


def _cross_attention(xq, xk, xv, res, wo, *, batch, seq, mem_len, tq, side_weights=()):
    d = xq.shape[1]
    nq = seq // tq
    grid = (batch, nq)
    side = _SideCasts(side_weights, grid)
    limit = _vmem_limit(
        [
            _nbytes((tq, d), xq.dtype),
            _nbytes((mem_len, d), xk.dtype),
            _nbytes((mem_len, d), xv.dtype),
            _nbytes((tq, d), F32),
            _nbytes((d, d), wo.dtype),
            _nbytes((tq, d), F32),
        ] + side.window_bytes,
        [_nbytes((tq, d), BF16), _nbytes((tq, d), F32)],
    )
    outs = pl.pallas_call(
        functools.partial(_xattn_kernel, n_side=len(side)),
        grid=grid,
        in_specs=[
            pl.BlockSpec((tq, d), lambda b, i: (b * nq + i, 0)),
            pl.BlockSpec((mem_len, d), lambda b, i: (b, 0)),
            pl.BlockSpec((mem_len, d), lambda b, i: (b, 0)),
            pl.BlockSpec((tq, d), lambda b, i: (b * nq + i, 0)),
            pl.BlockSpec((d, d), lambda b, i: (0, 0)),
        ] + side.in_specs,
        out_specs=[pl.BlockSpec((tq, d), lambda b, i: (b * nq + i, 0))] + side.out_specs,
        out_shape=[jax.ShapeDtypeStruct((batch * seq, d), F32)] + side.out_shapes,
        scratch_shapes=[pltpu.VMEM((tq, d), BF16)],
        compiler_params=pltpu.CompilerParams(
            dimension_semantics=("arbitrary", "arbitrary"), vmem_limit_bytes=limit
        ),
        name="cross_attention",
    )(xq, xk, xv, res, wo, *side_weights)
    return tuple(outs)


def _ffn_kernel(x_ref, g_ref, wg_ref, wu_ref, wd_ref, gf_ref, o_ref, h_ref, *, final_norm):
    f = pl.program_id(1)
    last = pl.num_programs(1) - 1
    blocks = _row_blocks(x_ref.shape[0])

    def partial_ffn(h, chunked=False):
        hidden = wg_ref.shape[1]
        out = None
        for cols in (_col_chunks(hidden) if chunked else [slice(0, hidden)]):
            gate = _dot(h, wg_ref[:, cols])
            up = _dot(h, wu_ref[:, cols])
            act = (gate * jax.nn.sigmoid(gate)) * up
            part = _dot(act.astype(BF16), wd_ref[cols, :].astype(BF16))
            out = part if out is None else out + part
        return out

    @pl.when(f == 0)
    def _():
        for rows in blocks:
            x = x_ref[rows, :]
            h = (_rms(x) * g_ref[...]).astype(h_ref.dtype)
            h_ref[rows, :] = h
            o_ref[rows, :] = x + partial_ffn(h)

    if final_norm:
        @pl.when(jnp.logical_and(f > 0, f < last))
        def _():
            o_ref[...] += partial_ffn(h_ref[...], chunked=True)

        @pl.when(f == last)
        def _():
            for rows in blocks:
                y = o_ref[rows, :] + partial_ffn(h_ref[rows, :])
                o_ref[rows, :] = _rms(y) * gf_ref[...]
    else:
        @pl.when(f > 0)
        def _():
            o_ref[...] += partial_ffn(h_ref[...], chunked=True)


def _ffn(x, g, wg, wu, wd, gf, *, final_norm, tm, tf):
    m, d = x.shape
    d_ff = wg.shape[1]
    limit = _vmem_limit(
        [
            _nbytes((tm, d), F32),
            _nbytes((d, tf), wg.dtype),
            _nbytes((d, tf), wu.dtype),
            _nbytes((tf, d), wd.dtype),
            _nbytes((tm, d), F32),
        ],
        [_nbytes((tm, d), BF16), _nbytes((tm, d), F32), 4 * _nbytes((tm, tf), F32)],
    )
    return pl.pallas_call(
        functools.partial(_ffn_kernel, final_norm=final_norm),
        grid=(m // tm, d_ff // tf),
        in_specs=[
            pl.BlockSpec((tm, d), lambda i, f: (i, 0)),
            pl.BlockSpec((1, d), lambda i, f: (0, 0)),
            pl.BlockSpec((d, tf), lambda i, f: (0, f)),
            pl.BlockSpec((d, tf), lambda i, f: (0, f)),
            pl.BlockSpec((tf, d), lambda i, f: (f, 0)),
            pl.BlockSpec((1, d), lambda i, f: (0, 0)),
        ],
        out_specs=pl.BlockSpec((tm, d), lambda i, f: (i, 0)),
        out_shape=jax.ShapeDtypeStruct((m, d), F32),
        scratch_shapes=[pltpu.VMEM((tm, d), BF16)],
        compiler_params=pltpu.CompilerParams(
            dimension_semantics=("parallel", "arbitrary"), vmem_limit_bytes=limit
        ),
        name="swiglu_ffn",
    )(x, g.reshape(1, d), wg, wu, wd, gf.reshape(1, d))


def kernel(x, mem, norm_mix_g, w_in, lambda_q1, lambda_k1, lambda_q2, lambda_k2, da_subln_g, w_o, norm_x_g, norm_mem_g, w_xq, w_xk, w_xv, w_xo, norm_ffn_g, w_gate, w_up, w_down, norm_f_g):
    batch, seq, d_model = x.shape
    mem_len = mem.shape[1]
    depth = w_in.shape[0]
    slopes = jnp.asarray(2.0 ** (-8.0 * np.arange(1, DA_HEADS + 1) / DA_HEADS), dtype=F32)
    log_gammas = jnp.asarray(np.log(1.0 - 2.0 ** (-5.0 - np.arange(RET_HEADS))), dtype=F32)

    xf = x.reshape(batch * seq, d_model)
    memf = mem.reshape(batch * mem_len, d_model)
    for l in range(depth):
        lam_init = 0.8 - 0.6 * math.exp(-0.3 * l)
        n_dq = DA_HEADS * 2 * DA_HEAD_DIM
        col_scale = jnp.where(jnp.arange(w_in.shape[2]) < n_dq, DA_HEAD_DIM ** -0.5 * LOG2_E, 1.0)
        proj, = _norm_matmul(
            xf, norm_mix_g[l], w_in[l], BF16, tm=ROW_TILE, tn=COL_TILE,
            group_tiles=IN_PROJ_GROUP_TILES, name="in_proj", col_scale=col_scale.astype(F32))
        da, ret, wb_up, wb_o, wb_xq, wb_xo = _mixers(
            proj, slopes, lambda_q1[l], lambda_k1[l], lambda_q2[l], lambda_k2[l], da_subln_g[l],
            log_gammas, batch=batch, seq=seq, lam_init=lam_init, tq=ATTN_Q_TILE,
            side_weights=(w_up[l], w_o[l], w_xq[l], w_xo[l]))
        xf = _matmul2_res(da, ret, wb_o, xf, tm=OUT_PROJ_ROW_TILE, name="out_proj")

        xq, wb_gate = _norm_matmul(xf, norm_x_g[l], wb_xq, BF16, tm=ROW_TILE, tn=d_model,
                                   name="xattn_q", side_weights=(w_gate[l],))
        xk, xv = _norm_matmul_pair(memf, norm_mem_g[l], w_xk[l], w_xv[l], BF16, tm=ROW_TILE,
                                   tn=KV_COL_TILE, name="xattn_kv")
        xf, = _cross_attention(xq, xk, xv, xf, wb_xo, batch=batch, seq=seq, mem_len=mem_len,
                               tq=XATTN_Q_TILE)

        xf = _ffn(xf, norm_ffn_g[l], wb_gate, wb_up, w_down[l], norm_f_g,
                  final_norm=(l == depth - 1), tm=FFN_ROW_TILE, tf=FFN_COL_TILE)
    return xf.reshape(batch, seq, d_model)
```

```python
import functools
import math

import jax
import jax.numpy as jnp
import numpy as np
from jax import lax
from jax.experimental import pallas as pl
from jax.experimental.pallas import tpu as pltpu

F32 = jnp.float32
BF16 = jnp.bfloat16

DA_HEADS = 4
DA_HEAD_DIM = 128
DA_V_DIM = 2 * DA_HEAD_DIM
RET_HEADS = 4
RET_QK_DIM = 128
RET_V_DIM = 256
XATTN_HEADS = 4
RET_CHUNK = 256
NORM_EPS = 1e-6
NEG_INF = -1e30
LOG2_E = math.log2(math.e)
NORM_BLOCK_ROWS = 256

ROW_TILE = 1024
COL_TILE = 1024
KV_COL_TILE = 512
IN_PROJ_GROUP_TILES = 2
OUT_PROJ_ROW_TILE = 512
ATTN_Q_TILE = 512
XATTN_Q_TILE = 512
FFN_ROW_TILE, FFN_COL_TILE = 1024, 512

V7X_LANES = 128
MXU_COLS = 256
BF16_TILE_ROWS = 16
V7X_VMEM_BYTES = 64 * 1024 * 1024
V7X_VMEM_USABLE_BYTES = V7X_VMEM_BYTES - 6 * 1024 * 1024
COMPILER_SCRATCH_BYTES = 4 * 1024 * 1024


def _nbytes(shape, dtype):
    return int(np.prod(shape)) * jnp.dtype(dtype).itemsize


def _vmem_limit(pipelined, resident):
    need = 2 * sum(pipelined) + sum(resident) + COMPILER_SCRATCH_BYTES
    return int(min(V7X_VMEM_USABLE_BYTES, need))


def _rms(x):
    return x * lax.rsqrt(jnp.mean(x * x, axis=-1, keepdims=True) + NORM_EPS)


def _fold_lanes(x, op):
    tiles = [x[:, i:i + V7X_LANES] for i in range(0, x.shape[1], V7X_LANES)]
    return functools.reduce(op, tiles)


def _dot(a, b):
    return jnp.dot(a, b, preferred_element_type=F32)


def _dot_nt(a, b):
    return lax.dot_general(a, b, (((1,), (1,)), ((), ())), preferred_element_type=F32)


def _dot_tn(a, b):
    return lax.dot_general(a, b, (((0,), (0,)), ((), ())), preferred_element_type=F32)


class _SideCasts:
    def __init__(self, weights, grid):
        self.weights = list(weights)
        self.grid = tuple(grid)
        n_steps = int(np.prod(self.grid))
        self.plans = []
        for w in self.weights:
            rows, n_blocks = w.shape[0], n_steps
            while rows % n_blocks or (rows // n_blocks) % BF16_TILE_ROWS:
                n_blocks -= 1
            self.plans.append((n_blocks, rows // n_blocks))

    def __len__(self):
        return len(self.weights)

    def _specs(self):
        specs = []
        for w, (n_blocks, block_rows) in zip(self.weights, self.plans):
            def index(*ids, n_blocks=n_blocks):
                step = ids[0]
                for extent, idx in zip(self.grid[1:], ids[1:]):
                    step = step * extent + idx
                return (jnp.minimum(step, n_blocks - 1), 0)
            specs.append(pl.BlockSpec((block_rows, w.shape[1]), index))
        return specs

    in_specs = property(_specs)
    out_specs = property(_specs)

    @property
    def out_shapes(self):
        return [jax.ShapeDtypeStruct(w.shape, BF16) for w in self.weights]

    @property
    def window_bytes(self):
        return [_nbytes((rows, w.shape[1]), dt)
                for w, (_, rows) in zip(self.weights, self.plans) for dt in (w.dtype, BF16)]


def _split_refs(refs, n_in, n_side):
    ins, rest = refs[:n_in], refs[n_in:]
    side_in, rest = rest[:n_side], rest[n_side:]
    out, side_out, scratch = rest[0], rest[1:1 + n_side], rest[1 + n_side:]
    return ins, out, scratch, list(zip(side_in, side_out))


def _cast_blocks(pairs):
    for src, dst in pairs:
        dst[...] = src[...].astype(dst.dtype)


def _row_blocks(n_rows):
    step = min(NORM_BLOCK_ROWS, n_rows)
    return [slice(r, r + step) for r in range(0, n_rows, step)]


def _col_chunks(n_cols):
    step = min(MXU_COLS, n_cols)
    return [slice(c, c + step) for c in range(0, n_cols, step)]


def _norm_matmul_kernel(*refs, n_side, cast_w, group_tiles):
    ins, o_ref, scratch, side = _split_refs(refs, 4 if cast_w else 3, n_side)
    x_ref, g_ref, w_ref = ins[:3]
    h_ref = scratch[0]
    wb_ref = scratch[1] if cast_w else w_ref
    step = pl.program_id(1)
    tm = x_ref.shape[0]

    def cast_weight(cols):
        wb_ref[:, cols] = (w_ref[:, cols] * ins[3][:, cols]).astype(wb_ref.dtype)

    for t in range(group_tiles):
        @pl.when(step == t)
        def _(t=t):
            _cast_blocks(side)
            if cast_w and t == 0:
                cast_weight(slice(None))
            for rows in _row_blocks(tm):
                group_rows = slice(t * tm + rows.start, t * tm + rows.stop)
                h = (_rms(x_ref[rows, :]) * g_ref[...]).astype(h_ref.dtype)
                h_ref[group_rows, :] = h
                o_ref[group_rows, :] = _dot(h, wb_ref[...]).astype(o_ref.dtype)

    @pl.when(step >= group_tiles)
    def _():
        _cast_blocks(side)
        if cast_w:
            for cols in _col_chunks(w_ref.shape[1]):
                cast_weight(cols)
                o_ref[:, cols] = _dot(h_ref[...], wb_ref[:, cols]).astype(o_ref.dtype)
        else:
            o_ref[...] = _dot(h_ref[...], w_ref[...]).astype(o_ref.dtype)


def _norm_matmul(x, g, w, out_dtype, *, tm, tn, name, group_tiles=1, col_scale=None,
                 side_weights=()):
    m, d = x.shape
    n = w.shape[1]
    tm, tn = min(tm, m), min(tn, n)
    gt = group_tiles
    grid = (m // (tm * gt), gt + n // tn - 1)
    side = _SideCasts(side_weights, grid)
    cast_w = w.dtype != BF16
    assert cast_w or col_scale is None

    def col_tile(s):
        return jnp.maximum(s - (gt - 1), 0)

    weight_ins, weight_specs = [w], [pl.BlockSpec((d, tn), lambda gi, s: (0, col_tile(s)))]
    if cast_w:
        cs = jnp.ones((n,), F32) if col_scale is None else col_scale
        weight_ins.append(cs.reshape(1, n))
        weight_specs.append(pl.BlockSpec((1, tn), lambda gi, s: (0, col_tile(s))))
    scratch = [pltpu.VMEM((gt * tm, d), BF16)] + ([pltpu.VMEM((d, tn), BF16)] if cast_w else [])
    limit = _vmem_limit(
        [_nbytes((tm, d), x.dtype), _nbytes((d, tn), w.dtype), _nbytes((gt * tm, tn), out_dtype),
         _nbytes((8, d), F32), _nbytes((8, tn), F32)]
        + side.window_bytes,
        [_nbytes((gt * tm, d), BF16), _nbytes((d, tn), BF16) * cast_w],
    )
    outs = pl.pallas_call(
        functools.partial(_norm_matmul_kernel, n_side=len(side), cast_w=cast_w, group_tiles=gt),
        grid=grid,
        in_specs=[
            pl.BlockSpec((tm, d), lambda gi, s: (gi * gt + jnp.minimum(s, gt - 1), 0)),
            pl.BlockSpec((1, d), lambda gi, s: (0, 0)),
        ] + weight_specs + side.in_specs,
        out_specs=[pl.BlockSpec((gt * tm, tn), lambda gi, s: (gi, col_tile(s)))] + side.out_specs,
        out_shape=[jax.ShapeDtypeStruct((m, n), out_dtype)] + side.out_shapes,
        scratch_shapes=scratch,
        compiler_params=pltpu.CompilerParams(
            dimension_semantics=("arbitrary", "arbitrary"), vmem_limit_bytes=limit
        ),
        name=name,
    )(x, g.reshape(1, d), *weight_ins, *side_weights)
    return tuple(outs)


def _norm_matmul_pair_kernel(x_ref, g_ref, w1_ref, w2_ref, o1_ref, o2_ref, h_ref):
    @pl.when(pl.program_id(1) == 0)
    def _():
        h_ref[...] = (_rms(x_ref[...]) * g_ref[...]).astype(h_ref.dtype)

    for w_ref, o_ref in ((w1_ref, o1_ref), (w2_ref, o2_ref)):
        for cols in _col_chunks(w_ref.shape[1]):
            o_ref[:, cols] = _dot(h_ref[...], w_ref[:, cols].astype(BF16)).astype(o_ref.dtype)


def _norm_matmul_pair(x, g, w1, w2, out_dtype, *, tm, tn, name):
    m, d = x.shape
    n = w1.shape[1]
    assert w1.shape == w2.shape
    tm, tn = min(tm, m), min(tn, n)
    w_spec = pl.BlockSpec((d, tn), lambda i, j: (0, j))
    o_spec = pl.BlockSpec((tm, tn), lambda i, j: (i, j))
    limit = _vmem_limit(
        [_nbytes((tm, d), x.dtype)] + [_nbytes((d, tn), w1.dtype), _nbytes((tm, tn), out_dtype)] * 2,
        [_nbytes((tm, d), BF16), _nbytes((tm, d), F32)],
    )
    return pl.pallas_call(
        _norm_matmul_pair_kernel,
        grid=(m // tm, n // tn),
        in_specs=[pl.BlockSpec((tm, d), lambda i, j: (i, 0)), pl.BlockSpec((1, d), lambda i, j: (0, 0)),
                  w_spec, w_spec],
        out_specs=[o_spec, o_spec],
        out_shape=[jax.ShapeDtypeStruct((m, n), out_dtype)] * 2,
        scratch_shapes=[pltpu.VMEM((tm, d), BF16)],
        compiler_params=pltpu.CompilerParams(
            dimension_semantics=("arbitrary", "arbitrary"), vmem_limit_bytes=limit
        ),
        name=name,
    )(x, g.reshape(1, d), w1, w2)


def _matmul2_res_kernel(a1_ref, a2_ref, w1_ref, w2_ref, res_ref, o_ref):
    acc = _dot(a1_ref[...], w1_ref[...]) + _dot(a2_ref[...], w2_ref[...])
    o_ref[...] = res_ref[...] + acc


def _matmul2_res(a1, a2, w, res, *, tm, name):
    m, k1 = a1.shape
    k2 = a2.shape[1]
    assert k1 == k2 and w.shape[0] == k1 + k2
    n = w.shape[1]
    limit = _vmem_limit(
        [_nbytes((tm, k1), a1.dtype), _nbytes((tm, k2), a2.dtype),
         _nbytes((tm, n), F32), _nbytes((tm, n), F32)],
        [_nbytes(w.shape, w.dtype), _nbytes((tm, n), F32)],
    )
    resident = pl.Buffered(1)
    return pl.pallas_call(
        _matmul2_res_kernel,
        grid=(m // tm,),
        in_specs=[
            pl.BlockSpec((tm, k1), lambda i: (i, 0)),
            pl.BlockSpec((tm, k2), lambda i: (i, 0)),
            pl.BlockSpec((k1, n), lambda i: (0, 0), pipeline_mode=resident),
            pl.BlockSpec((k2, n), lambda i: (1, 0), pipeline_mode=resident),
            pl.BlockSpec((tm, n), lambda i: (i, 0)),
        ],
        out_specs=pl.BlockSpec((tm, n), lambda i: (i, 0)),
        out_shape=jax.ShapeDtypeStruct((m, n), F32),
        compiler_params=pltpu.CompilerParams(
            dimension_semantics=("arbitrary",), vmem_limit_bytes=limit
        ),
        name=name,
    )(a1, a2, w, w, res)


def _bf16_part(x):
    bits = lax.bitcast_convert_type(x, jnp.int32) & jnp.int32(-65536)
    return lax.bitcast_convert_type(bits, F32)


def _diff_attn_body(ins, o_ref, scratch, *, tq, lam_init):
    slope_ref, lq1_ref, lk1_ref, lq2_ref, lk2_ref, g_ref, q_ref, k_ref, v_ref = ins
    kx_ref, s_ref, p_ref = scratch
    s_len = q_ref.shape[0]
    d = DA_HEAD_DIM
    lam = (jnp.exp(jnp.sum(lq1_ref[...] * lk1_ref[...], axis=-1, keepdims=True))
           - jnp.exp(jnp.sum(lq2_ref[...] * lk2_ref[...], axis=-1, keepdims=True))
           + lam_init)

    lane = lax.broadcasted_iota(jnp.int32, (1, d), 1)
    slope2 = jnp.full((1, d), slope_ref[pl.program_id(1)] * LOG2_E, F32)
    piece_hi = _bf16_part(slope2)
    rest = slope2 - piece_hi
    piece_mid = _bf16_part(rest)
    piece_lo = _bf16_part(rest - piece_mid)
    piece = jnp.where((lane == 0) | (lane == 3), piece_hi,
                      jnp.where((lane == 1) | (lane == 4), piece_mid, piece_lo))
    q_extra = jnp.where(lane < 3, piece * 256.0, jnp.where(lane < 6, piece, 0.0))
    q_extra = jnp.broadcast_to(q_extra, (tq, d)).astype(BF16)
    kpos = lax.broadcasted_iota(jnp.int32, (s_len, d), 0)
    klane = lax.broadcasted_iota(jnp.int32, (s_len, d), 1)
    k_extra = jnp.where(klane < 3, kpos >> 8, jnp.where(klane < 6, kpos & 255, 0))
    kx_ref[...] = k_extra.astype(F32).astype(kx_ref.dtype)

    row = lax.broadcasted_iota(jnp.int32, (tq, tq), 0)
    col = lax.broadcasted_iota(jnp.int32, (tq, tq), 1)
    causal = col <= row

    for qi in reversed(range(s_len // tq)):
        lo, hi = qi * tq, (qi + 1) * tq
        key_blocks = [slice(j * tq, (j + 1) * tq) for j in range(qi + 1)]
        heads = []
        for c in range(2):
            dcols = slice(c * d, (c + 1) * d)
            q_aug = jnp.concatenate([q_ref[lo:hi, dcols], q_extra], axis=1)
            m = None
            for j, cols in enumerate(key_blocks):
                k_aug = jnp.concatenate([k_ref[cols, dcols], kx_ref[cols, :]], axis=1)
                s = _dot_nt(q_aug, k_aug)
                if j == qi:
                    s = jnp.where(causal, s, NEG_INF)
                s_ref[c, :, cols] = s
                bm = _fold_lanes(s, jnp.maximum)
                m = bm if m is None else jnp.maximum(m, bm)
            m = jnp.max(m, axis=-1, keepdims=True)
            l = None
            for cols in key_blocks:
                p = jnp.exp2(s_ref[c, :, cols] - m)
                bl = _fold_lanes(p, jnp.add)
                l = bl if l is None else l + bl
                p_ref[c, :, cols] = p.astype(p_ref.dtype)
            l = jnp.sum(l, axis=-1, keepdims=True)
            acc = _dot(p_ref[c, :, 0:hi], v_ref[0:hi, :])
            heads.append((acc, l))
        (acc1, l1), (acc2, l2) = heads
        out = acc1 * (1.0 / l1) - acc2 * (lam / l2)
        y = _rms(out) * g_ref[...] * (1.0 - lam_init)
        o_ref[lo:hi, :] = y.astype(o_ref.dtype)
        yield


def _retention_body(ins, o_ref):
    lg_ref, q_ref, k_ref, v_ref, gate_ref = ins
    s_len = q_ref.shape[0]
    c = RET_CHUNK
    dk, dv = RET_QK_DIM, RET_V_DIM
    scale = dk ** -0.5
    lg = lg_ref[pl.program_id(1)]

    row = lax.broadcasted_iota(jnp.int32, (c, c), 0)
    col = lax.broadcasted_iota(jnp.int32, (c, c), 1)
    diff = (row - col).astype(F32)
    intra = jnp.where(diff >= 0, jnp.exp(lg * jnp.maximum(diff, 0.0)), 0.0) * scale
    row_k = lax.broadcasted_iota(jnp.int32, (c, dk), 0).astype(F32)
    k_decay = jnp.exp(lg * (float(c - 1) - row_k)) * scale
    row_v = lax.broadcasted_iota(jnp.int32, (c, dv), 0).astype(F32)
    q_decay = jnp.exp(lg * (row_v + 1.0))
    chunk_decay = jnp.exp(jnp.full((1, dv), lg * float(c), F32))

    state = jnp.zeros((dk, dv), F32)
    for i in range(s_len // c):
        lo, hi = i * c, (i + 1) * c
        q = q_ref[lo:hi, :]
        k = k_ref[lo:hi, :]
        v = v_ref[lo:hi, :]
        scores = _dot_nt(q, k) * intra
        y = _dot(scores.astype(BF16), v)
        if i > 0:
            y = y + _dot(q, state.astype(BF16)) * q_decay
        if i + 1 < s_len // c:
            kd = (k.astype(F32) * k_decay).astype(BF16)
            state = state * chunk_decay + _dot_tn(kd, v)
        gate = gate_ref[lo:hi, :].astype(F32)
        o_ref[lo:hi, :] = (_rms(y) * (gate * jax.nn.sigmoid(gate))).astype(o_ref.dtype)
        yield


N_ATTN_INS, N_RET_INS = 9, 5


def _mixers_kernel(*refs, n_side, tq, lam_init):
    ins, rest = refs[:N_ATTN_INS + N_RET_INS], refs[N_ATTN_INS + N_RET_INS:]
    side_in, rest = rest[:n_side], rest[n_side:]
    da_ref, ret_ref = rest[:2]
    side_out, scratch = rest[2:2 + n_side], rest[2 + n_side:]
    _cast_blocks(zip(side_in, side_out))
    pending = [_retention_body(ins[N_ATTN_INS:], ret_ref),
               _diff_attn_body(ins[:N_ATTN_INS], da_ref, scratch, tq=tq, lam_init=lam_init)]
    while pending:
        pending = [body for body in pending if next(body, StopIteration) is not StopIteration]


def _mixers(proj, slopes, lq1, lk1, lq2, lk2, g, log_gammas, *, batch, seq, lam_init, tq,
            side_weights=()):
    assert DA_HEADS == RET_HEADS
    hb = DA_V_DIM
    q_blk0, k_blk0, v_blk0 = 0, DA_HEADS, 2 * DA_HEADS
    da_cols = 3 * DA_HEADS * DA_V_DIM
    rq_blk0 = da_cols // RET_QK_DIM
    rk_blk0 = rq_blk0 + RET_HEADS
    rv_blk0 = (da_cols + 2 * RET_HEADS * RET_QK_DIM) // RET_V_DIM
    rg_blk0 = rv_blk0 + RET_HEADS
    grid = (batch, DA_HEADS)
    side = _SideCasts(side_weights, grid)
    scratch = [
        pltpu.VMEM((seq, DA_HEAD_DIM), BF16),
        pltpu.VMEM((2, tq, seq), F32),
        pltpu.VMEM((2, tq, seq), BF16),
    ]
    limit = _vmem_limit(
        [_nbytes((seq, hb), proj.dtype)] * 4
        + [_nbytes((seq, RET_QK_DIM), proj.dtype)] * 2 + [_nbytes((seq, RET_V_DIM), proj.dtype)] * 3
        + side.window_bytes,
        [_nbytes((seq, DA_HEAD_DIM), BF16),
         3 * _nbytes((2, tq, seq), F32), 3 * _nbytes((2, tq, seq), BF16)],
    )
    vec = pl.BlockSpec((1, DA_HEAD_DIM), lambda b, h: (0, 0))
    smem = pl.BlockSpec(memory_space=pltpu.SMEM)

    def head_cols(width, blk0):
        return pl.BlockSpec((seq, width), lambda b, h: (b, blk0 + h))

    outs = pl.pallas_call(
        functools.partial(_mixers_kernel, n_side=len(side), tq=tq, lam_init=lam_init),
        grid=grid,
        in_specs=[
            smem, vec, vec, vec, vec,
            pl.BlockSpec((1, hb), lambda b, h: (0, 0)),
            head_cols(hb, q_blk0), head_cols(hb, k_blk0), head_cols(hb, v_blk0),
            smem,
            head_cols(RET_QK_DIM, rq_blk0), head_cols(RET_QK_DIM, rk_blk0),
            head_cols(RET_V_DIM, rv_blk0), head_cols(RET_V_DIM, rg_blk0),
        ] + side.in_specs,
        out_specs=[head_cols(hb, 0), head_cols(RET_V_DIM, 0)] + side.out_specs,
        out_shape=[jax.ShapeDtypeStruct((batch * seq, DA_HEADS * hb), BF16),
                   jax.ShapeDtypeStruct((batch * seq, RET_HEADS * RET_V_DIM), BF16)] + side.out_shapes,
        scratch_shapes=scratch,
        compiler_params=pltpu.CompilerParams(
            dimension_semantics=("arbitrary", "arbitrary"), vmem_limit_bytes=limit
        ),
        name="token_mixers",
    )(slopes, lq1.reshape(1, -1), lk1.reshape(1, -1), lq2.reshape(1, -1), lk2.reshape(1, -1),
      g.reshape(1, hb), proj, proj, proj, log_gammas, proj, proj, proj, proj, *side_weights)
    return tuple(outs)


def _xattn_kernel(*refs, n_side):
    ins, o_ref, (xo_ref,), side = _split_refs(refs, 5, n_side)
    xq_ref, xk_ref, xv_ref, res_ref, wo_ref = ins
    _cast_blocks(side)
    d_model = xq_ref.shape[1]
    hd = d_model // XATTN_HEADS
    scale2 = hd ** -0.5 * LOG2_E
    for h in range(XATTN_HEADS):
        cols = slice(h * hd, (h + 1) * hd)
        s = _dot_nt(xq_ref[:, cols], xk_ref[:, cols]) * scale2
        p = jnp.exp2(s - jnp.max(s, axis=-1, keepdims=True))
        p = p * (1.0 / jnp.sum(p, axis=-1, keepdims=True))
        xo_ref[:, cols] = _dot(p.astype(BF16), xv_ref[:, cols]).astype(xo_ref.dtype)
    o_ref[...] = res_ref[...] + _dot(xo_ref[...], wo_ref[...])


def _cross_attention(xq, xk, xv, res, wo, *, batch, seq, mem_len, tq, side_weights=()):
    d = xq.shape[1]
    nq = seq // tq
    grid = (batch, nq)
    side = _SideCasts(side_weights, grid)
    limit = _vmem_limit(
        [
            _nbytes((tq, d), xq.dtype),
            _nbytes((mem_len, d), xk.dtype),
            _nbytes((mem_len, d), xv.dtype),
            _nbytes((tq, d), F32),
            _nbytes((d, d), wo.dtype),
            _nbytes((tq, d), F32),
        ] + side.window_bytes,
        [_nbytes((tq, d), BF16), _nbytes((tq, d), F32)],
    )
    outs = pl.pallas_call(
        functools.partial(_xattn_kernel, n_side=len(side)),
        grid=grid,
        in_specs=[
            pl.BlockSpec((tq, d), lambda b, i: (b * nq + i, 0)),
            pl.BlockSpec((mem_len, d), lambda b, i: (b, 0)),
            pl.BlockSpec((mem_len, d), lambda b, i: (b, 0)),
            pl.BlockSpec((tq, d), lambda b, i: (b * nq + i, 0)),
            pl.BlockSpec((d, d), lambda b, i: (0, 0)),
        ] + side.in_specs,
        out_specs=[pl.BlockSpec((tq, d), lambda b, i: (b * nq + i, 0))] + side.out_specs,
        out_shape=[jax.ShapeDtypeStruct((batch * seq, d), F32)] + side.out_shapes,
        scratch_shapes=[pltpu.VMEM((tq, d), BF16)],
        compiler_params=pltpu.CompilerParams(
            dimension_semantics=("arbitrary", "arbitrary"), vmem_limit_bytes=limit
        ),
        name="cross_attention",
    )(xq, xk, xv, res, wo, *side_weights)
    return tuple(outs)


def _ffn_kernel(x_ref, g_ref, wg_ref, wu_ref, wd_ref, gf_ref, o_ref, h_ref, *, final_norm):
    f = pl.program_id(1)
    last = pl.num_programs(1) - 1
    blocks = _row_blocks(x_ref.shape[0])

    def partial_ffn(h, chunked=False):
        hidden = wg_ref.shape[1]
        out = None
        for cols in (_col_chunks(hidden) if chunked else [slice(0, hidden)]):
            gate = _dot(h, wg_ref[:, cols])
            up = _dot(h, wu_ref[:, cols])
            act = (gate * jax.nn.sigmoid(gate)) * up
            part = _dot(act.astype(BF16), wd_ref[cols, :])
            out = part if out is None else out + part
        return out

    @pl.when(f == 0)
    def _():
        for rows in blocks:
            x = x_ref[rows, :]
            h = (_rms(x) * g_ref[...]).astype(h_ref.dtype)
            h_ref[rows, :] = h
            o_ref[rows, :] = x + partial_ffn(h)

    if final_norm:
        @pl.when(jnp.logical_and(f > 0, f < last))
        def _():
            o_ref[...] += partial_ffn(h_ref[...], chunked=True)

        @pl.when(f == last)
        def _():
            for rows in blocks:
                y = o_ref[rows, :] + partial_ffn(h_ref[rows, :])
                o_ref[rows, :] = _rms(y) * gf_ref[...]
    else:
        @pl.when(f > 0)
        def _():
            o_ref[...] += partial_ffn(h_ref[...], chunked=True)


def _ffn(x, g, wg, wu, wd, gf, *, final_norm, tm, tf):
    m, d = x.shape
    d_ff = wg.shape[1]
    limit = _vmem_limit(
        [
            _nbytes((tm, d), F32),
            _nbytes((d, tf), wg.dtype),
            _nbytes((d, tf), wu.dtype),
            _nbytes((tf, d), wd.dtype),
            _nbytes((tm, d), F32),
        ],
        [_nbytes((tm, d), BF16), _nbytes((tm, d), F32), 4 * _nbytes((tm, tf), F32)],
    )
    return pl.pallas_call(
        functools.partial(_ffn_kernel, final_norm=final_norm),
        grid=(m // tm, d_ff // tf),
        in_specs=[
            pl.BlockSpec((tm, d), lambda i, f: (i, 0)),
            pl.BlockSpec((1, d), lambda i, f: (0, 0)),
            pl.BlockSpec((d, tf), lambda i, f: (0, f)),
            pl.BlockSpec((d, tf), lambda i, f: (0, f)),
            pl.BlockSpec((tf, d), lambda i, f: (f, 0)),
            pl.BlockSpec((1, d), lambda i, f: (0, 0)),
        ],
        out_specs=pl.BlockSpec((tm, d), lambda i, f: (i, 0)),
        out_shape=jax.ShapeDtypeStruct((m, d), F32),
        scratch_shapes=[pltpu.VMEM((tm, d), BF16)],
        compiler_params=pltpu.CompilerParams(
            dimension_semantics=("parallel", "arbitrary"), vmem_limit_bytes=limit
        ),
        name="swiglu_ffn",
    )(x, g.reshape(1, d), wg, wu, wd, gf.reshape(1, d))


def kernel(x, mem, norm_mix_g, w_in, lambda_q1, lambda_k1, lambda_q2, lambda_k2, da_subln_g, w_o, norm_x_g, norm_mem_g, w_xq, w_xk, w_xv, w_xo, norm_ffn_g, w_gate, w_up, w_down, norm_f_g):
    batch, seq, d_model = x.shape
    mem_len = mem.shape[1]
    depth = w_in.shape[0]
    slopes = jnp.asarray(2.0 ** (-8.0 * np.arange(1, DA_HEADS + 1) / DA_HEADS), dtype=F32)
    log_gammas = jnp.asarray(np.log(1.0 - 2.0 ** (-5.0 - np.arange(RET_HEADS))), dtype=F32)

    xf = x.reshape(batch * seq, d_model)
    memf = mem.reshape(batch * mem_len, d_model)
    for l in range(depth):
        lam_init = 0.8 - 0.6 * math.exp(-0.3 * l)
        n_dq = DA_HEADS * 2 * DA_HEAD_DIM
        col_scale = jnp.where(jnp.arange(w_in.shape[2]) < n_dq, DA_HEAD_DIM ** -0.5 * LOG2_E, 1.0)
        proj, = _norm_matmul(
            xf, norm_mix_g[l], w_in[l], BF16, tm=ROW_TILE, tn=COL_TILE,
            group_tiles=IN_PROJ_GROUP_TILES, name="in_proj", col_scale=col_scale.astype(F32))
        da, ret, wb_up, wb_o, wb_xq = _mixers(
            proj, slopes, lambda_q1[l], lambda_k1[l], lambda_q2[l], lambda_k2[l], da_subln_g[l],
            log_gammas, batch=batch, seq=seq, lam_init=lam_init, tq=ATTN_Q_TILE,
            side_weights=(w_up[l], w_o[l], w_xq[l]))
        xf = _matmul2_res(da, ret, wb_o, xf, tm=OUT_PROJ_ROW_TILE, name="out_proj")

        xq, wb_gate, wb_xo = _norm_matmul(xf, norm_x_g[l], wb_xq, BF16, tm=ROW_TILE, tn=d_model,
                                          name="xattn_q", side_weights=(w_gate[l], w_xo[l]))
        xk, xv = _norm_matmul_pair(memf, norm_mem_g[l], w_xk[l], w_xv[l], BF16, tm=ROW_TILE,
                                   tn=KV_COL_TILE, name="xattn_kv")
        xf, wb_down = _cross_attention(xq, xk, xv, xf, wb_xo, batch=batch, seq=seq, mem_len=mem_len,
                                       tq=XATTN_Q_TILE, side_weights=(w_down[l],))

        xf = _ffn(xf, norm_ffn_g[l], wb_gate, wb_up, wb_down, norm_f_g,
                  final_norm=(l == depth - 1), tm=FFN_ROW_TILE, tf=FFN_COL_TILE)
    return xf.reshape(batch, seq, d_model)
```

```python
import functools
import math

import jax
import jax.numpy as jnp
import numpy as np
from jax import lax
from jax.experimental import pallas as pl
from jax.experimental.pallas import tpu as pltpu

F32 = jnp.float32
BF16 = jnp.bfloat16

DA_HEADS = 4
DA_HEAD_DIM = 128
DA_V_DIM = 2 * DA_HEAD_DIM
RET_HEADS = 4
RET_QK_DIM = 128
RET_V_DIM = 256
XATTN_HEADS = 4
RET_CHUNK = 256
NORM_EPS = 1e-6
NEG_INF = -1e30
LOG2_E = math.log2(math.e)
NORM_BLOCK_ROWS = 256

ROW_TILE = 1024
COL_TILE = 1024
KV_COL_TILE = 512
IN_PROJ_GROUP_TILES = 2
OUT_PROJ_ROW_TILE = 512
ATTN_Q_TILE = 512
XATTN_Q_TILE = 512
FFN_ROW_TILE, FFN_COL_TILE = 1024, 512

V7X_LANES = 128
MXU_COLS = 256
BF16_TILE_ROWS = 16
V7X_VMEM_BYTES = 64 * 1024 * 1024
V7X_VMEM_USABLE_BYTES = V7X_VMEM_BYTES - 6 * 1024 * 1024
COMPILER_SCRATCH_BYTES = 4 * 1024 * 1024


def _nbytes(shape, dtype):
    return int(np.prod(shape)) * jnp.dtype(dtype).itemsize


def _vmem_limit(pipelined, resident):
    need = 2 * sum(pipelined) + sum(resident) + COMPILER_SCRATCH_BYTES
    return int(min(V7X_VMEM_USABLE_BYTES, need))


def _rms(x):
    return x * lax.rsqrt(jnp.mean(x * x, axis=-1, keepdims=True) + NORM_EPS)


def _fold_lanes(x, op):
    tiles = [x[:, i:i + V7X_LANES] for i in range(0, x.shape[1], V7X_LANES)]
    return functools.reduce(op, tiles)


def _dot(a, b):
    return jnp.dot(a, b, preferred_element_type=F32)


def _dot_nt(a, b):
    return lax.dot_general(a, b, (((1,), (1,)), ((), ())), preferred_element_type=F32)


def _dot_tn(a, b):
    return lax.dot_general(a, b, (((0,), (0,)), ((), ())), preferred_element_type=F32)


class _SideCasts:
    def __init__(self, weights, grid):
        self.weights = list(weights)
        self.grid = tuple(grid)
        n_steps = int(np.prod(self.grid))
        self.plans = []
        for w in self.weights:
            rows, n_blocks = w.shape[0], n_steps
            while rows % n_blocks or (rows // n_blocks) % BF16_TILE_ROWS:
                n_blocks -= 1
            self.plans.append((n_blocks, rows // n_blocks))

    def __len__(self):
        return len(self.weights)

    def _specs(self):
        specs = []
        for w, (n_blocks, block_rows) in zip(self.weights, self.plans):
            def index(*ids, n_blocks=n_blocks):
                step = ids[0]
                for extent, idx in zip(self.grid[1:], ids[1:]):
                    step = step * extent + idx
                return (jnp.minimum(step, n_blocks - 1), 0)
            specs.append(pl.BlockSpec((block_rows, w.shape[1]), index))
        return specs

    in_specs = property(_specs)
    out_specs = property(_specs)

    @property
    def out_shapes(self):
        return [jax.ShapeDtypeStruct(w.shape, BF16) for w in self.weights]

    @property
    def window_bytes(self):
        return [_nbytes((rows, w.shape[1]), dt)
                for w, (_, rows) in zip(self.weights, self.plans) for dt in (w.dtype, BF16)]


def _split_refs(refs, n_in, n_side):
    ins, rest = refs[:n_in], refs[n_in:]
    side_in, rest = rest[:n_side], rest[n_side:]
    out, side_out, scratch = rest[0], rest[1:1 + n_side], rest[1 + n_side:]
    return ins, out, scratch, list(zip(side_in, side_out))


def _cast_blocks(pairs):
    for src, dst in pairs:
        dst[...] = src[...].astype(dst.dtype)


def _row_blocks(n_rows):
    step = min(NORM_BLOCK_ROWS, n_rows)
    return [slice(r, r + step) for r in range(0, n_rows, step)]


def _col_chunks(n_cols):
    step = min(MXU_COLS, n_cols)
    return [slice(c, c + step) for c in range(0, n_cols, step)]


def _norm_matmul_kernel(*refs, n_side, cast_w, group_tiles):
    ins, o_ref, scratch, side = _split_refs(refs, 4 if cast_w else 3, n_side)
    x_ref, g_ref, w_ref = ins[:3]
    h_ref = scratch[0]
    wb_ref = scratch[1] if cast_w else w_ref
    step = pl.program_id(1)
    tm = x_ref.shape[0]

    def cast_weight(cols):
        wb_ref[:, cols] = (w_ref[:, cols] * ins[3][:, cols]).astype(wb_ref.dtype)

    for t in range(group_tiles):
        @pl.when(step == t)
        def _(t=t):
            _cast_blocks(side)
            if cast_w and t == 0:
                cast_weight(slice(None))
            for rows in _row_blocks(tm):
                group_rows = slice(t * tm + rows.start, t * tm + rows.stop)
                h = (_rms(x_ref[rows, :]) * g_ref[...]).astype(h_ref.dtype)
                h_ref[group_rows, :] = h
                o_ref[group_rows, :] = _dot(h, wb_ref[...]).astype(o_ref.dtype)

    @pl.when(step >= group_tiles)
    def _():
        _cast_blocks(side)
        if cast_w:
            for cols in _col_chunks(w_ref.shape[1]):
                cast_weight(cols)
                o_ref[:, cols] = _dot(h_ref[...], wb_ref[:, cols]).astype(o_ref.dtype)
        else:
            o_ref[...] = _dot(h_ref[...], w_ref[...]).astype(o_ref.dtype)


def _norm_matmul(x, g, w, out_dtype, *, tm, tn, name, group_tiles=1, col_scale=None,
                 side_weights=()):
    m, d = x.shape
    n = w.shape[1]
    tm, tn = min(tm, m), min(tn, n)
    gt = group_tiles
    grid = (m // (tm * gt), gt + n // tn - 1)
    side = _SideCasts(side_weights, grid)
    cast_w = w.dtype != BF16
    assert cast_w or col_scale is None

    def col_tile(s):
        return jnp.maximum(s - (gt - 1), 0)

    weight_ins, weight_specs = [w], [pl.BlockSpec((d, tn), lambda gi, s: (0, col_tile(s)))]
    if cast_w:
        cs = jnp.ones((n,), F32) if col_scale is None else col_scale
        weight_ins.append(cs.reshape(1, n))
        weight_specs.append(pl.BlockSpec((1, tn), lambda gi, s: (0, col_tile(s))))
    scratch = [pltpu.VMEM((gt * tm, d), BF16)] + ([pltpu.VMEM((d, tn), BF16)] if cast_w else [])
    limit = _vmem_limit(
        [_nbytes((tm, d), x.dtype), _nbytes((d, tn), w.dtype), _nbytes((gt * tm, tn), out_dtype),
         _nbytes((8, d), F32), _nbytes((8, tn), F32)]
        + side.window_bytes,
        [_nbytes((gt * tm, d), BF16), _nbytes((d, tn), BF16) * cast_w],
    )
    outs = pl.pallas_call(
        functools.partial(_norm_matmul_kernel, n_side=len(side), cast_w=cast_w, group_tiles=gt),
        grid=grid,
        in_specs=[
            pl.BlockSpec((tm, d), lambda gi, s: (gi * gt + jnp.minimum(s, gt - 1), 0)),
            pl.BlockSpec((1, d), lambda gi, s: (0, 0)),
        ] + weight_specs + side.in_specs,
        out_specs=[pl.BlockSpec((gt * tm, tn), lambda gi, s: (gi, col_tile(s)))] + side.out_specs,
        out_shape=[jax.ShapeDtypeStruct((m, n), out_dtype)] + side.out_shapes,
        scratch_shapes=scratch,
        compiler_params=pltpu.CompilerParams(
            dimension_semantics=("arbitrary", "arbitrary"), vmem_limit_bytes=limit
        ),
        name=name,
    )(x, g.reshape(1, d), *weight_ins, *side_weights)
    return tuple(outs)


def _norm_matmul_pair_kernel(x_ref, g_ref, w1_ref, w2_ref, o1_ref, o2_ref, h_ref):
    @pl.when(pl.program_id(1) == 0)
    def _():
        h_ref[...] = (_rms(x_ref[...]) * g_ref[...]).astype(h_ref.dtype)

    for w_ref, o_ref in ((w1_ref, o1_ref), (w2_ref, o2_ref)):
        for cols in _col_chunks(w_ref.shape[1]):
            o_ref[:, cols] = _dot(h_ref[...], w_ref[:, cols].astype(BF16)).astype(o_ref.dtype)


def _norm_matmul_pair(x, g, w1, w2, out_dtype, *, tm, tn, name):
    m, d = x.shape
    n = w1.shape[1]
    assert w1.shape == w2.shape
    tm, tn = min(tm, m), min(tn, n)
    w_spec = pl.BlockSpec((d, tn), lambda i, j: (0, j))
    o_spec = pl.BlockSpec((tm, tn), lambda i, j: (i, j))
    limit = _vmem_limit(
        [_nbytes((tm, d), x.dtype)] + [_nbytes((d, tn), w1.dtype), _nbytes((tm, tn), out_dtype)] * 2,
        [_nbytes((tm, d), BF16), _nbytes((tm, d), F32)],
    )
    return pl.pallas_call(
        _norm_matmul_pair_kernel,
        grid=(m // tm, n // tn),
        in_specs=[pl.BlockSpec((tm, d), lambda i, j: (i, 0)), pl.BlockSpec((1, d), lambda i, j: (0, 0)),
                  w_spec, w_spec],
        out_specs=[o_spec, o_spec],
        out_shape=[jax.ShapeDtypeStruct((m, n), out_dtype)] * 2,
        scratch_shapes=[pltpu.VMEM((tm, d), BF16)],
        compiler_params=pltpu.CompilerParams(
            dimension_semantics=("arbitrary", "arbitrary"), vmem_limit_bytes=limit
        ),
        name=name,
    )(x, g.reshape(1, d), w1, w2)


def _matmul2_res_kernel(a1_ref, a2_ref, w1_ref, w2_ref, res_ref, o_ref):
    acc = _dot(a1_ref[...], w1_ref[...]) + _dot(a2_ref[...], w2_ref[...])
    o_ref[...] = res_ref[...] + acc


def _matmul2_res(a1, a2, w, res, *, tm, name):
    m, k1 = a1.shape
    k2 = a2.shape[1]
    assert k1 == k2 and w.shape[0] == k1 + k2
    n = w.shape[1]
    limit = _vmem_limit(
        [_nbytes((tm, k1), a1.dtype), _nbytes((tm, k2), a2.dtype),
         _nbytes((tm, n), F32), _nbytes((tm, n), F32)],
        [_nbytes(w.shape, w.dtype), _nbytes((tm, n), F32)],
    )
    resident = pl.Buffered(1)
    return pl.pallas_call(
        _matmul2_res_kernel,
        grid=(m // tm,),
        in_specs=[
            pl.BlockSpec((tm, k1), lambda i: (i, 0)),
            pl.BlockSpec((tm, k2), lambda i: (i, 0)),
            pl.BlockSpec((k1, n), lambda i: (0, 0), pipeline_mode=resident),
            pl.BlockSpec((k2, n), lambda i: (1, 0), pipeline_mode=resident),
            pl.BlockSpec((tm, n), lambda i: (i, 0)),
        ],
        out_specs=pl.BlockSpec((tm, n), lambda i: (i, 0)),
        out_shape=jax.ShapeDtypeStruct((m, n), F32),
        compiler_params=pltpu.CompilerParams(
            dimension_semantics=("arbitrary",), vmem_limit_bytes=limit
        ),
        name=name,
    )(a1, a2, w, w, res)


def _bf16_part(x):
    bits = lax.bitcast_convert_type(x, jnp.int32) & jnp.int32(-65536)
    return lax.bitcast_convert_type(bits, F32)


def _diff_attn_body(ins, o_ref, scratch, *, tq, lam_init):
    slope_ref, lq1_ref, lk1_ref, lq2_ref, lk2_ref, g_ref, q_ref, k_ref, v_ref = ins
    kx_ref, s_ref = scratch
    s_len = q_ref.shape[0]
    d = DA_HEAD_DIM
    lam = (jnp.exp(jnp.sum(lq1_ref[...] * lk1_ref[...], axis=-1, keepdims=True))
           - jnp.exp(jnp.sum(lq2_ref[...] * lk2_ref[...], axis=-1, keepdims=True))
           + lam_init)

    lane = lax.broadcasted_iota(jnp.int32, (1, d), 1)
    slope2 = jnp.full((1, d), slope_ref[pl.program_id(1)] * LOG2_E, F32)
    piece_hi = _bf16_part(slope2)
    rest = slope2 - piece_hi
    piece_mid = _bf16_part(rest)
    piece_lo = _bf16_part(rest - piece_mid)
    piece = jnp.where((lane == 0) | (lane == 3), piece_hi,
                      jnp.where((lane == 1) | (lane == 4), piece_mid, piece_lo))
    q_extra = jnp.where(lane < 3, piece * 256.0, jnp.where(lane < 6, piece, 0.0))
    q_extra = jnp.broadcast_to(q_extra, (tq, d)).astype(BF16)
    kpos = lax.broadcasted_iota(jnp.int32, (s_len, d), 0)
    klane = lax.broadcasted_iota(jnp.int32, (s_len, d), 1)
    k_extra = jnp.where(klane < 3, kpos >> 8, jnp.where(klane < 6, kpos & 255, 0))
    kx_ref[...] = k_extra.astype(F32).astype(kx_ref.dtype)

    row = lax.broadcasted_iota(jnp.int32, (tq, tq), 0)
    col = lax.broadcasted_iota(jnp.int32, (tq, tq), 1)
    causal = col <= row

    for qi in reversed(range(s_len // tq)):
        lo, hi = qi * tq, (qi + 1) * tq
        key_blocks = [slice(j * tq, (j + 1) * tq) for j in range(qi + 1)]
        heads = []
        for c in range(2):
            dcols = slice(c * d, (c + 1) * d)
            q_aug = jnp.concatenate([q_ref[lo:hi, dcols], q_extra], axis=1)
            m = None
            for j, cols in enumerate(key_blocks):
                k_aug = jnp.concatenate([k_ref[cols, dcols], kx_ref[cols, :]], axis=1)
                s = _dot_nt(q_aug, k_aug)
                if j == qi:
                    s = jnp.where(causal, s, NEG_INF)
                s_ref[c, :, cols] = s
                bm = _fold_lanes(s, jnp.maximum)
                m = bm if m is None else jnp.maximum(m, bm)
            m = jnp.max(m, axis=-1, keepdims=True)
            l = acc = None
            for cols in key_blocks:
                p = jnp.exp2(s_ref[c, :, cols] - m)
                bl = _fold_lanes(p, jnp.add)
                l = bl if l is None else l + bl
                part = _dot(p.astype(BF16), v_ref[cols, :])
                acc = part if acc is None else acc + part
            l = jnp.sum(l, axis=-1, keepdims=True)
            heads.append((acc, l))
        (acc1, l1), (acc2, l2) = heads
        out = acc1 * (1.0 / l1) - acc2 * (lam / l2)
        y = _rms(out) * g_ref[...] * (1.0 - lam_init)
        o_ref[lo:hi, :] = y.astype(o_ref.dtype)
        yield


def _retention_body(ins, o_ref):
    lg_ref, q_ref, k_ref, v_ref, gate_ref = ins
    s_len = q_ref.shape[0]
    c = RET_CHUNK
    dk, dv = RET_QK_DIM, RET_V_DIM
    scale = dk ** -0.5
    lg = lg_ref[pl.program_id(1)]

    row = lax.broadcasted_iota(jnp.int32, (c, c), 0)
    col = lax.broadcasted_iota(jnp.int32, (c, c), 1)
    diff = (row - col).astype(F32)
    intra = jnp.where(diff >= 0, jnp.exp(lg * jnp.maximum(diff, 0.0)), 0.0) * scale
    row_k = lax.broadcasted_iota(jnp.int32, (c, dk), 0).astype(F32)
    k_decay = jnp.exp(lg * (float(c - 1) - row_k)) * scale
    row_v = lax.broadcasted_iota(jnp.int32, (c, dv), 0).astype(F32)
    q_decay = jnp.exp(lg * (row_v + 1.0))
    chunk_decay = jnp.exp(jnp.full((1, dv), lg * float(c), F32))

    state = jnp.zeros((dk, dv), F32)
    for i in range(s_len // c):
        lo, hi = i * c, (i + 1) * c
        q = q_ref[lo:hi, :]
        k = k_ref[lo:hi, :]
        v = v_ref[lo:hi, :]
        scores = _dot_nt(q, k) * intra
        y = _dot(scores.astype(BF16), v)
        if i > 0:
            y = y + _dot(q, state.astype(BF16)) * q_decay
        if i + 1 < s_len // c:
            kd = (k.astype(F32) * k_decay).astype(BF16)
            state = state * chunk_decay + _dot_tn(kd, v)
        gate = gate_ref[lo:hi, :].astype(F32)
        o_ref[lo:hi, :] = (_rms(y) * (gate * jax.nn.sigmoid(gate))).astype(o_ref.dtype)
        yield


N_ATTN_INS, N_RET_INS = 9, 5


def _mixers_kernel(*refs, n_side, tq, lam_init):
    ins, rest = refs[:N_ATTN_INS + N_RET_INS], refs[N_ATTN_INS + N_RET_INS:]
    side_in, rest = rest[:n_side], rest[n_side:]
    da_ref, ret_ref = rest[:2]
    side_out, scratch = rest[2:2 + n_side], rest[2 + n_side:]
    _cast_blocks(zip(side_in, side_out))
    pending = [_retention_body(ins[N_ATTN_INS:], ret_ref),
               _diff_attn_body(ins[:N_ATTN_INS], da_ref, scratch, tq=tq, lam_init=lam_init)]
    while pending:
        pending = [body for body in pending if next(body, StopIteration) is not StopIteration]


def _mixers(proj, slopes, lq1, lk1, lq2, lk2, g, log_gammas, *, batch, seq, lam_init, tq,
            side_weights=()):
    assert DA_HEADS == RET_HEADS
    hb = DA_V_DIM
    q_blk0, k_blk0, v_blk0 = 0, DA_HEADS, 2 * DA_HEADS
    da_cols = 3 * DA_HEADS * DA_V_DIM
    rq_blk0 = da_cols // RET_QK_DIM
    rk_blk0 = rq_blk0 + RET_HEADS
    rv_blk0 = (da_cols + 2 * RET_HEADS * RET_QK_DIM) // RET_V_DIM
    rg_blk0 = rv_blk0 + RET_HEADS
    grid = (batch, DA_HEADS)
    side = _SideCasts(side_weights, grid)
    scratch = [
        pltpu.VMEM((seq, DA_HEAD_DIM), BF16),
        pltpu.VMEM((2, tq, seq), F32),
    ]
    limit = _vmem_limit(
        [_nbytes((seq, hb), proj.dtype)] * 4
        + [_nbytes((seq, RET_QK_DIM), proj.dtype)] * 2 + [_nbytes((seq, RET_V_DIM), proj.dtype)] * 3
        + side.window_bytes,
        [_nbytes((seq, DA_HEAD_DIM), BF16),
         3 * _nbytes((2, tq, seq), F32), 3 * _nbytes((2, tq, seq), BF16)],
    )
    vec = pl.BlockSpec((1, DA_HEAD_DIM), lambda b, h: (0, 0))
    smem = pl.BlockSpec(memory_space=pltpu.SMEM)

    def head_cols(width, blk0):
        return pl.BlockSpec((seq, width), lambda b, h: (b, blk0 + h))

    outs = pl.pallas_call(
        functools.partial(_mixers_kernel, n_side=len(side), tq=tq, lam_init=lam_init),
        grid=grid,
        in_specs=[
            smem, vec, vec, vec, vec,
            pl.BlockSpec((1, hb), lambda b, h: (0, 0)),
            head_cols(hb, q_blk0), head_cols(hb, k_blk0), head_cols(hb, v_blk0),
            smem,
            head_cols(RET_QK_DIM, rq_blk0), head_cols(RET_QK_DIM, rk_blk0),
            head_cols(RET_V_DIM, rv_blk0), head_cols(RET_V_DIM, rg_blk0),
        ] + side.in_specs,
        out_specs=[head_cols(hb, 0), head_cols(RET_V_DIM, 0)] + side.out_specs,
        out_shape=[jax.ShapeDtypeStruct((batch * seq, DA_HEADS * hb), BF16),
                   jax.ShapeDtypeStruct((batch * seq, RET_HEADS * RET_V_DIM), BF16)] + side.out_shapes,
        scratch_shapes=scratch,
        compiler_params=pltpu.CompilerParams(
            dimension_semantics=("arbitrary", "arbitrary"), vmem_limit_bytes=limit
        ),
        name="token_mixers",
    )(slopes, lq1.reshape(1, -1), lk1.reshape(1, -1), lq2.reshape(1, -1), lk2.reshape(1, -1),
      g.reshape(1, hb), proj, proj, proj, log_gammas, proj, proj, proj, proj, *side_weights)
    return tuple(outs)


def _xattn_kernel(*refs, n_side):
    ins, o_ref, (xo_ref,), side = _split_refs(refs, 5, n_side)
    xq_ref, xk_ref, xv_ref, res_ref, wo_ref = ins
    _cast_blocks(side)
    d_model = xq_ref.shape[1]
    hd = d_model // XATTN_HEADS
    scale2 = hd ** -0.5 * LOG2_E
    for h in range(XATTN_HEADS):
        cols = slice(h * hd, (h + 1) * hd)
        s = _dot_nt(xq_ref[:, cols], xk_ref[:, cols]) * scale2
        p = jnp.exp2(s - jnp.max(s, axis=-1, keepdims=True))
        p = p * (1.0 / jnp.sum(p, axis=-1, keepdims=True))
        xo_ref[:, cols] = _dot(p.astype(BF16), xv_ref[:, cols]).astype(xo_ref.dtype)
    o_ref[...] = res_ref[...] + _dot(xo_ref[...], wo_ref[...])


def _cross_attention(xq, xk, xv, res, wo, *, batch, seq, mem_len, tq, side_weights=()):
    d = xq.shape[1]
    nq = seq // tq
    grid = (batch, nq)
    side = _SideCasts(side_weights, grid)
    limit = _vmem_limit(
        [
            _nbytes((tq, d), xq.dtype),
            _nbytes((mem_len, d), xk.dtype),
            _nbytes((mem_len, d), xv.dtype),
            _nbytes((tq, d), F32),
            _nbytes((d, d), wo.dtype),
            _nbytes((tq, d), F32),
        ] + side.window_bytes,
        [_nbytes((tq, d), BF16), _nbytes((tq, d), F32)],
    )
    outs = pl.pallas_call(
        functools.partial(_xattn_kernel, n_side=len(side)),
        grid=grid,
        in_specs=[
            pl.BlockSpec((tq, d), lambda b, i: (b * nq + i, 0)),
            pl.BlockSpec((mem_len, d), lambda b, i: (b, 0)),
            pl.BlockSpec((mem_len, d), lambda b, i: (b, 0)),
            pl.BlockSpec((tq, d), lambda b, i: (b * nq + i, 0)),
            pl.BlockSpec((d, d), lambda b, i: (0, 0)),
        ] + side.in_specs,
        out_specs=[pl.BlockSpec((tq, d), lambda b, i: (b * nq + i, 0))] + side.out_specs,
        out_shape=[jax.ShapeDtypeStruct((batch * seq, d), F32)] + side.out_shapes,
        scratch_shapes=[pltpu.VMEM((tq, d), BF16)],
        compiler_params=pltpu.CompilerParams(
            dimension_semantics=("arbitrary", "arbitrary"), vmem_limit_bytes=limit
        ),
        name="cross_attention",
    )(xq, xk, xv, res, wo, *side_weights)
    return tuple(outs)


def _ffn_kernel(x_ref, g_ref, wg_ref, wu_ref, wd_ref, gf_ref, o_ref, h_ref, *, final_norm):
    f = pl.program_id(1)
    last = pl.num_programs(1) - 1
    blocks = _row_blocks(x_ref.shape[0])

    def partial_ffn(h, chunked=False):
        hidden = wg_ref.shape[1]
        out = None
        for cols in (_col_chunks(hidden) if chunked else [slice(0, hidden)]):
            gate = _dot(h, wg_ref[:, cols])
            up = _dot(h, wu_ref[:, cols])
            act = (gate * jax.nn.sigmoid(gate)) * up
            part = _dot(act.astype(BF16), wd_ref[cols, :])
            out = part if out is None else out + part
        return out

    @pl.when(f == 0)
    def _():
        for rows in blocks:
            x = x_ref[rows, :]
            h = (_rms(x) * g_ref[...]).astype(h_ref.dtype)
            h_ref[rows, :] = h
            o_ref[rows, :] = x + partial_ffn(h)

    if final_norm:
        @pl.when(jnp.logical_and(f > 0, f < last))
        def _():
            o_ref[...] += partial_ffn(h_ref[...], chunked=True)

        @pl.when(f == last)
        def _():
            for rows in blocks:
                y = o_ref[rows, :] + partial_ffn(h_ref[rows, :])
                o_ref[rows, :] = _rms(y) * gf_ref[...]
    else:
        @pl.when(f > 0)
        def _():
            o_ref[...] += partial_ffn(h_ref[...], chunked=True)


def _ffn(x, g, wg, wu, wd, gf, *, final_norm, tm, tf):
    m, d = x.shape
    d_ff = wg.shape[1]
    limit = _vmem_limit(
        [
            _nbytes((tm, d), F32),
            _nbytes((d, tf), wg.dtype),
            _nbytes((d, tf), wu.dtype),
            _nbytes((tf, d), wd.dtype),
            _nbytes((tm, d), F32),
        ],
        [_nbytes((tm, d), BF16), _nbytes((tm, d), F32), 4 * _nbytes((tm, tf), F32)],
    )
    return pl.pallas_call(
        functools.partial(_ffn_kernel, final_norm=final_norm),
        grid=(m // tm, d_ff // tf),
        in_specs=[
            pl.BlockSpec((tm, d), lambda i, f: (i, 0)),
            pl.BlockSpec((1, d), lambda i, f: (0, 0)),
            pl.BlockSpec((d, tf), lambda i, f: (0, f)),
            pl.BlockSpec((d, tf), lambda i, f: (0, f)),
            pl.BlockSpec((tf, d), lambda i, f: (f, 0)),
            pl.BlockSpec((1, d), lambda i, f: (0, 0)),
        ],
        out_specs=pl.BlockSpec((tm, d), lambda i, f: (i, 0)),
        out_shape=jax.ShapeDtypeStruct((m, d), F32),
        scratch_shapes=[pltpu.VMEM((tm, d), BF16)],
        compiler_params=pltpu.CompilerParams(
            dimension_semantics=("parallel", "arbitrary"), vmem_limit_bytes=limit
        ),
        name="swiglu_ffn",
    )(x, g.reshape(1, d), wg, wu, wd, gf.reshape(1, d))


def kernel(x, mem, norm_mix_g, w_in, lambda_q1, lambda_k1, lambda_q2, lambda_k2, da_subln_g, w_o, norm_x_g, norm_mem_g, w_xq, w_xk, w_xv, w_xo, norm_ffn_g, w_gate, w_up, w_down, norm_f_g):
    batch, seq, d_model = x.shape
    mem_len = mem.shape[1]
    depth = w_in.shape[0]
    slopes = jnp.asarray(2.0 ** (-8.0 * np.arange(1, DA_HEADS + 1) / DA_HEADS), dtype=F32)
    log_gammas = jnp.asarray(np.log(1.0 - 2.0 ** (-5.0 - np.arange(RET_HEADS))), dtype=F32)

    xf = x.reshape(batch * seq, d_model)
    memf = mem.reshape(batch * mem_len, d_model)
    for l in range(depth):
        lam_init = 0.8 - 0.6 * math.exp(-0.3 * l)
        n_dq = DA_HEADS * 2 * DA_HEAD_DIM
        col_scale = jnp.where(jnp.arange(w_in.shape[2]) < n_dq, DA_HEAD_DIM ** -0.5 * LOG2_E, 1.0)
        proj, = _norm_matmul(
            xf, norm_mix_g[l], w_in[l], BF16, tm=ROW_TILE, tn=COL_TILE,
            group_tiles=IN_PROJ_GROUP_TILES, name="in_proj", col_scale=col_scale.astype(F32))
        da, ret, wb_up, wb_o, wb_xq, wb_xo = _mixers(
            proj, slopes, lambda_q1[l], lambda_k1[l], lambda_q2[l], lambda_k2[l], da_subln_g[l],
            log_gammas, batch=batch, seq=seq, lam_init=lam_init, tq=ATTN_Q_TILE,
            side_weights=(w_up[l], w_o[l], w_xq[l], w_xo[l]))
        xf = _matmul2_res(da, ret, wb_o, xf, tm=OUT_PROJ_ROW_TILE, name="out_proj")

        xq, wb_gate = _norm_matmul(xf, norm_x_g[l], wb_xq, BF16, tm=ROW_TILE, tn=d_model,
                                   name="xattn_q", side_weights=(w_gate[l],))
        xk, xv = _norm_matmul_pair(memf, norm_mem_g[l], w_xk[l], w_xv[l], BF16, tm=ROW_TILE,
                                   tn=KV_COL_TILE, name="xattn_kv")
        xf, wb_down = _cross_attention(xq, xk, xv, xf, wb_xo, batch=batch, seq=seq, mem_len=mem_len,
                                       tq=XATTN_Q_TILE, side_weights=(w_down[l],))

        xf = _ffn(xf, norm_ffn_g[l], wb_gate, wb_up, wb_down, norm_f_g,
                  final_norm=(l == depth - 1), tm=FFN_ROW_TILE, tf=FFN_COL_TILE)
    return xf.reshape(batch, seq, d_model)
```
